```python
import math
import jax, jax.numpy as jnp
from jax import lax
import numpy as np


D_MODEL = 1024
BATCH = 32
SEQ = 256
DEPTH = 2
DEC_BATCH = 8
DEC_SEQ = 1024
PAST_LEN = 512

GRID_W = 64
D_CONV = 512
N_HEADS = 8
HEAD_DIM = 64
V_DIM = 2 * HEAD_DIM
ATTN_W = N_HEADS * V_DIM
AXIS_DIM = HEAD_DIM // 2
ROPE_BASE = 10000.0
Q_BLOCK = 128
D_FF = 2816
N_EXPERTS = 8
TOP_K = 2
D_FF_EXPERT = 1408
N_DENSE = (DEPTH + 1) // 2
N_MOE = DEPTH // 2
ALPHA = (2 * DEPTH) ** 0.25
BETA = (8 * DEPTH) ** -0.25
LN_EPS = 1e-5
D_IN = 3 * D_CONV + 3 * ATTN_W + 2 * D_MODEL
IN_SPLITS = [D_CONV, 2 * D_CONV, 3 * D_CONV, 3 * D_CONV + ATTN_W, 3 * D_CONV + 2 * ATTN_W,
             3 * D_CONV + 3 * ATTN_W, 3 * D_CONV + 3 * ATTN_W + D_MODEL]

kernel_name = 'hybrid_diffusion_conv_diffattn_step'


def layer_norm(x, g, b):
    xf = x.astype(jnp.float32)
    mu = jnp.mean(xf, axis=-1, keepdims=True)
    var = jnp.mean(jnp.square(xf - mu), axis=-1, keepdims=True)
    return ((xf - mu) * lax.rsqrt(var + LN_EPS) * g + b).astype(x.dtype)


def rms_norm(x, g):
    xf = x.astype(jnp.float32)
    return (xf * lax.rsqrt(jnp.mean(jnp.square(xf), axis=-1, keepdims=True) + LN_EPS) * g).astype(x.dtype)


def modulate(x, shift, scale):
    return x * (1 + scale) + shift


def axial_angles(n_tokens):
    rows = n_tokens // GRID_W
    row = jnp.repeat(jnp.arange(rows, dtype=jnp.float32), GRID_W)
    col = jnp.tile(jnp.arange(GRID_W, dtype=jnp.float32), rows)
    inv_freq = ROPE_BASE ** (-jnp.arange(0, AXIS_DIM, 2, dtype=jnp.float32) / AXIS_DIM)
    return row[:, None] * inv_freq, col[:, None] * inv_freq


def _rot(x, cos, sin):
    x1, x2 = jnp.split(x, 2, axis=-1)
    return jnp.concatenate([x1 * cos - x2 * sin, x1 * sin + x2 * cos], axis=-1)


def axial_rope(x, ang_r, ang_c):
    shp = x.shape
    xs = x.reshape(shp[:-1] + (2, HEAD_DIM))
    cr = jnp.cos(ang_r)[:, None, :].astype(x.dtype)
    sr = jnp.sin(ang_r)[:, None, :].astype(x.dtype)
    cc = jnp.cos(ang_c)[:, None, :].astype(x.dtype)
    sc = jnp.sin(ang_c)[:, None, :].astype(x.dtype)
    xr = _rot(xs[..., :AXIS_DIM], cr, sr)
    xc = _rot(xs[..., AXIS_DIM:], cc, sc)
    return jnp.concatenate([xr, xc], axis=-1).reshape(shp)


def short_conv(u, w, b):
    up = jnp.pad(u, ((0, 0), (1, 1), (0, 0)))
    return up[:, :-2] * w[0] + up[:, 1:-1] * w[1] + up[:, 2:] * w[2] + b


def diff_attention(q, k, v, lam):
    b, h, lq, _ = q.shape
    nb = lq // Q_BLOCK
    k1, k2 = k[..., :HEAD_DIM], k[..., HEAD_DIM:]
    scale = HEAD_DIM ** -0.5
    qb = q.reshape(b, h, nb, Q_BLOCK, 2 * HEAD_DIM).transpose(2, 0, 1, 3, 4)

    def one_block(qblk):
        s1 = jnp.einsum('bhqd,bhkd->bhqk', qblk[..., :HEAD_DIM], k1).astype(jnp.float32) * scale
        s2 = jnp.einsum('bhqd,bhkd->bhqk', qblk[..., HEAD_DIM:], k2).astype(jnp.float32) * scale
        p = jax.nn.softmax(s1, axis=-1) - lam * jax.nn.softmax(s2, axis=-1)
        return jnp.einsum('bhqk,bhkd->bhqd', p.astype(v.dtype), v)

    out = lax.map(one_block, qb)
    return out.transpose(1, 2, 0, 3, 4).reshape(b, h, lq, V_DIM)


def token_mixers(h, w_in_l, conv_w_l, conv_b_l, w_conv_out_l, subln_g_l, w_attn_out_l, w_out_l,
                 lam, lam_init, ang_r, ang_c, ctx_k, ctx_v):
    b, n, _ = h.shape
    gate_b, gate_c, u, q, k, v, g_conv, g_attn = jnp.split(h @ w_in_l, IN_SPLITS, axis=-1)
    y_conv = (gate_b * short_conv(gate_c * u, conv_w_l, conv_b_l)) @ w_conv_out_l
    q = q.reshape(b, n, N_HEADS, V_DIM).transpose(0, 2, 1, 3)
    k = k.reshape(b, n, N_HEADS, V_DIM).transpose(0, 2, 1, 3)
    v = v.reshape(b, n, N_HEADS, V_DIM).transpose(0, 2, 1, 3)
    if ang_r is not None:
        q = axial_rope(q, ang_r, ang_c)
        k = axial_rope(k, ang_r, ang_c)
    if ctx_k is not None:
        k_all = jnp.concatenate([ctx_k.astype(k.dtype), k], axis=2)
        v_all = jnp.concatenate([ctx_v.astype(v.dtype), v], axis=2)
    else:
        k_all, v_all = k, v
    o = rms_norm(diff_attention(q, k_all, v_all, lam), subln_g_l) * (1 - lam_init)
    y_attn = o.transpose(0, 2, 1, 3).reshape(b, n, ATTN_W) @ w_attn_out_l
    merged = jax.nn.sigmoid(g_conv) * y_conv + jax.nn.sigmoid(g_attn) * y_attn
    return merged @ w_out_l, k, v


def swiglu(h, wg, wu, wd):
    return (jax.nn.silu(h @ wg) * (h @ wu)) @ wd


def moe_swiglu(h, router, wg, wu, wd):
    probs = jax.nn.softmax((h @ router).astype(jnp.float32), axis=-1)
    top_v, top_i = lax.top_k(probs, TOP_K)
    top_v = top_v / jnp.sum(top_v, axis=-1, keepdims=True)
    gates = jnp.sum(jax.nn.one_hot(top_i, N_EXPERTS, dtype=jnp.float32) * top_v[..., None], axis=-2).astype(h.dtype)
    out = jnp.zeros_like(h)
    for e in range(N_EXPERTS):
        out = out + gates[..., e:e + 1] * swiglu(h, wg[e], wu[e], wd[e])
    return out


def setup_inputs(seed: int = 0) -> dict:
    key = jax.random.key(seed)
    ks = iter(jax.random.split(key, 40))

    def nrm(shape, scale):
        return jax.random.normal(next(ks), shape, jnp.float32) * scale

    d = D_MODEL
    return {
        'x_prompt': nrm((BATCH, SEQ, d), 1.0),
        'x_sample': nrm((DEC_BATCH, DEC_SEQ, d), 1.0),
        'cache_k': nrm((DEC_BATCH, DEPTH, N_HEADS, PAST_LEN, V_DIM), 1.0),
        'cache_v': nrm((DEC_BATCH, DEPTH, N_HEADS, PAST_LEN, V_DIM), 1.0),
        'c': nrm((DEC_BATCH, d), 1.0),
        'c_ctx': nrm((d,), 1.0),
        'ln_in_g': 1.0 + nrm((d,), 0.02),
        'ln_in_b': nrm((d,), 0.02),
        'ada_w': nrm((DEPTH, d, 6 * d), 0.5 * d ** -0.5),
        'ada_b': nrm((DEPTH, 6 * d), 0.02),
        'w_in': nrm((DEPTH, d, D_IN), d ** -0.5),
        'conv_w': nrm((DEPTH, 3, D_CONV), 0.5),
        'conv_b': nrm((DEPTH, D_CONV), 0.02),
        'w_conv_out': nrm((DEPTH, D_CONV, d), D_CONV ** -0.5),
        'lam_q1': nrm((DEPTH, HEAD_DIM), 0.1),
        'lam_k1': nrm((DEPTH, HEAD_DIM), 0.1),
        'lam_q2': nrm((DEPTH, HEAD_DIM), 0.1),
        'lam_k2': nrm((DEPTH, HEAD_DIM), 0.1),
        'subln_g': 1.0 + nrm((DEPTH, V_DIM), 0.02),
        'w_attn_out': nrm((DEPTH, ATTN_W, d), ATTN_W ** -0.5),
        'w_out': nrm((DEPTH, d, d), BETA * d ** -0.5),
        'ln1_g': 1.0 + nrm((DEPTH, d), 0.02),
        'ln1_b': nrm((DEPTH, d), 0.02),
        'ln2_g': 1.0 + nrm((DEPTH, d), 0.02),
        'ln2_b': nrm((DEPTH, d), 0.02),
        'ffn_w_gate': nrm((N_DENSE, d, D_FF), d ** -0.5),
        'ffn_w_up': nrm((N_DENSE, d, D_FF), d ** -0.5),
        'ffn_w_down': nrm((N_DENSE, D_FF, d), BETA * D_FF ** -0.5),
        'moe_router': nrm((N_MOE, d, N_EXPERTS), d ** -0.5),
        'moe_w_gate': nrm((N_MOE, N_EXPERTS, d, D_FF_EXPERT), d ** -0.5),
        'moe_w_up': nrm((N_MOE, N_EXPERTS, d, D_FF_EXPERT), d ** -0.5),
        'moe_w_down': nrm((N_MOE, N_EXPERTS, D_FF_EXPERT, d), BETA * D_FF_EXPERT ** -0.5),
    }


def reference(x_prompt, x_sample, cache_k, cache_v, c, c_ctx, ln_in_g, ln_in_b, ada_w, ada_b,
              w_in, conv_w, conv_b, w_conv_out, lam_q1, lam_k1, lam_q2, lam_k2, subln_g,
              w_attn_out, w_out, ln1_g, ln1_b, ln2_g, ln2_b, ffn_w_gate, ffn_w_up, ffn_w_down,
              moe_router, moe_w_gate, moe_w_up, moe_w_down):
    ang_r, ang_c = axial_angles(x_sample.shape[1])
    xp = layer_norm(x_prompt, ln_in_g, ln_in_b)
    xs = layer_norm(x_sample, ln_in_g, ln_in_b)
    new_k, new_v = [], []
    for l in range(DEPTH):
        lam_init = 0.8 - 0.6 * math.exp(-0.3 * l)
        lam = (jnp.exp(jnp.sum(lam_q1[l].astype(jnp.float32) * lam_k1[l].astype(jnp.float32)))
               - jnp.exp(jnp.sum(lam_q2[l].astype(jnp.float32) * lam_k2[l].astype(jnp.float32))) + lam_init)
        mp = jnp.split(jax.nn.silu(c_ctx) @ ada_w[l] + ada_b[l], 6, axis=-1)
        ms = [m[:, None, :] for m in jnp.split(jax.nn.silu(c) @ ada_w[l] + ada_b[l], 6, axis=-1)]
        mix_w = (w_in[l], conv_w[l], conv_b[l], w_conv_out[l], subln_g[l], w_attn_out[l], w_out[l], lam, lam_init)
        m_p, k_p, v_p = token_mixers(modulate(xp, mp[0], mp[1]), *mix_w, None, None, None, None)
        new_k.append(k_p)
        new_v.append(v_p)
        xp = layer_norm(ALPHA * xp + mp[2] * m_p, ln1_g[l], ln1_b[l])
        m_s, _, _ = token_mixers(modulate(xs, ms[0], ms[1]), *mix_w, ang_r, ang_c, cache_k[:, l], cache_v[:, l])
        xs = layer_norm(ALPHA * xs + ms[2] * m_s, ln1_g[l], ln1_b[l])
        hp = modulate(xp, mp[3], mp[4])
        hs = modulate(xs, ms[3], ms[4])
        if l % 2 == 0:
            i = l // 2
            f_p = swiglu(hp, ffn_w_gate[i], ffn_w_up[i], ffn_w_down[i])
            f_s = swiglu(hs, ffn_w_gate[i], ffn_w_up[i], ffn_w_down[i])
        else:
            i = l // 2
            f_p = moe_swiglu(hp, moe_router[i], moe_w_gate[i], moe_w_up[i], moe_w_down[i])
            f_s = moe_swiglu(hs, moe_router[i], moe_w_gate[i], moe_w_up[i], moe_w_down[i])
        xp = layer_norm(ALPHA * xp + mp[5] * f_p, ln2_g[l], ln2_b[l])
        xs = layer_norm(ALPHA * xs + ms[5] * f_s, ln2_g[l], ln2_b[l])
    new_cache_k = jnp.stack(new_k, axis=1)
    new_cache_v = jnp.stack(new_v, axis=1)
    return (xp, xs, new_cache_k, new_cache_v)
```

```python
import functools
import math

import jax
import jax.numpy as jnp
from jax import lax
from jax.experimental import pallas as pl
from jax.experimental.pallas import tpu as pltpu

D_MODEL = 1024
BATCH = 32
SEQ = 256
DEPTH = 2
DEC_BATCH = 8
DEC_SEQ = 1024
PAST_LEN = 512
GRID_W = 64
D_CONV = 512
N_HEADS = 8
HEAD_DIM = 64
V_DIM = 2 * HEAD_DIM
ATTN_W = N_HEADS * V_DIM
AXIS_DIM = HEAD_DIM // 2
ROPE_BASE = 10000.0
D_FF = 2816
N_EXPERTS = 8
D_FF_EXPERT = 1408
ALPHA = (2 * DEPTH) ** 0.25
LN_EPS = 1e-5
QK_SCALE = HEAD_DIM ** -0.5

NP_TOK = BATCH * SEQ
NS_TOK = DEC_BATCH * DEC_SEQ
N_TOK = NP_TOK + NS_TOK
MOD_ROWS = 16
CTX_ROW = DEC_BATCH
LANES = 128
VMEM_LIMIT = 56 * 1024 * 1024

F32 = jnp.float32
BF16 = jnp.bfloat16


def _params(n_axes, vmem=VMEM_LIMIT):
    return pltpu.CompilerParams(dimension_semantics=("arbitrary",) * n_axes, vmem_limit_bytes=vmem)


def _resident(shape):
    return pl.BlockSpec(shape, lambda *_: (0,) * len(shape), pipeline_mode=pl.Buffered(1))


def _mod_row(i, tm):
    n_p = NP_TOK // tm
    return jnp.where(i < n_p, CTX_ROW, (i - n_p) // (DEC_SEQ // tm))


def _mod_spec(tm):
    return pl.BlockSpec((1, 6, D_MODEL), lambda i, *_: (_mod_row(i, tm), 0, 0))


def _layer_norm(x, g, b):
    mu = jnp.mean(x, axis=-1, keepdims=True)
    xc = x - mu
    var = jnp.mean(xc * xc, axis=-1, keepdims=True)
    return xc * lax.rsqrt(var + LN_EPS) * g + b


def _ada_kernel(c_ref, w_ref, b_ref, o_ref):
    c = c_ref[...]
    a = (c * jax.nn.sigmoid(c)).astype(BF16)
    o_ref[0] = jnp.dot(a, w_ref[0].astype(BF16), preferred_element_type=F32) + b_ref[0]


def _ada(cvec, ada_w, ada_b):
    tn = 1024
    return pl.pallas_call(
        _ada_kernel,
        grid=(DEPTH, 6 * D_MODEL // tn),
        in_specs=[
            pl.BlockSpec((MOD_ROWS, D_MODEL), lambda l, j: (0, 0)),
            pl.BlockSpec((1, D_MODEL, tn), lambda l, j: (l, 0, j)),
            pl.BlockSpec((1, 1, tn), lambda l, j: (l, 0, j)),
        ],
        out_specs=pl.BlockSpec((1, MOD_ROWS, tn), lambda l, j: (l, 0, j)),
        out_shape=jax.ShapeDtypeStruct((DEPTH, MOD_ROWS, 6 * D_MODEL), F32),
        compiler_params=_params(2),
        name="ada",
    )(cvec, ada_w, ada_b.reshape(DEPTH, 1, 6 * D_MODEL))


def _ln_in_kernel(xp_ref, xs_ref, g_ref, b_ref, mod_ref, x_ref, h_ref, *, n_p):
    i = pl.program_id(0)

    def emit(src_ref):
        y = _layer_norm(src_ref[...], g_ref[...], b_ref[...])
        x_ref[...] = y
        h_ref[...] = (y * (1 + mod_ref[0, 1:2, :]) + mod_ref[0, 0:1, :]).astype(BF16)

    @pl.when(i < n_p)
    def _():
        emit(xp_ref)

    @pl.when(i >= n_p)
    def _():
        emit(xs_ref)


def _ln_in(xp, xs, g, b, mods):
    tm = 512
    n_p = NP_TOK // tm
    tile = lambda i: (i, 0)
    return pl.pallas_call(
        functools.partial(_ln_in_kernel, n_p=n_p),
        grid=(N_TOK // tm,),
        in_specs=[
            pl.BlockSpec((tm, D_MODEL), lambda i: (jnp.minimum(i, n_p - 1), 0)),
            pl.BlockSpec((tm, D_MODEL), lambda i: (jnp.maximum(i - n_p, 0), 0)),
            pl.BlockSpec((1, D_MODEL), lambda i: (0, 0)),
            pl.BlockSpec((1, D_MODEL), lambda i: (0, 0)),
            _mod_spec(tm),
        ],
        out_specs=[pl.BlockSpec((tm, D_MODEL), tile), pl.BlockSpec((tm, D_MODEL), tile)],
        out_shape=[jax.ShapeDtypeStruct((N_TOK, D_MODEL), F32), jax.ShapeDtypeStruct((N_TOK, D_MODEL), BF16)],
        compiler_params=_params(1),
        name="ln_in",
    )(xp, xs, g.reshape(1, D_MODEL), b.reshape(1, D_MODEL), mods)


def _conv_kernel(h_ref, w_ref, cw_ref, cb_ref, y_ref, *, tm):
    i = pl.program_id(0)
    y = jnp.dot(h_ref[...], w_ref[...], preferred_element_type=F32)
    gate_b, gate_c, u = y[:, :D_CONV], y[:, D_CONV:2 * D_CONV], y[:, 2 * D_CONV:]
    pc = gate_c * u
    seq = jnp.where(i < NP_TOK // tm, SEQ, DEC_SEQ)
    pos = lax.broadcasted_iota(jnp.int32, (tm, 1), 0) & (seq - 1)
    prev = jnp.where(pos == 0, 0.0, pltpu.roll(pc, 1, axis=0))
    nxt = jnp.where(pos == seq - 1, 0.0, pltpu.roll(pc, tm - 1, axis=0))
    conv = prev * cw_ref[0:1, :] + pc * cw_ref[1:2, :] + nxt * cw_ref[2:3, :] + cb_ref[...]
    y_ref[...] = (gate_b * conv).astype(BF16)


def _conv_branch(h, w3, conv_w, conv_b):
    tm = DEC_SEQ
    return pl.pallas_call(
        functools.partial(_conv_kernel, tm=tm),
        grid=(N_TOK // tm,),
        in_specs=[
            pl.BlockSpec((tm, D_MODEL), lambda i: (i, 0)),
            _resident((D_MODEL, 3 * D_CONV)),
            pl.BlockSpec((3, D_CONV), lambda i: (0, 0)),
            pl.BlockSpec((1, D_CONV), lambda i: (0, 0)),
        ],
        out_specs=pl.BlockSpec((tm, D_CONV), lambda i: (i, 0)),
        out_shape=jax.ShapeDtypeStruct((N_TOK, D_CONV), BF16),
        compiler_params=_params(1),
        name="conv_branch",
    )(h, w3, conv_w, conv_b.reshape(1, D_CONV))


def _rope_tables():
    pos = jnp.arange(DEC_SEQ)
    row = (pos // GRID_W).astype(F32)
    col = (pos % GRID_W).astype(F32)
    inv_freq = ROPE_BASE ** (-jnp.arange(0, AXIS_DIM, 2, dtype=F32) / AXIS_DIM)
    ang_r = row[:, None] * inv_freq
    ang_c = col[:, None] * inv_freq
    lane = jnp.arange(V_DIM)
    sub = lane % HEAD_DIM
    ang = jnp.where((sub < AXIS_DIM)[None, :], ang_r[:, lane % (AXIS_DIM // 2)], ang_c[:, lane % (AXIS_DIM // 2)])
    first = ((lane % AXIS_DIM) < AXIS_DIM // 2)[None, :]
    cos, sin = jnp.cos(ang), jnp.sin(ang)
    return cos, jnp.where(first, -sin, 0.0), jnp.where(first, 0.0, sin)


def _qkv_kernel(h_ref, w_ref, cos_ref, sup_ref, sdn_ref, q_ref, k_ref, v_ref, kc_ref, vc_ref, *, tm):
    i = pl.program_id(0)
    h = h_ref[...]
    yq = jnp.dot(h, w_ref[:, 0:ATTN_W], preferred_element_type=F32)
    yk = jnp.dot(h, w_ref[:, ATTN_W:2 * ATTN_W], preferred_element_type=F32)
    yv = jnp.dot(h, w_ref[:, 2 * ATTN_W:], preferred_element_type=F32)
    v_ref[...] = yv.astype(BF16)

    @pl.when(i < NP_TOK // tm)
    def _():
        q_ref[...] = (yq * QK_SCALE).astype(BF16)
        k_ref[...] = yk.astype(BF16)
        for s in range(tm // SEQ):
            for hd in range(N_HEADS):
                kc_ref[s, hd] = yk[s * SEQ:(s + 1) * SEQ, hd * V_DIM:(hd + 1) * V_DIM]
                vc_ref[s, hd] = yv[s * SEQ:(s + 1) * SEQ, hd * V_DIM:(hd + 1) * V_DIM]

    @pl.when(i >= NP_TOK // tm)
    def _():
        cos, s_up, s_dn = cos_ref[...], sup_ref[...], sdn_ref[...]

        def rope(x):
            return x * cos + pltpu.roll(x, V_DIM - AXIS_DIM // 2, axis=1) * s_up + pltpu.roll(x, AXIS_DIM // 2, axis=1) * s_dn

        for hd in range(N_HEADS):
            sl = slice(hd * V_DIM, (hd + 1) * V_DIM)
            q_ref[:, sl] = (rope(yq[:, sl]) * QK_SCALE).astype(BF16)
            k_ref[:, sl] = rope(yk[:, sl]).astype(BF16)


def _qkv(h, wqkv, tables):
    tm = 512
    n_p = NP_TOK // tm
    tile = pl.BlockSpec((tm, ATTN_W), lambda i: (i, 0))
    tab = pl.BlockSpec((tm, V_DIM), lambda i: (i % (DEC_SEQ // tm), 0))
    cache = pl.BlockSpec((tm // SEQ, N_HEADS, SEQ, V_DIM), lambda i: (jnp.minimum(i, n_p - 1), 0, 0, 0))
    act = jax.ShapeDtypeStruct((N_TOK, ATTN_W), BF16)
    ctx = jax.ShapeDtypeStruct((BATCH, N_HEADS, SEQ, V_DIM), F32)
    return pl.pallas_call(
        functools.partial(_qkv_kernel, tm=tm),
        grid=(N_TOK // tm,),
        in_specs=[pl.BlockSpec((tm, D_MODEL), lambda i: (i, 0)), _resident((D_MODEL, 3 * ATTN_W)), tab, tab, tab],
        out_specs=[tile, tile, tile, cache, cache],
        out_shape=[act, act, act, ctx, ctx],
        compiler_params=_params(1),
        name="qkv",
    )(h, wqkv, *tables)


def _lam(lam_ref, lam_init):
    a = jnp.sum(lam_ref[0:1, :] * lam_ref[1:2, :], axis=1, keepdims=True)
    b = jnp.sum(lam_ref[2:3, :] * lam_ref[3:4, :], axis=1, keepdims=True)
    return jnp.exp(a) - jnp.exp(b) + lam_init


def _diff_attn_tile(q, k, v, lam, g, lam_init):
    tq = q.shape[0]
    lo = lax.broadcasted_iota(jnp.int32, (1, V_DIM), 1) < HEAD_DIM
    zero = jnp.zeros_like(q)
    qq = jnp.concatenate([jnp.where(lo, q, zero), jnp.where(lo, zero, q)], axis=0)
    s = lax.dot_general(qq, k, (((1,), (1,)), ((), ())), preferred_element_type=F32)
    e = jnp.exp(s - jnp.max(s, axis=-1, keepdims=True))
    r = 1.0 / jnp.sum(e, axis=-1, keepdims=True)
    p = e[:tq] * r[:tq] - e[tq:] * (lam * r[tq:])
    o = jnp.dot(p.astype(BF16), v, preferred_element_type=F32)
    ms = jnp.mean(o * o, axis=-1, keepdims=True)
    return o * lax.rsqrt(ms + LN_EPS) * g * (1 - lam_init)


def _attn_prompt_kernel(lam_ref, g_ref, q_ref, k_ref, v_ref, o_ref, *, lam_init):
    lam = _lam(lam_ref, lam_init)
    for hd in range(N_HEADS):
        sl = slice(hd * V_DIM, (hd + 1) * V_DIM)
        o = _diff_attn_tile(q_ref[:, sl], k_ref[:, sl], v_ref[:, sl], lam, g_ref[...], lam_init)
        o_ref[:, sl] = o.astype(BF16)


def _attn_prompt(lam_vecs, g, q, k, v, lam_init):
    blk = pl.BlockSpec((SEQ, ATTN_W), lambda b: (b, 0))
    return pl.pallas_call(
        functools.partial(_attn_prompt_kernel, lam_init=lam_init),
        grid=(BATCH,),
        in_specs=[pl.BlockSpec((4, HEAD_DIM), lambda b: (0, 0)), pl.BlockSpec((1, V_DIM), lambda b: (0, 0)), blk, blk, blk],
        out_specs=blk,
        out_shape=jax.ShapeDtypeStruct((NP_TOK, ATTN_W), BF16),
        compiler_params=_params(1),
        name="attn_prompt",
    )(lam_vecs, g, q, k, v)


def _attn_sample_kernel(lam_ref, g_ref, q_ref, kn_ref, vn_ref, kc_ref, vc_ref, o_ref, k_s, v_s, *, lam_init, tq):
    lam = _lam(lam_ref, lam_init)
    k_s[0:PAST_LEN, :] = kc_ref[...].astype(BF16)
    k_s[PAST_LEN:, :] = kn_ref[...]
    v_s[0:PAST_LEN, :] = vc_ref[...].astype(BF16)
    v_s[PAST_LEN:, :] = vn_ref[...]

    def body(t, carry):
        r0 = pl.multiple_of(t * tq, tq)
        o = _diff_attn_tile(q_ref[pl.ds(r0, tq), :], k_s[...], v_s[...], lam, g_ref[...], lam_init)
        o_ref[pl.ds(r0, tq), :] = o.astype(BF16)
        return carry

    lax.fori_loop(0, DEC_SEQ // tq, body, 0)


def _attn_sample(lam_vecs, g, q, k, v, cache_k, cache_v, layer, lam_init):
    tq = 256
    first = NP_TOK // DEC_SEQ
    new = pl.BlockSpec((DEC_SEQ, V_DIM), lambda b, h: (first + b, h))
    past = pl.BlockSpec((None, None, None, PAST_LEN, V_DIM), lambda b, h: (b, layer, h, 0, 0))
    return pl.pallas_call(
        functools.partial(_attn_sample_kernel, lam_init=lam_init, tq=tq),
        grid=(DEC_BATCH, N_HEADS),
        in_specs=[pl.BlockSpec((4, HEAD_DIM), lambda b, h: (0, 0)), pl.BlockSpec((1, V_DIM), lambda b, h: (0, 0)),
                  new, new, new, past, past],
        out_specs=pl.BlockSpec((DEC_SEQ, V_DIM), lambda b, h: (b, h)),
        out_shape=jax.ShapeDtypeStruct((NS_TOK, ATTN_W), BF16),
        scratch_shapes=[pltpu.VMEM((PAST_LEN + DEC_SEQ, V_DIM), BF16), pltpu.VMEM((PAST_LEN + DEC_SEQ, V_DIM), BF16)],
        compiler_params=_params(2),
        name="attn_sample",
    )(lam_vecs, g, q, k, v, cache_k, cache_v)


def _top2_gates(logits):
    col = lax.broadcasted_iota(jnp.int32, logits.shape, 1)
    valid = col < N_EXPERTS
    lg = jnp.where(valid, logits, -1e30)
    e = jnp.where(valid, jnp.exp(lg - jnp.max(lg, axis=-1, keepdims=True)), 0.0)
    p = jnp.where(valid, e / jnp.sum(e, axis=-1, keepdims=True), -1.0)
    v1 = jnp.max(p, axis=-1, keepdims=True)
    i1 = jnp.min(jnp.where(p == v1, col, LANES), axis=-1, keepdims=True)
    p2 = jnp.where(col == i1, -1.0, p)
    v2 = jnp.max(p2, axis=-1, keepdims=True)
    i2 = jnp.min(jnp.where(p2 == v2, col, LANES), axis=-1, keepdims=True)
    den = v1 + v2
    return jnp.where(col == i1, v1 / den, 0.0) + jnp.where(col == i2, v2 / den, 0.0)


def _merge_kernel(h_ref, cy_ref, op_ref, os_ref, x_ref, mod_ref, wg_ref, wc_ref, wa_ref, wo_ref, g1_ref, b1_ref,
                  *rest, tm, routed):
    if routed:
        router_ref, x1_ref, h2_ref, gates_ref = rest
    else:
        x1_ref, h2_ref = rest
    i = pl.program_id(0)
    g = jnp.dot(h_ref[...], wg_ref[...], preferred_element_type=F32)
    y_conv = jnp.dot(cy_ref[...], wc_ref[...], preferred_element_type=F32)
    o = jnp.where(i < NP_TOK // tm, op_ref[...], os_ref[...])
    y_attn = jnp.dot(o, wa_ref[...], preferred_element_type=F32)
    merged = jax.nn.sigmoid(g[:, :D_MODEL]) * y_conv + jax.nn.sigmoid(g[:, D_MODEL:]) * y_attn
    m = jnp.dot(merged.astype(BF16), wo_ref[...], preferred_element_type=F32)
    x1 = _layer_norm(ALPHA * x_ref[...] + mod_ref[0, 2:3, :] * m, g1_ref[...], b1_ref[...])
    x1_ref[...] = x1
    h2 = x1 * (1 + mod_ref[0, 4:5, :]) + mod_ref[0, 3:4, :]
    h2_ref[...] = h2.astype(BF16)
    if routed:
        logits = jnp.dot(h2, router_ref[...], preferred_element_type=F32, precision=lax.Precision.HIGHEST)
        gates_ref[...] = _top2_gates(logits)


def _merge(h, conv_y, o_p, o_s, x, mods, wg, wc, wa, wo, g1, b1, router=None):
    tm = 512
    n_p = NP_TOK // tm
    routed = router is not None
    tile = lambda w: pl.BlockSpec((tm, w), lambda i: (i, 0))
    vec = pl.BlockSpec((1, D_MODEL), lambda i: (0, 0))
    in_specs = [
        tile(D_MODEL), tile(D_CONV),
        pl.BlockSpec((tm, ATTN_W), lambda i: (jnp.minimum(i, n_p - 1), 0)),
        pl.BlockSpec((tm, ATTN_W), lambda i: (jnp.maximum(i - n_p, 0), 0)),
        tile(D_MODEL), _mod_spec(tm),
        _resident((D_MODEL, 2 * D_MODEL)), _resident((D_CONV, D_MODEL)), _resident((ATTN_W, D_MODEL)),
        _resident((D_MODEL, D_MODEL)), vec, vec,
    ]
    args = [h, conv_y, o_p, o_s, x, mods, wg, wc, wa, wo, g1.reshape(1, D_MODEL), b1.reshape(1, D_MODEL)]
    out_specs = [tile(D_MODEL), tile(D_MODEL)]
    out_shape = [jax.ShapeDtypeStruct((N_TOK, D_MODEL), F32), jax.ShapeDtypeStruct((N_TOK, D_MODEL), BF16)]
    if routed:
        in_specs.append(_resident((D_MODEL, LANES)))
        args.append(router)
        out_specs.append(tile(LANES))
        out_shape.append(jax.ShapeDtypeStruct((N_TOK, LANES), F32))
    return pl.pallas_call(
        functools.partial(_merge_kernel, tm=tm, routed=routed),
        grid=(N_TOK // tm,),
        in_specs=in_specs, out_specs=out_specs, out_shape=out_shape,
        compiler_params=_params(1),
        name="merge_routed" if routed else "merge",
    )(*args)


def _ffn_kernel(*refs, tm, n_slabs, gated, last):
    refs = list(refs)
    h_ref = refs.pop(0)
    gates_ref = refs.pop(0) if gated else None
    wg_ref, wu_ref, wd_ref, x_ref, mod_ref, g2_ref, b2_ref = refs[:7]
    refs = refs[7:]
    if last:
        yp_ref, ys_ref, acc_ref = refs
    else:
        nmod_ref, x2_ref, hn_ref, acc_ref = refs
    i, e = pl.program_id(0), pl.program_id(1)
    h = h_ref[...]
    a = jnp.dot(h, wg_ref[...], preferred_element_type=F32)
    u = jnp.dot(h, wu_ref[...], preferred_element_type=F32)
    hid = (a * jax.nn.sigmoid(a) * u).astype(BF16)
    f = jnp.dot(hid, wd_ref[...], preferred_element_type=F32)
    if gated:
        gates = gates_ref[...]
        col = lax.broadcasted_iota(jnp.int32, gates.shape, 1)
        f = jnp.sum(jnp.where(col == e, gates, 0.0), axis=-1, keepdims=True) * f

    @pl.when(e == 0)
    def _():
        acc_ref[...] = f

    @pl.when(e > 0)
    def _():
        acc_ref[...] += f

    @pl.when(e == n_slabs - 1)
    def _():
        x2 = _layer_norm(ALPHA * x_ref[...] + mod_ref[0, 5:6, :] * acc_ref[...], g2_ref[...], b2_ref[...])
        if last:
            @pl.when(i < NP_TOK // tm)
            def _():
                yp_ref[...] = x2

            @pl.when(i >= NP_TOK // tm)
            def _():
                ys_ref[...] = x2
        else:
            x2_ref[...] = x2
            hn_ref[...] = (x2 * (1 + nmod_ref[0, 1:2, :]) + nmod_ref[0, 0:1, :]).astype(BF16)


def _ffn(h2, gates, wg, wu, wd, x1, mods, g2, b2, next_mods):
    tm = 512
    n_p = NP_TOK // tm
    n_slabs, _, d_slab = wg.shape
    gated, last = gates is not None, next_mods is None
    tile = lambda w: pl.BlockSpec((tm, w), lambda i, e: (i, 0))
    vec = pl.BlockSpec((1, D_MODEL), lambda i, e: (0, 0))
    in_specs, args = [tile(D_MODEL)], [h2]
    if gated:
        in_specs.append(tile(LANES))
        args.append(gates)
    in_specs += [
        pl.BlockSpec((None, D_MODEL, d_slab), lambda i, e: (e, 0, 0)),
        pl.BlockSpec((None, D_MODEL, d_slab), lambda i, e: (e, 0, 0)),
        pl.BlockSpec((None, d_slab, D_MODEL), lambda i, e: (e, 0, 0)),
        tile(D_MODEL), _mod_spec(tm), vec, vec,
    ]
    args += [wg, wu, wd, x1, mods, g2.reshape(1, D_MODEL), b2.reshape(1, D_MODEL)]
    if last:
        out_specs = [pl.BlockSpec((tm, D_MODEL), lambda i, e: (jnp.minimum(i, n_p - 1), 0)),
                     pl.BlockSpec((tm, D_MODEL), lambda i, e: (jnp.maximum(i - n_p, 0), 0))]
        out_shape = [jax.ShapeDtypeStruct((NP_TOK, D_MODEL), F32), jax.ShapeDtypeStruct((NS_TOK, D_MODEL), F32)]
    else:
        in_specs.append(_mod_spec(tm))
        args.append(next_mods)
        out_specs = [tile(D_MODEL), tile(D_MODEL)]
        out_shape = [jax.ShapeDtypeStruct((N_TOK, D_MODEL), F32), jax.ShapeDtypeStruct((N_TOK, D_MODEL), BF16)]
    return pl.pallas_call(
        functools.partial(_ffn_kernel, tm=tm, n_slabs=n_slabs, gated=gated, last=last),
        grid=(N_TOK // tm, n_slabs),
        in_specs=in_specs, out_specs=out_specs, out_shape=out_shape,
        scratch_shapes=[pltpu.VMEM((tm, D_MODEL), F32)],
        compiler_params=_params(2),
        name="ffn_routed" if gated else "ffn",
    )(*args)


def kernel(x_prompt, x_sample, cache_k, cache_v, c, c_ctx, ln_in_g, ln_in_b, ada_w, ada_b, w_in, conv_w, conv_b, w_conv_out, lam_q1, lam_k1, lam_q2, lam_k2, subln_g, w_attn_out, w_out, ln1_g, ln1_b, ln2_g, ln2_b, ffn_w_gate, ffn_w_up, ffn_w_down, moe_router, moe_w_gate, moe_w_up, moe_w_down):
    cvec = jnp.concatenate([c, c_ctx[None, :], jnp.zeros((MOD_ROWS - DEC_BATCH - 1, D_MODEL), F32)], axis=0)
    mods = _ada(cvec, ada_w, ada_b).reshape(DEPTH, MOD_ROWS, 6, D_MODEL)
    tables = _rope_tables()

    x, h = _ln_in(x_prompt.reshape(NP_TOK, D_MODEL), x_sample.reshape(NS_TOK, D_MODEL), ln_in_g, ln_in_b, mods[0])
    new_k, new_v = [], []
    for l in range(DEPTH):
        lam_init = 0.8 - 0.6 * math.exp(-0.3 * l)
        w_l = w_in[l].astype(BF16)
        w3 = w_l[:, :3 * D_CONV]
        wqkv = w_l[:, 3 * D_CONV:3 * D_CONV + 3 * ATTN_W]
        wgate = w_l[:, 3 * D_CONV + 3 * ATTN_W:]
        lam_vecs = jnp.stack([lam_q1[l], lam_k1[l], lam_q2[l], lam_k2[l]]).astype(F32)
        g_sub = subln_g[l].reshape(1, V_DIM)

        conv_y = _conv_branch(h, w3, conv_w[l], conv_b[l])
        q, k, v, k_ctx, v_ctx = _qkv(h, wqkv, tables)
        new_k.append(k_ctx)
        new_v.append(v_ctx)
        o_p = _attn_prompt(lam_vecs, g_sub, q, k, v, lam_init)
        o_s = _attn_sample(lam_vecs, g_sub, q, k, v, cache_k, cache_v, l, lam_init)

        routed = l % 2 == 1
        router = None
        if routed:
            router = jnp.pad(moe_router[l // 2], ((0, 0), (0, LANES - N_EXPERTS)))
        merged = _merge(h, conv_y, o_p, o_s, x, mods[l], wgate, w_conv_out[l].astype(BF16), w_attn_out[l].astype(BF16),
                        w_out[l].astype(BF16), ln1_g[l], ln1_b[l], router)
        next_mods = mods[l + 1] if l + 1 < DEPTH else None
        if routed:
            x1, h2, gates = merged
            i = l // 2
            out = _ffn(h2, gates, moe_w_gate[i].astype(BF16), moe_w_up[i].astype(BF16), moe_w_down[i].astype(BF16),
                       x1, mods[l], ln2_g[l], ln2_b[l], next_mods)
        else:
            x1, h2 = merged
            i = l // 2
            n_slabs = D_FF // D_FF_EXPERT
            slab = lambda w: w.astype(BF16).reshape(D_MODEL, n_slabs, D_FF_EXPERT).transpose(1, 0, 2)
            out = _ffn(h2, None, slab(ffn_w_gate[i]), slab(ffn_w_up[i]),
                       ffn_w_down[i].astype(BF16).reshape(n_slabs, D_FF_EXPERT, D_MODEL),
                       x1, mods[l], ln2_g[l], ln2_b[l], next_mods)
        if next_mods is None:
            y_p, y_s = out
        else:
            x, h = out

    return (y_p.reshape(BATCH, SEQ, D_MODEL), y_s.reshape(DEC_BATCH, DEC_SEQ, D_MODEL),
            jnp.stack(new_k, axis=1), jnp.stack(new_v, axis=1))
```

```python
import functools
import math

import jax
import jax.numpy as jnp
from jax import lax
from jax.experimental import pallas as pl
from jax.experimental.pallas import tpu as pltpu

D_MODEL = 1024
BATCH = 32
SEQ = 256
DEPTH = 2
DEC_BATCH = 8
DEC_SEQ = 1024
PAST_LEN = 512
GRID_W = 64
D_CONV = 512
N_HEADS = 8
HEAD_DIM = 64
V_DIM = 2 * HEAD_DIM
ATTN_W = N_HEADS * V_DIM
AXIS_DIM = HEAD_DIM // 2
ROPE_BASE = 10000.0
D_FF = 2816
N_EXPERTS = 8
D_FF_EXPERT = 1408
ALPHA = (2 * DEPTH) ** 0.25
LN_EPS = 1e-5
QK_SCALE = HEAD_DIM ** -0.5

NP_TOK = BATCH * SEQ
NS_TOK = DEC_BATCH * DEC_SEQ
N_TOK = NP_TOK + NS_TOK
MOD_ROWS = 16
CTX_ROW = DEC_BATCH
LANES = 128
VMEM_LIMIT = 56 * 1024 * 1024

F32 = jnp.float32
BF16 = jnp.bfloat16


def _params(n_axes, vmem=VMEM_LIMIT):
    return pltpu.CompilerParams(dimension_semantics=("arbitrary",) * n_axes, vmem_limit_bytes=vmem)


def _resident(shape):
    return pl.BlockSpec(shape, lambda *_: (0,) * len(shape), pipeline_mode=pl.Buffered(1))


def _mod_row(i, tm):
    n_p = NP_TOK // tm
    return jnp.where(i < n_p, CTX_ROW, (i - n_p) // (DEC_SEQ // tm))


def _mod_spec(tm):
    return pl.BlockSpec((1, 6, D_MODEL), lambda i, *_: (_mod_row(i, tm), 0, 0))


def _layer_norm(x, g, b):
    mu = jnp.mean(x, axis=-1, keepdims=True)
    xc = x - mu
    var = jnp.mean(xc * xc, axis=-1, keepdims=True)
    return xc * lax.rsqrt(var + LN_EPS) * g + b


def _ada_kernel(c_ref, w_ref, b_ref, o_ref):
    c = c_ref[...]
    a = (c * jax.nn.sigmoid(c)).astype(BF16)
    o_ref[0] = jnp.dot(a, w_ref[0].astype(BF16), preferred_element_type=F32) + b_ref[0]


def _ada(cvec, ada_w, ada_b):
    tn = 1024
    return pl.pallas_call(
        _ada_kernel,
        grid=(DEPTH, 6 * D_MODEL // tn),
        in_specs=[
            pl.BlockSpec((MOD_ROWS, D_MODEL), lambda l, j: (0, 0)),
            pl.BlockSpec((1, D_MODEL, tn), lambda l, j: (l, 0, j)),
            pl.BlockSpec((1, 1, tn), lambda l, j: (l, 0, j)),
        ],
        out_specs=pl.BlockSpec((1, MOD_ROWS, tn), lambda l, j: (l, 0, j)),
        out_shape=jax.ShapeDtypeStruct((DEPTH, MOD_ROWS, 6 * D_MODEL), F32),
        compiler_params=_params(2),
        name="ada",
    )(cvec, ada_w, ada_b.reshape(DEPTH, 1, 6 * D_MODEL))


def _ln_in_kernel(xp_ref, xs_ref, g_ref, b_ref, mod_ref, x_ref, h_ref, *, n_p):
    i = pl.program_id(0)

    def emit(src_ref):
        y = _layer_norm(src_ref[...], g_ref[...], b_ref[...])
        x_ref[...] = y
        h_ref[...] = (y * (1 + mod_ref[0, 1:2, :]) + mod_ref[0, 0:1, :]).astype(BF16)

    @pl.when(i < n_p)
    def _():
        emit(xp_ref)

    @pl.when(i >= n_p)
    def _():
        emit(xs_ref)


def _ln_in(xp, xs, g, b, mods):
    tm = 512
    n_p = NP_TOK // tm
    tile = lambda i: (i, 0)
    return pl.pallas_call(
        functools.partial(_ln_in_kernel, n_p=n_p),
        grid=(N_TOK // tm,),
        in_specs=[
            pl.BlockSpec((tm, D_MODEL), lambda i: (jnp.minimum(i, n_p - 1), 0)),
            pl.BlockSpec((tm, D_MODEL), lambda i: (jnp.maximum(i - n_p, 0), 0)),
            pl.BlockSpec((1, D_MODEL), lambda i: (0, 0)),
            pl.BlockSpec((1, D_MODEL), lambda i: (0, 0)),
            _mod_spec(tm),
        ],
        out_specs=[pl.BlockSpec((tm, D_MODEL), tile), pl.BlockSpec((tm, D_MODEL), tile)],
        out_shape=[jax.ShapeDtypeStruct((N_TOK, D_MODEL), F32), jax.ShapeDtypeStruct((N_TOK, D_MODEL), BF16)],
        compiler_params=_params(1),
        name="ln_in",
    )(xp, xs, g.reshape(1, D_MODEL), b.reshape(1, D_MODEL), mods)


def _conv_kernel(h_ref, w_ref, cw_ref, cb_ref, y_ref, *, tm):
    i = pl.program_id(0)
    y = jnp.dot(h_ref[...], w_ref[...], preferred_element_type=F32)
    gate_b, gate_c, u = y[:, :D_CONV], y[:, D_CONV:2 * D_CONV], y[:, 2 * D_CONV:]
    pc = gate_c * u
    seq = jnp.where(i < NP_TOK // tm, SEQ, DEC_SEQ)
    pos = lax.broadcasted_iota(jnp.int32, (tm, 1), 0) & (seq - 1)
    prev = jnp.where(pos == 0, 0.0, pltpu.roll(pc, 1, axis=0))
    nxt = jnp.where(pos == seq - 1, 0.0, pltpu.roll(pc, tm - 1, axis=0))
    conv = prev * cw_ref[0:1, :] + pc * cw_ref[1:2, :] + nxt * cw_ref[2:3, :] + cb_ref[...]
    y_ref[...] = (gate_b * conv).astype(BF16)


def _conv_branch(h, w3, conv_w, conv_b):
    tm = DEC_SEQ
    return pl.pallas_call(
        functools.partial(_conv_kernel, tm=tm),
        grid=(N_TOK // tm,),
        in_specs=[
            pl.BlockSpec((tm, D_MODEL), lambda i: (i, 0)),
            _resident((D_MODEL, 3 * D_CONV)),
            pl.BlockSpec((3, D_CONV), lambda i: (0, 0)),
            pl.BlockSpec((1, D_CONV), lambda i: (0, 0)),
        ],
        out_specs=pl.BlockSpec((tm, D_CONV), lambda i: (i, 0)),
        out_shape=jax.ShapeDtypeStruct((N_TOK, D_CONV), BF16),
        compiler_params=_params(1),
        name="conv_branch",
    )(h, w3, conv_w, conv_b.reshape(1, D_CONV))


def _rope_tables():
    pos = jnp.arange(DEC_SEQ)
    row = (pos // GRID_W).astype(F32)
    col = (pos % GRID_W).astype(F32)
    inv_freq = ROPE_BASE ** (-jnp.arange(0, AXIS_DIM, 2, dtype=F32) / AXIS_DIM)
    ang_r = row[:, None] * inv_freq
    ang_c = col[:, None] * inv_freq
    lane = jnp.arange(V_DIM)
    sub = lane % HEAD_DIM
    ang = jnp.where((sub < AXIS_DIM)[None, :], ang_r[:, lane % (AXIS_DIM // 2)], ang_c[:, lane % (AXIS_DIM // 2)])
    first = ((lane % AXIS_DIM) < AXIS_DIM // 2)[None, :]
    cos, sin = jnp.cos(ang), jnp.sin(ang)
    return cos, jnp.where(first, -sin, 0.0), jnp.where(first, 0.0, sin)


def _qkv_kernel(h_ref, w_ref, cos_ref, sup_ref, sdn_ref, *rest, tm, layer, first):
    q_ref, k_ref, v_ref, kc_all, vc_all = rest[-5:]
    kc_ref, vc_ref = (kc_all.at[:, layer], vc_all.at[:, layer]) if first else (kc_all, vc_all)
    i = pl.program_id(0)
    h = h_ref[...]
    yq = jnp.dot(h, w_ref[:, 0:ATTN_W], preferred_element_type=F32)
    yk = jnp.dot(h, w_ref[:, ATTN_W:2 * ATTN_W], preferred_element_type=F32)
    yv = jnp.dot(h, w_ref[:, 2 * ATTN_W:], preferred_element_type=F32)
    v_ref[...] = yv.astype(BF16)

    @pl.when(i < NP_TOK // tm)
    def _():
        q_ref[...] = (yq * QK_SCALE).astype(BF16)
        k_ref[...] = yk.astype(BF16)
        for s in range(tm // SEQ):
            for hd in range(N_HEADS):
                kc_ref[s, hd] = yk[s * SEQ:(s + 1) * SEQ, hd * V_DIM:(hd + 1) * V_DIM]
                vc_ref[s, hd] = yv[s * SEQ:(s + 1) * SEQ, hd * V_DIM:(hd + 1) * V_DIM]
        if first:
            for other in range(DEPTH):
                if other != layer:
                    kc_all[:, other] = jnp.zeros((tm // SEQ, N_HEADS, SEQ, V_DIM), F32)
                    vc_all[:, other] = jnp.zeros((tm // SEQ, N_HEADS, SEQ, V_DIM), F32)

    @pl.when(i >= NP_TOK // tm)
    def _():
        cos, s_up, s_dn = cos_ref[...], sup_ref[...], sdn_ref[...]

        def rope(x):
            return x * cos + pltpu.roll(x, V_DIM - AXIS_DIM // 2, axis=1) * s_up + pltpu.roll(x, AXIS_DIM // 2, axis=1) * s_dn

        for hd in range(N_HEADS):
            sl = slice(hd * V_DIM, (hd + 1) * V_DIM)
            q_ref[:, sl] = (rope(yq[:, sl]) * QK_SCALE).astype(BF16)
            k_ref[:, sl] = rope(yk[:, sl]).astype(BF16)


def _qkv(h, wqkv, tables, layer, caches):
    tm = 512
    n_p = NP_TOK // tm
    tile = pl.BlockSpec((tm, ATTN_W), lambda i: (i, 0))
    tab = pl.BlockSpec((tm, V_DIM), lambda i: (i % (DEC_SEQ // tm), 0))
    first = caches is None
    if first:
        cache = pl.BlockSpec((tm // SEQ, DEPTH, N_HEADS, SEQ, V_DIM), lambda i: (jnp.minimum(i, n_p - 1), 0, 0, 0, 0))
    else:
        cache = pl.BlockSpec((tm // SEQ, None, N_HEADS, SEQ, V_DIM), lambda i: (jnp.minimum(i, n_p - 1), layer, 0, 0, 0))
    act = jax.ShapeDtypeStruct((N_TOK, ATTN_W), BF16)
    ctx = jax.ShapeDtypeStruct((BATCH, DEPTH, N_HEADS, SEQ, V_DIM), F32)
    in_specs = [pl.BlockSpec((tm, D_MODEL), lambda i: (i, 0)), _resident((D_MODEL, 3 * ATTN_W)), tab, tab, tab]
    args = [h, wqkv, *tables]
    aliases = {}
    if not first:
        aliases = {len(args): 3, len(args) + 1: 4}
        in_specs += [pl.BlockSpec(memory_space=pl.ANY)] * 2
        args += list(caches)
    return pl.pallas_call(
        functools.partial(_qkv_kernel, tm=tm, layer=layer, first=first),
        grid=(N_TOK // tm,),
        in_specs=in_specs,
        out_specs=[tile, tile, tile, cache, cache],
        out_shape=[act, act, act, ctx, ctx],
        input_output_aliases=aliases,
        compiler_params=_params(1),
        name="qkv",
    )(*args)


def _lam(lam_ref, lam_init):
    a = jnp.sum(lam_ref[0:1, :] * lam_ref[1:2, :], axis=1, keepdims=True)
    b = jnp.sum(lam_ref[2:3, :] * lam_ref[3:4, :], axis=1, keepdims=True)
    return jnp.exp(a) - jnp.exp(b) + lam_init


def _diff_attn_tile(q, k, v, lam, g, lam_init):
    tq = q.shape[0]
    lo = lax.broadcasted_iota(jnp.int32, (1, V_DIM), 1) < HEAD_DIM
    zero = jnp.zeros_like(q)
    qq = jnp.concatenate([jnp.where(lo, q, zero), jnp.where(lo, zero, q)], axis=0)
    s = lax.dot_general(qq, k, (((1,), (1,)), ((), ())), preferred_element_type=F32)
    e = jnp.exp(s - jnp.max(s, axis=-1, keepdims=True))
    r = 1.0 / jnp.sum(e, axis=-1, keepdims=True)
    p = e[:tq] * r[:tq] - e[tq:] * (lam * r[tq:])
    o = jnp.dot(p.astype(BF16), v, preferred_element_type=F32)
    ms = jnp.mean(o * o, axis=-1, keepdims=True)
    return o * lax.rsqrt(ms + LN_EPS) * g * (1 - lam_init)


def _attn_prompt_kernel(lam_ref, g_ref, q_ref, k_ref, v_ref, o_ref, *, lam_init):
    lam = _lam(lam_ref, lam_init)
    for hd in range(N_HEADS):
        sl = slice(hd * V_DIM, (hd + 1) * V_DIM)
        o = _diff_attn_tile(q_ref[:, sl], k_ref[:, sl], v_ref[:, sl], lam, g_ref[...], lam_init)
        o_ref[:, sl] = o.astype(BF16)


def _attn_prompt(lam_vecs, g, q, k, v, lam_init):
    blk = pl.BlockSpec((SEQ, ATTN_W), lambda b: (b, 0))
    return pl.pallas_call(
        functools.partial(_attn_prompt_kernel, lam_init=lam_init),
        grid=(BATCH,),
        in_specs=[pl.BlockSpec((4, HEAD_DIM), lambda b: (0, 0)), pl.BlockSpec((1, V_DIM), lambda b: (0, 0)), blk, blk, blk],
        out_specs=blk,
        out_shape=jax.ShapeDtypeStruct((NP_TOK, ATTN_W), BF16),
        compiler_params=_params(1),
        name="attn_prompt",
    )(lam_vecs, g, q, k, v)


def _attn_sample_kernel(lam_ref, g_ref, q_ref, kn_ref, vn_ref, kc_ref, vc_ref, o_ref, k_s, v_s, *, lam_init, tq):
    lam = _lam(lam_ref, lam_init)
    k_s[0:PAST_LEN, :] = kc_ref[...].astype(BF16)
    k_s[PAST_LEN:, :] = kn_ref[...]
    v_s[0:PAST_LEN, :] = vc_ref[...].astype(BF16)
    v_s[PAST_LEN:, :] = vn_ref[...]

    def body(t, carry):
        r0 = pl.multiple_of(t * tq, tq)
        o = _diff_attn_tile(q_ref[pl.ds(r0, tq), :], k_s[...], v_s[...], lam, g_ref[...], lam_init)
        o_ref[pl.ds(r0, tq), :] = o.astype(BF16)
        return carry

    lax.fori_loop(0, DEC_SEQ // tq, body, 0)


def _attn_sample(lam_vecs, g, q, k, v, cache_k, cache_v, layer, lam_init):
    tq = 256
    first = NP_TOK // DEC_SEQ
    new = pl.BlockSpec((DEC_SEQ, V_DIM), lambda b, h: (first + b, h))
    past = pl.BlockSpec((None, None, None, PAST_LEN, V_DIM), lambda b, h: (b, layer, h, 0, 0))
    return pl.pallas_call(
        functools.partial(_attn_sample_kernel, lam_init=lam_init, tq=tq),
        grid=(DEC_BATCH, N_HEADS),
        in_specs=[pl.BlockSpec((4, HEAD_DIM), lambda b, h: (0, 0)), pl.BlockSpec((1, V_DIM), lambda b, h: (0, 0)),
                  new, new, new, past, past],
        out_specs=pl.BlockSpec((DEC_SEQ, V_DIM), lambda b, h: (b, h)),
        out_shape=jax.ShapeDtypeStruct((NS_TOK, ATTN_W), BF16),
        scratch_shapes=[pltpu.VMEM((PAST_LEN + DEC_SEQ, V_DIM), BF16), pltpu.VMEM((PAST_LEN + DEC_SEQ, V_DIM), BF16)],
        compiler_params=_params(2),
        name="attn_sample",
    )(lam_vecs, g, q, k, v, cache_k, cache_v)


_NT = (((1,), (1,)), ((), ()))


def _route(h2, router_ref, cnt_ref):
    tm = h2.shape[0]
    hi = h2.astype(BF16)
    lo = (h2 - hi.astype(F32)).astype(BF16)
    a = lax.dot_general(router_ref[...], hi, _NT, preferred_element_type=F32)
    b = lax.dot_general(router_ref[0:N_EXPERTS, :], lo, _NT, preferred_element_type=F32)
    logits = a[:N_EXPERTS] + a[N_EXPERTS:] + b
    e = jnp.exp(logits - jnp.max(logits, axis=0, keepdims=True))
    p = e / jnp.sum(e, axis=0, keepdims=True)
    row = lax.broadcasted_iota(jnp.int32, p.shape, 0)
    v1 = jnp.max(p, axis=0, keepdims=True)
    i1 = jnp.min(jnp.where(p == v1, row, N_EXPERTS), axis=0, keepdims=True)
    p2 = jnp.where(row == i1, -1.0, p)
    v2 = jnp.max(p2, axis=0, keepdims=True)
    i2 = jnp.min(jnp.where(p2 == v2, row, N_EXPERTS), axis=0, keepdims=True)
    den = v1 + v2
    pick1, pick2 = row == i1, row == i2
    picked = jnp.where(pick1 | pick2, 1.0, 0.0)
    before = lax.broadcasted_iota(jnp.int32, (tm, tm), 0) < lax.broadcasted_iota(jnp.int32, (tm, tm), 1)
    ahead = jnp.dot(picked.astype(BF16), jnp.where(before, 1.0, 0.0).astype(BF16), preferred_element_type=F32)
    ahead = ahead + cnt_ref[:, 0:1]
    rank1 = jnp.sum(jnp.where(pick1, ahead, 0.0), axis=0, keepdims=True)
    rank2 = jnp.sum(jnp.where(pick2, ahead, 0.0), axis=0, keepdims=True)
    cnt_ref[...] = cnt_ref[...] + jnp.sum(picked, axis=1, keepdims=True)
    zero = jnp.zeros_like(v1)
    return jnp.concatenate([i1.astype(F32), i2.astype(F32), rank1, rank2, v1 / den, v2 / den, zero, zero], axis=0)


def _merge_kernel(h_ref, cy_ref, op_ref, os_ref, x_ref, mod_ref, wg_ref, wc_ref, wa_ref, wo_ref, g1_ref, b1_ref,
                  *rest, tm, routed):
    if routed:
        router_ref, x1_ref, h2_ref, route_ref, total_ref, cnt_ref = rest
    else:
        x1_ref, h2_ref = rest
    i = pl.program_id(0)
    g = jnp.dot(h_ref[...], wg_ref[...], preferred_element_type=F32)
    y_conv = jnp.dot(cy_ref[...], wc_ref[...], preferred_element_type=F32)
    o = jnp.where(i < NP_TOK // tm, op_ref[...], os_ref[...])
    y_attn = jnp.dot(o, wa_ref[...], preferred_element_type=F32)
    merged = jax.nn.sigmoid(g[:, :D_MODEL]) * y_conv + jax.nn.sigmoid(g[:, D_MODEL:]) * y_attn
    m = jnp.dot(merged.astype(BF16), wo_ref[...], preferred_element_type=F32)
    x1 = _layer_norm(ALPHA * x_ref[...] + mod_ref[0, 2:3, :] * m, g1_ref[...], b1_ref[...])
    x1_ref[...] = x1
    h2 = x1 * (1 + mod_ref[0, 4:5, :]) + mod_ref[0, 3:4, :]
    h2_ref[...] = h2.astype(h2_ref.dtype)
    if routed:
        @pl.when(i == 0)
        def _():
            cnt_ref[...] = jnp.zeros_like(cnt_ref)

        route_ref[...] = _route(h2, router_ref, cnt_ref)
        total_ref[...] = cnt_ref[...]


def _merge(h, conv_y, o_p, o_s, x, mods, wg, wc, wa, wo, g1, b1, router=None):
    tm = 512
    n_p = NP_TOK // tm
    routed = router is not None
    tile = lambda w: pl.BlockSpec((tm, w), lambda i: (i, 0))
    vec = pl.BlockSpec((1, D_MODEL), lambda i: (0, 0))
    in_specs = [
        tile(D_MODEL), tile(D_CONV),
        pl.BlockSpec((tm, ATTN_W), lambda i: (jnp.minimum(i, n_p - 1), 0)),
        pl.BlockSpec((tm, ATTN_W), lambda i: (jnp.maximum(i - n_p, 0), 0)),
        tile(D_MODEL), _mod_spec(tm),
        _resident((D_MODEL, 2 * D_MODEL)), _resident((D_CONV, D_MODEL)), _resident((ATTN_W, D_MODEL)),
        _resident((D_MODEL, D_MODEL)), vec, vec,
    ]
    args = [h, conv_y, o_p, o_s, x, mods, wg, wc, wa, wo, g1.reshape(1, D_MODEL), b1.reshape(1, D_MODEL)]
    out_specs = [tile(D_MODEL), tile(D_MODEL)]
    out_shape = [jax.ShapeDtypeStruct((N_TOK, D_MODEL), F32),
                 jax.ShapeDtypeStruct((N_TOK, D_MODEL), F32 if routed else BF16)]
    scratch = []
    if routed:
        in_specs.append(_resident((2 * N_EXPERTS, D_MODEL)))
        args.append(router)
        out_specs += [pl.BlockSpec((N_EXPERTS, tm), lambda i: (0, i)), pl.BlockSpec((N_EXPERTS, LANES), lambda i: (0, 0))]
        out_shape += [jax.ShapeDtypeStruct((N_EXPERTS, N_TOK), F32), jax.ShapeDtypeStruct((N_EXPERTS, LANES), F32)]
        scratch = [pltpu.VMEM((N_EXPERTS, LANES), F32)]
    return pl.pallas_call(
        functools.partial(_merge_kernel, tm=tm, routed=routed),
        grid=(N_TOK // tm,),
        in_specs=in_specs, out_specs=out_specs, out_shape=out_shape,
        scratch_shapes=scratch,
        compiler_params=_params(1),
        name="merge_routed" if routed else "merge",
    )(*args)


def _swiglu(x, wg_ref, wu_ref, wd_ref):
    a = jnp.dot(x, wg_ref[...], preferred_element_type=F32)
    u = jnp.dot(x, wu_ref[...], preferred_element_type=F32)
    hid = (a * jax.nn.sigmoid(a) * u).astype(BF16)
    return jnp.dot(hid, wd_ref[...], preferred_element_type=F32)


def _ffn_kernel(h_ref, wg_ref, wu_ref, wd_ref, x_ref, mod_ref, g2_ref, b2_ref, nmod_ref, x2_ref, hn_ref, acc_ref, *, n_slabs):
    e = pl.program_id(1)
    f = _swiglu(h_ref[...], wg_ref, wu_ref, wd_ref)

    @pl.when(e == 0)
    def _():
        acc_ref[...] = f

    @pl.when(e > 0)
    def _():
        acc_ref[...] += f

    @pl.when(e == n_slabs - 1)
    def _():
        x2 = _layer_norm(ALPHA * x_ref[...] + mod_ref[0, 5:6, :] * acc_ref[...], g2_ref[...], b2_ref[...])
        x2_ref[...] = x2
        hn_ref[...] = (x2 * (1 + nmod_ref[0, 1:2, :]) + nmod_ref[0, 0:1, :]).astype(BF16)


def _ffn(h2, wg, wu, wd, x1, mods, g2, b2, next_mods):
    tm = 512
    n_slabs, _, d_slab = wg.shape
    tile = lambda w: pl.BlockSpec((tm, w), lambda i, e: (i, 0))
    vec = pl.BlockSpec((1, D_MODEL), lambda i, e: (0, 0))
    w_in = pl.BlockSpec((None, D_MODEL, d_slab), lambda i, e: (e, 0, 0))
    w_out = pl.BlockSpec((None, d_slab, D_MODEL), lambda i, e: (e, 0, 0))
    return pl.pallas_call(
        functools.partial(_ffn_kernel, n_slabs=n_slabs),
        grid=(N_TOK // tm, n_slabs),
        in_specs=[tile(D_MODEL), w_in, w_in, w_out, tile(D_MODEL), _mod_spec(tm), vec, vec, _mod_spec(tm)],
        out_specs=[tile(D_MODEL), tile(D_MODEL)],
        out_shape=[jax.ShapeDtypeStruct((N_TOK, D_MODEL), F32), jax.ShapeDtypeStruct((N_TOK, D_MODEL), BF16)],
        scratch_shapes=[pltpu.VMEM((tm, D_MODEL), F32)],
        compiler_params=_params(2),
        name="ffn",
    )(h2, wg, wu, wd, x1, mods, g2.reshape(1, D_MODEL), b2.reshape(1, D_MODEL), next_mods)


N_PAIRS = 2 * N_TOK
SLOT_TILE = 512
N_SLOT_TILES = N_PAIRS // SLOT_TILE
N_ITEMS = N_SLOT_TILES + N_EXPERTS - 1


def _routing_tables(route, totals):
    counts = totals[:, 0].astype(jnp.int32)
    off = jnp.concatenate([jnp.zeros((1,), jnp.int32), jnp.cumsum(counts)])
    e1, e2 = route[0].astype(jnp.int32), route[1].astype(jnp.int32)
    slots = jnp.stack([off[e1] + route[2].astype(jnp.int32), off[e2] + route[3].astype(jnp.int32)], axis=1)
    weights = jnp.stack([route[4], route[5]], axis=1)
    t0 = jnp.arange(N_SLOT_TILES, dtype=jnp.int32)[:, None] * SLOT_TILE
    live = jnp.maximum(off[None, :-1], t0) < jnp.minimum(off[None, 1:], t0 + SLOT_TILE)
    n_items = jnp.sum(live).astype(jnp.int32)
    order = jnp.nonzero(live.reshape(-1), size=N_ITEMS, fill_value=0)[0].astype(jnp.int32)
    order = jnp.where(jnp.arange(N_ITEMS) < n_items, order, order[n_items - 1])
    return slots, weights, order // N_EXPERTS, order % N_EXPERTS, off, n_items.reshape(1)


def _dispatch_kernel(slot_ref, h_ref, xs_hbm, sem, *, tm):
    def row_copy(r, k):
        return pltpu.make_async_copy(h_ref.at[pl.ds(r, 1), :], xs_hbm.at[pl.ds(slot_ref[0, 2 * r + k], 1), :], sem)

    def issue(r, carry):
        row_copy(r, 0).start()
        row_copy(r, 1).start()
        return carry

    lax.fori_loop(0, tm, issue, 0)
    for _ in range(2):
        pltpu.make_async_copy(h_ref, xs_hbm.at[pl.ds(0, tm), :], sem).wait()


def _dispatch(h2, slots):
    tm = 512
    return pl.pallas_call(
        functools.partial(_dispatch_kernel, tm=tm),
        grid=(N_TOK // tm,),
        in_specs=[pl.BlockSpec((None, 1, 2 * tm), lambda i: (i, 0, 0), memory_space=pltpu.SMEM),
                  pl.BlockSpec((tm, D_MODEL), lambda i: (i, 0))],
        out_specs=pl.BlockSpec(memory_space=pl.ANY),
        out_shape=jax.ShapeDtypeStruct((N_PAIRS, D_MODEL), F32),
        scratch_shapes=[pltpu.SemaphoreType.DMA(())],
        compiler_params=_params(1),
        name="moe_dispatch",
    )(slots.reshape(N_TOK // tm, 1, 2 * tm), h2)


def _moe_kernel(tile_ref, expert_ref, off_ref, n_ref, x_ref, wg_ref, wu_ref, wd_ref, y_ref):
    j = pl.program_id(0)

    @pl.when(j < n_ref[0])
    def _():
        t, e = tile_ref[j], expert_ref[j]
        row = lax.broadcasted_iota(jnp.int32, (SLOT_TILE, 1), 0) + t * SLOT_TILE
        mine = (row >= off_ref[e]) & (row < off_ref[e + 1])
        f = _swiglu(x_ref[...].astype(BF16), wg_ref, wu_ref, wd_ref)
        opens_tile = (j == 0) | (tile_ref[jnp.maximum(j - 1, 0)] != t)

        @pl.when(opens_tile)
        def _():
            y_ref[...] = jnp.where(mine, f, 0.0)

        @pl.when(jnp.logical_not(opens_tile))
        def _():
            y_ref[...] = jnp.where(mine, f, y_ref[...])


def _moe_ffn(x_sorted, item_tile, item_expert, off, n_items, wg, wu, wd):
    d_ff = wg.shape[-1]
    rows = pl.BlockSpec((SLOT_TILE, D_MODEL), lambda j, it, ie, off, n: (it[j], 0))
    w_in = pl.BlockSpec((None, D_MODEL, d_ff), lambda j, it, ie, off, n: (ie[j], 0, 0))
    w_out = pl.BlockSpec((None, d_ff, D_MODEL), lambda j, it, ie, off, n: (ie[j], 0, 0))
    return pl.pallas_call(
        _moe_kernel,
        grid_spec=pltpu.PrefetchScalarGridSpec(
            num_scalar_prefetch=4, grid=(N_ITEMS,),
            in_specs=[rows, w_in, w_in, w_out], out_specs=rows),
        out_shape=jax.ShapeDtypeStruct((N_PAIRS, D_MODEL), F32),
        compiler_params=_params(1),
        name="moe_ffn",
    )(item_tile, item_expert, off, n_items, x_sorted, wg, wu, wd)


def _combine_kernel(slot_ref, w_ref, x_ref, mod_ref, g2_ref, b2_ref, y_hbm, yp_ref, ys_ref, buf, sem, *, tm):
    i = pl.program_id(0)

    def row_copy(r, k):
        return pltpu.make_async_copy(y_hbm.at[pl.ds(slot_ref[0, 2 * r + k], 1), :], buf.at[k, pl.ds(r, 1), :], sem)

    def issue(r, carry):
        row_copy(r, 0).start()
        row_copy(r, 1).start()
        return carry

    lax.fori_loop(0, tm, issue, 0)
    for k in range(2):
        pltpu.make_async_copy(y_hbm.at[pl.ds(0, tm), :], buf.at[k], sem).wait()
    w = w_ref[...]
    f = w[:, 0:1] * buf[0] + w[:, 1:2] * buf[1]
    x2 = _layer_norm(ALPHA * x_ref[...] + mod_ref[0, 5:6, :] * f, g2_ref[...], b2_ref[...])

    @pl.when(i < NP_TOK // tm)
    def _():
        yp_ref[...] = x2

    @pl.when(i >= NP_TOK // tm)
    def _():
        ys_ref[...] = x2


def _combine(y_sorted, slots, weights, x1, mods, g2, b2):
    tm = 256
    n_p = NP_TOK // tm
    vec = pl.BlockSpec((1, D_MODEL), lambda i: (0, 0))
    return pl.pallas_call(
        functools.partial(_combine_kernel, tm=tm),
        grid=(N_TOK // tm,),
        in_specs=[pl.BlockSpec((None, 1, 2 * tm), lambda i: (i, 0, 0), memory_space=pltpu.SMEM),
                  pl.BlockSpec((tm, 2), lambda i: (i, 0)),
                  pl.BlockSpec((tm, D_MODEL), lambda i: (i, 0)), _mod_spec(tm), vec, vec,
                  pl.BlockSpec(memory_space=pl.ANY)],
        out_specs=[pl.BlockSpec((tm, D_MODEL), lambda i: (jnp.minimum(i, n_p - 1), 0)),
                   pl.BlockSpec((tm, D_MODEL), lambda i: (jnp.maximum(i - n_p, 0), 0))],
        out_shape=[jax.ShapeDtypeStruct((NP_TOK, D_MODEL), F32), jax.ShapeDtypeStruct((NS_TOK, D_MODEL), F32)],
        scratch_shapes=[pltpu.VMEM((2, tm, D_MODEL), F32), pltpu.SemaphoreType.DMA(())],
        compiler_params=_params(1),
        name="moe_combine",
    )(slots.reshape(N_TOK // tm, 1, 2 * tm), weights, x1, mods, g2.reshape(1, D_MODEL), b2.reshape(1, D_MODEL), y_sorted)


def kernel(x_prompt, x_sample, cache_k, cache_v, c, c_ctx, ln_in_g, ln_in_b, ada_w, ada_b, w_in, conv_w, conv_b, w_conv_out, lam_q1, lam_k1, lam_q2, lam_k2, subln_g, w_attn_out, w_out, ln1_g, ln1_b, ln2_g, ln2_b, ffn_w_gate, ffn_w_up, ffn_w_down, moe_router, moe_w_gate, moe_w_up, moe_w_down):
    assert DEPTH == 2
    cvec = jnp.concatenate([c, c_ctx[None, :], jnp.zeros((MOD_ROWS - DEC_BATCH - 1, D_MODEL), F32)], axis=0)
    mods = _ada(cvec, ada_w, ada_b).reshape(DEPTH, MOD_ROWS, 6, D_MODEL)
    tables = _rope_tables()

    x, h = _ln_in(x_prompt.reshape(NP_TOK, D_MODEL), x_sample.reshape(NS_TOK, D_MODEL), ln_in_g, ln_in_b, mods[0])
    caches = None
    for l in range(DEPTH):
        lam_init = 0.8 - 0.6 * math.exp(-0.3 * l)
        w_l = w_in[l].astype(BF16)
        w3 = w_l[:, :3 * D_CONV]
        wqkv = w_l[:, 3 * D_CONV:3 * D_CONV + 3 * ATTN_W]
        wgate = w_l[:, 3 * D_CONV + 3 * ATTN_W:]
        lam_vecs = jnp.stack([lam_q1[l], lam_k1[l], lam_q2[l], lam_k2[l]]).astype(F32)
        g_sub = subln_g[l].reshape(1, V_DIM)

        conv_y = _conv_branch(h, w3, conv_w[l], conv_b[l])
        q, k, v, *caches = _qkv(h, wqkv, tables, l, caches)
        o_p = _attn_prompt(lam_vecs, g_sub, q, k, v, lam_init)
        o_s = _attn_sample(lam_vecs, g_sub, q, k, v, cache_k, cache_v, l, lam_init)

        mix_w = (wgate, w_conv_out[l].astype(BF16), w_attn_out[l].astype(BF16), w_out[l].astype(BF16), ln1_g[l], ln1_b[l])
        i = l // 2
        if l % 2 == 0:
            x1, h2 = _merge(h, conv_y, o_p, o_s, x, mods[l], *mix_w)
            n_slabs = D_FF // D_FF_EXPERT
            slab = lambda w: w.astype(BF16).reshape(D_MODEL, n_slabs, D_FF_EXPERT).transpose(1, 0, 2)
            x, h = _ffn(h2, slab(ffn_w_gate[i]), slab(ffn_w_up[i]),
                        ffn_w_down[i].astype(BF16).reshape(n_slabs, D_FF_EXPERT, D_MODEL),
                        x1, mods[l], ln2_g[l], ln2_b[l], mods[l + 1])
        else:
            r_t = moe_router[i].T
            r_hi = r_t.astype(BF16)
            router = jnp.concatenate([r_hi, (r_t - r_hi.astype(F32)).astype(BF16)], axis=0)
            x1, h2, route, totals = _merge(h, conv_y, o_p, o_s, x, mods[l], *mix_w, router)
            slots, weights, item_tile, item_expert, off, n_items = _routing_tables(route, totals)
            x_sorted = _dispatch(h2, slots)
            y_sorted = _moe_ffn(x_sorted, item_tile, item_expert, off, n_items,
                                moe_w_gate[i].astype(BF16), moe_w_up[i].astype(BF16), moe_w_down[i].astype(BF16))
            y_p, y_s = _combine(y_sorted, slots, weights, x1, mods[l], ln2_g[l], ln2_b[l])

    return (y_p.reshape(BATCH, SEQ, D_MODEL), y_s.reshape(DEC_BATCH, DEC_SEQ, D_MODEL), caches[0], caches[1])
```

```python
import functools
import math

import jax
import jax.numpy as jnp
from jax import lax
from jax.experimental import pallas as pl
from jax.experimental.pallas import tpu as pltpu

D_MODEL = 1024
BATCH = 32
SEQ = 256
DEPTH = 2
DEC_BATCH = 8
DEC_SEQ = 1024
PAST_LEN = 512
GRID_W = 64
D_CONV = 512
N_HEADS = 8
HEAD_DIM = 64
V_DIM = 2 * HEAD_DIM
ATTN_W = N_HEADS * V_DIM
AXIS_DIM = HEAD_DIM // 2
ROPE_BASE = 10000.0
D_FF = 2816
N_EXPERTS = 8
D_FF_EXPERT = 1408
ALPHA = (2 * DEPTH) ** 0.25
LN_EPS = 1e-5
QK_SCALE = HEAD_DIM ** -0.5 * math.log2(math.e)

NP_TOK = BATCH * SEQ
NS_TOK = DEC_BATCH * DEC_SEQ
N_TOK = NP_TOK + NS_TOK
MOD_ROWS = 16
CTX_ROW = DEC_BATCH
LANES = 128
VMEM_LIMIT = 56 * 1024 * 1024

F32 = jnp.float32
BF16 = jnp.bfloat16
_NT = (((1,), (1,)), ((), ()))


def _params(n_axes, vmem=VMEM_LIMIT):
    return pltpu.CompilerParams(dimension_semantics=("arbitrary",) * n_axes, vmem_limit_bytes=vmem)


def _resident(shape):
    return pl.BlockSpec(shape, lambda *_: (0,) * len(shape), pipeline_mode=pl.Buffered(1))


def _mod_row(i, tm):
    n_p = NP_TOK // tm
    return jnp.where(i < n_p, CTX_ROW, (i - n_p) // (DEC_SEQ // tm))


def _mod_spec(tm):
    return pl.BlockSpec((1, 6, D_MODEL), lambda i, *_: (_mod_row(i, tm), 0, 0))


def _layer_norm(x, g, b):
    mu = jnp.mean(x, axis=-1, keepdims=True)
    xc = x - mu
    var = jnp.mean(xc * xc, axis=-1, keepdims=True)
    return xc * lax.rsqrt(var + LN_EPS) * g + b


def _ada_kernel(c_ref, w_ref, b_ref, o_ref):
    c = c_ref[...]
    a = (c * jax.nn.sigmoid(c)).astype(BF16)
    o_ref[0] = jnp.dot(a, w_ref[0].astype(BF16), preferred_element_type=F32) + b_ref[0]


def _ada(cvec, ada_w, ada_b):
    tn = 1024
    return pl.pallas_call(
        _ada_kernel,
        grid=(DEPTH, 6 * D_MODEL // tn),
        in_specs=[
            pl.BlockSpec((MOD_ROWS, D_MODEL), lambda l, j: (0, 0)),
            pl.BlockSpec((1, D_MODEL, tn), lambda l, j: (l, 0, j)),
            pl.BlockSpec((1, 1, tn), lambda l, j: (l, 0, j)),
        ],
        out_specs=pl.BlockSpec((1, MOD_ROWS, tn), lambda l, j: (l, 0, j)),
        out_shape=jax.ShapeDtypeStruct((DEPTH, MOD_ROWS, 6 * D_MODEL), F32),
        compiler_params=_params(2),
        name="ada",
    )(cvec, ada_w, ada_b.reshape(DEPTH, 1, 6 * D_MODEL))


def _ln_in_kernel(xp_ref, xs_ref, g_ref, b_ref, mod_ref, x_ref, h_ref, *, n_p):
    i = pl.program_id(0)

    def emit(src_ref):
        y = _layer_norm(src_ref[...], g_ref[...], b_ref[...])
        x_ref[...] = y
        h_ref[...] = (y * (1 + mod_ref[0, 1:2, :]) + mod_ref[0, 0:1, :]).astype(BF16)

    @pl.when(i < n_p)
    def _():
        emit(xp_ref)

    @pl.when(i >= n_p)
    def _():
        emit(xs_ref)


def _ln_in(xp, xs, g, b, mods):
    tm = 512
    n_p = NP_TOK // tm
    tile = lambda i: (i, 0)
    return pl.pallas_call(
        functools.partial(_ln_in_kernel, n_p=n_p),
        grid=(N_TOK // tm,),
        in_specs=[
            pl.BlockSpec((tm, D_MODEL), lambda i: (jnp.minimum(i, n_p - 1), 0)),
            pl.BlockSpec((tm, D_MODEL), lambda i: (jnp.maximum(i - n_p, 0), 0)),
            pl.BlockSpec((1, D_MODEL), lambda i: (0, 0)),
            pl.BlockSpec((1, D_MODEL), lambda i: (0, 0)),
            _mod_spec(tm),
        ],
        out_specs=[pl.BlockSpec((tm, D_MODEL), tile), pl.BlockSpec((tm, D_MODEL), tile)],
        out_shape=[jax.ShapeDtypeStruct((N_TOK, D_MODEL), F32), jax.ShapeDtypeStruct((N_TOK, D_MODEL), BF16)],
        compiler_params=_params(1),
        name="ln_in",
    )(xp, xs, g.reshape(1, D_MODEL), b.reshape(1, D_MODEL), mods)


def _conv_kernel(h_ref, w_ref, cw_ref, cb_ref, y_ref, *, tm):
    i = pl.program_id(0)
    y = jnp.dot(h_ref[...], w_ref[...], preferred_element_type=F32)
    gate_b, gate_c, u = y[:, :D_CONV], y[:, D_CONV:2 * D_CONV], y[:, 2 * D_CONV:]
    pc = gate_c * u
    seq = jnp.where(i < NP_TOK // tm, SEQ, DEC_SEQ)
    pos = lax.broadcasted_iota(jnp.int32, (tm, 1), 0) & (seq - 1)
    prev = jnp.where(pos == 0, 0.0, pltpu.roll(pc, 1, axis=0))
    nxt = jnp.where(pos == seq - 1, 0.0, pltpu.roll(pc, tm - 1, axis=0))
    conv = prev * cw_ref[0:1, :] + pc * cw_ref[1:2, :] + nxt * cw_ref[2:3, :] + cb_ref[...]
    y_ref[...] = (gate_b * conv).astype(BF16)


def _conv_branch(h, w3, conv_w, conv_b):
    tm = DEC_SEQ
    return pl.pallas_call(
        functools.partial(_conv_kernel, tm=tm),
        grid=(N_TOK // tm,),
        in_specs=[
            pl.BlockSpec((tm, D_MODEL), lambda i: (i, 0)),
            _resident((D_MODEL, 3 * D_CONV)),
            pl.BlockSpec((3, D_CONV), lambda i: (0, 0)),
            pl.BlockSpec((1, D_CONV), lambda i: (0, 0)),
        ],
        out_specs=pl.BlockSpec((tm, D_CONV), lambda i: (i, 0)),
        out_shape=jax.ShapeDtypeStruct((N_TOK, D_CONV), BF16),
        compiler_params=_params(1),
        name="conv_branch",
    )(h, w3, conv_w, conv_b.reshape(1, D_CONV))


def _rope_tables():
    pos = jnp.arange(DEC_SEQ)
    row = (pos // GRID_W).astype(F32)
    col = (pos % GRID_W).astype(F32)
    inv_freq = ROPE_BASE ** (-jnp.arange(0, AXIS_DIM, 2, dtype=F32) / AXIS_DIM)
    ang_r = row[:, None] * inv_freq
    ang_c = col[:, None] * inv_freq
    lane = jnp.arange(V_DIM)
    sub = lane % HEAD_DIM
    ang = jnp.where((sub < AXIS_DIM)[None, :], ang_r[:, lane % (AXIS_DIM // 2)], ang_c[:, lane % (AXIS_DIM // 2)])
    first = ((lane % AXIS_DIM) < AXIS_DIM // 2)[None, :]
    cos, sin = jnp.cos(ang), jnp.sin(ang)
    return cos, jnp.where(first, -sin, 0.0), jnp.where(first, 0.0, sin)


def _qkv_kernel(h_ref, w_ref, cos_ref, sup_ref, sdn_ref, *rest, tm, layer, first):
    q_ref, k_ref, v_ref, kc_all, vc_all = rest[-5:]
    kc_ref, vc_ref = (kc_all.at[:, layer], vc_all.at[:, layer]) if first else (kc_all, vc_all)
    i = pl.program_id(0)
    h = h_ref[...]
    yq = jnp.dot(h, w_ref[:, 0:ATTN_W], preferred_element_type=F32)
    yk = jnp.dot(h, w_ref[:, ATTN_W:2 * ATTN_W], preferred_element_type=F32)
    yv = jnp.dot(h, w_ref[:, 2 * ATTN_W:], preferred_element_type=F32)
    v_ref[...] = yv.astype(BF16)

    @pl.when(i < NP_TOK // tm)
    def _():
        q_ref[...] = (yq * QK_SCALE).astype(BF16)
        k_ref[...] = yk.astype(BF16)
        for s in range(tm // SEQ):
            for hd in range(N_HEADS):
                kc_ref[s, hd] = yk[s * SEQ:(s + 1) * SEQ, hd * V_DIM:(hd + 1) * V_DIM]
                vc_ref[s, hd] = yv[s * SEQ:(s + 1) * SEQ, hd * V_DIM:(hd + 1) * V_DIM]
        if first:
            for other in range(DEPTH):
                if other != layer:
                    kc_all[:, other] = jnp.zeros((tm // SEQ, N_HEADS, SEQ, V_DIM), F32)
                    vc_all[:, other] = jnp.zeros((tm // SEQ, N_HEADS, SEQ, V_DIM), F32)

    @pl.when(i >= NP_TOK // tm)
    def _():
        cos, s_up, s_dn = cos_ref[...], sup_ref[...], sdn_ref[...]

        def rope(x):
            return x * cos + pltpu.roll(x, V_DIM - AXIS_DIM // 2, axis=1) * s_up + pltpu.roll(x, AXIS_DIM // 2, axis=1) * s_dn

        for hd in range(N_HEADS):
            sl = slice(hd * V_DIM, (hd + 1) * V_DIM)
            q_ref[:, sl] = (rope(yq[:, sl]) * QK_SCALE).astype(BF16)
            k_ref[:, sl] = rope(yk[:, sl]).astype(BF16)


def _qkv(h, wqkv, tables, layer, caches):
    tm = 512
    n_p = NP_TOK // tm
    tile = pl.BlockSpec((tm, ATTN_W), lambda i: (i, 0))
    tab = pl.BlockSpec((tm, V_DIM), lambda i: (i % (DEC_SEQ // tm), 0))
    first = caches is None
    if first:
        cache = pl.BlockSpec((tm // SEQ, DEPTH, N_HEADS, SEQ, V_DIM), lambda i: (jnp.minimum(i, n_p - 1), 0, 0, 0, 0))
    else:
        cache = pl.BlockSpec((tm // SEQ, None, N_HEADS, SEQ, V_DIM), lambda i: (jnp.minimum(i, n_p - 1), layer, 0, 0, 0))
    act = jax.ShapeDtypeStruct((N_TOK, ATTN_W), BF16)
    ctx = jax.ShapeDtypeStruct((BATCH, DEPTH, N_HEADS, SEQ, V_DIM), F32)
    in_specs = [pl.BlockSpec((tm, D_MODEL), lambda i: (i, 0)), _resident((D_MODEL, 3 * ATTN_W)), tab, tab, tab]
    args = [h, wqkv, *tables]
    aliases = {}
    if not first:
        aliases = {len(args): 3, len(args) + 1: 4}
        in_specs += [pl.BlockSpec(memory_space=pl.ANY)] * 2
        args += list(caches)
    return pl.pallas_call(
        functools.partial(_qkv_kernel, tm=tm, layer=layer, first=first),
        grid=(N_TOK // tm,),
        in_specs=in_specs,
        out_specs=[tile, tile, tile, cache, cache],
        out_shape=[act, act, act, ctx, ctx],
        input_output_aliases=aliases,
        compiler_params=_params(1),
        name="qkv",
    )(*args)


def _lam(lam_ref, lam_init):
    a = jnp.sum(lam_ref[0:1, :] * lam_ref[1:2, :], axis=1, keepdims=True)
    b = jnp.sum(lam_ref[2:3, :] * lam_ref[3:4, :], axis=1, keepdims=True)
    return jnp.exp(a) - jnp.exp(b) + lam_init


def _scores(q, k):
    lo = lax.broadcasted_iota(jnp.int32, (1, V_DIM), 1) < HEAD_DIM
    zero = jnp.zeros_like(q)
    qq = jnp.concatenate([jnp.where(lo, q, zero), jnp.where(lo, zero, q)], axis=0)
    return lax.dot_general(qq, k, _NT, preferred_element_type=F32)


def _diff_probs(s, lam):
    tq = s.shape[0] // 2
    e = jnp.exp2(s - jnp.max(s, axis=-1, keepdims=True))
    l = jnp.sum(e, axis=-1, keepdims=True)
    p = e[:tq] - e[tq:] * (lam * l[:tq] / l[tq:])
    return p.astype(BF16), 1.0 / l[:tq]


def _head_out(p, inv_l1, v, g, lam_init):
    o = jnp.dot(p, v, preferred_element_type=F32) * inv_l1
    ms = jnp.mean(o * o, axis=-1, keepdims=True)
    return (o * lax.rsqrt(ms + LN_EPS) * g * (1 - lam_init)).astype(BF16)


def _diff_attn_tiles(tiles, lam, g, lam_init):
    n = len(tiles)
    scores = lambda t: _scores(tiles[t][0](), tiles[t][1]())
    s = {0: scores(0)}
    probs = {}
    for t in range(-1, n):
        if t + 2 < n:
            s[t + 2] = scores(t + 2)
        if t + 1 < n:
            if t + 1 not in s:
                s[t + 1] = scores(t + 1)
            probs[t + 1] = _diff_probs(s.pop(t + 1), lam)
        if t >= 0:
            p, inv_l1 = probs.pop(t)
            tiles[t][3](_head_out(p, inv_l1, tiles[t][2](), g, lam_init))


def _attn_prompt_kernel(lam_ref, g_ref, q_ref, k_ref, v_ref, o_ref, *, lam_init):
    def tile(hd):
        sl = slice(hd * V_DIM, (hd + 1) * V_DIM)

        def store(o):
            o_ref[:, sl] = o

        return (lambda: q_ref[:, sl], lambda: k_ref[:, sl], lambda: v_ref[:, sl], store)

    _diff_attn_tiles([tile(hd) for hd in range(N_HEADS)], _lam(lam_ref, lam_init), g_ref[...], lam_init)


def _attn_prompt(lam_vecs, g, q, k, v, lam_init):
    blk = pl.BlockSpec((SEQ, ATTN_W), lambda b: (b, 0))
    return pl.pallas_call(
        functools.partial(_attn_prompt_kernel, lam_init=lam_init),
        grid=(BATCH,),
        in_specs=[pl.BlockSpec((4, HEAD_DIM), lambda b: (0, 0)), pl.BlockSpec((1, V_DIM), lambda b: (0, 0)), blk, blk, blk],
        out_specs=blk,
        out_shape=jax.ShapeDtypeStruct((NP_TOK, ATTN_W), BF16),
        compiler_params=_params(1),
        name="attn_prompt",
    )(lam_vecs, g, q, k, v)


def _attn_sample_kernel(lam_ref, g_ref, q_ref, kn_ref, vn_ref, kc_ref, vc_ref, o_ref, k_s, v_s, *, lam_init, tq):
    k_s[0:PAST_LEN, :] = kc_ref[...].astype(BF16)
    k_s[PAST_LEN:, :] = kn_ref[...]
    v_s[0:PAST_LEN, :] = vc_ref[...].astype(BF16)
    v_s[PAST_LEN:, :] = vn_ref[...]

    def tile(t):
        rows = slice(t * tq, (t + 1) * tq)

        def store(o):
            o_ref[rows, :] = o

        return (lambda: q_ref[rows, :], lambda: k_s[...], lambda: v_s[...], store)

    _diff_attn_tiles([tile(t) for t in range(DEC_SEQ // tq)], _lam(lam_ref, lam_init), g_ref[...], lam_init)


def _attn_sample(lam_vecs, g, q, k, v, cache_k, cache_v, layer, lam_init):
    tq = 128
    first = NP_TOK // DEC_SEQ
    new = pl.BlockSpec((DEC_SEQ, V_DIM), lambda b, h: (first + b, h))
    past = pl.BlockSpec((None, None, None, PAST_LEN, V_DIM), lambda b, h: (b, layer, h, 0, 0))
    return pl.pallas_call(
        functools.partial(_attn_sample_kernel, lam_init=lam_init, tq=tq),
        grid=(DEC_BATCH, N_HEADS),
        in_specs=[pl.BlockSpec((4, HEAD_DIM), lambda b, h: (0, 0)), pl.BlockSpec((1, V_DIM), lambda b, h: (0, 0)),
                  new, new, new, past, past],
        out_specs=pl.BlockSpec((DEC_SEQ, V_DIM), lambda b, h: (b, h)),
        out_shape=jax.ShapeDtypeStruct((NS_TOK, ATTN_W), BF16),
        scratch_shapes=[pltpu.VMEM((PAST_LEN + DEC_SEQ, V_DIM), BF16), pltpu.VMEM((PAST_LEN + DEC_SEQ, V_DIM), BF16)],
        compiler_params=_params(2),
        name="attn_sample",
    )(lam_vecs, g, q, k, v, cache_k, cache_v)


def _route(h2, router_ref, cnt_ref):
    tm = h2.shape[0]
    hi = h2.astype(BF16)
    lo = (h2 - hi.astype(F32)).astype(BF16)
    a = lax.dot_general(router_ref[...], hi, _NT, preferred_element_type=F32)
    b = lax.dot_general(router_ref[0:N_EXPERTS, :], lo, _NT, preferred_element_type=F32)
    logits = a[:N_EXPERTS] + a[N_EXPERTS:] + b
    e = jnp.exp(logits - jnp.max(logits, axis=0, keepdims=True))
    p = e / jnp.sum(e, axis=0, keepdims=True)
    row = lax.broadcasted_iota(jnp.int32, p.shape, 0)
    v1 = jnp.max(p, axis=0, keepdims=True)
    i1 = jnp.min(jnp.where(p == v1, row, N_EXPERTS), axis=0, keepdims=True)
    p2 = jnp.where(row == i1, -1.0, p)
    v2 = jnp.max(p2, axis=0, keepdims=True)
    i2 = jnp.min(jnp.where(p2 == v2, row, N_EXPERTS), axis=0, keepdims=True)
    den = v1 + v2
    pick1, pick2 = row == i1, row == i2
    picked = jnp.where(pick1 | pick2, 1.0, 0.0)
    before = lax.broadcasted_iota(jnp.int32, (tm, tm), 0) < lax.broadcasted_iota(jnp.int32, (tm, tm), 1)
    ahead = jnp.dot(picked.astype(BF16), jnp.where(before, 1.0, 0.0).astype(BF16), preferred_element_type=F32)
    ahead = ahead + cnt_ref[:, 0:1]
    rank1 = jnp.sum(jnp.where(pick1, ahead, 0.0), axis=0, keepdims=True)
    rank2 = jnp.sum(jnp.where(pick2, ahead, 0.0), axis=0, keepdims=True)
    cnt_ref[...] = cnt_ref[...] + jnp.sum(picked, axis=1, keepdims=True)
    zero = jnp.zeros_like(v1)
    return jnp.concatenate([i1.astype(F32), i2.astype(F32), rank1, rank2, v1 / den, v2 / den, zero, zero], axis=0)


def _merge_kernel(h_ref, cy_ref, op_ref, os_ref, x_ref, mod_ref, wg_ref, wc_ref, wa_ref, wo_ref, g1_ref, b1_ref,
                  *rest, tm, routed):
    if routed:
        router_ref, x1_ref, h2_ref, route_ref, total_ref, cnt_ref = rest
    else:
        x1_ref, h2_ref = rest
    i = pl.program_id(0)
    g = jnp.dot(h_ref[...], wg_ref[...], preferred_element_type=F32)
    y_conv = jnp.dot(cy_ref[...], wc_ref[...], preferred_element_type=F32)
    o = jnp.where(i < NP_TOK // tm, op_ref[...], os_ref[...])
    y_attn = jnp.dot(o, wa_ref[...], preferred_element_type=F32)
    merged = jax.nn.sigmoid(g[:, :D_MODEL]) * y_conv + jax.nn.sigmoid(g[:, D_MODEL:]) * y_attn
    m = jnp.dot(merged.astype(BF16), wo_ref[...], preferred_element_type=F32)
    x1 = _layer_norm(ALPHA * x_ref[...] + mod_ref[0, 2:3, :] * m, g1_ref[...], b1_ref[...])
    x1_ref[...] = x1
    h2 = x1 * (1 + mod_ref[0, 4:5, :]) + mod_ref[0, 3:4, :]
    h2_ref[...] = h2.astype(h2_ref.dtype)
    if routed:
        @pl.when(i == 0)
        def _():
            cnt_ref[...] = jnp.zeros_like(cnt_ref)

        route_ref[...] = _route(h2, router_ref, cnt_ref)
        total_ref[...] = cnt_ref[...]


def _merge(h, conv_y, o_p, o_s, x, mods, wg, wc, wa, wo, g1, b1, router=None):
    tm = 512
    n_p = NP_TOK // tm
    routed = router is not None
    tile = lambda w: pl.BlockSpec((tm, w), lambda i: (i, 0))
    vec = pl.BlockSpec((1, D_MODEL), lambda i: (0, 0))
    in_specs = [
        tile(D_MODEL), tile(D_CONV),
        pl.BlockSpec((tm, ATTN_W), lambda i: (jnp.minimum(i, n_p - 1), 0)),
        pl.BlockSpec((tm, ATTN_W), lambda i: (jnp.maximum(i - n_p, 0), 0)),
        tile(D_MODEL), _mod_spec(tm),
        _resident((D_MODEL, 2 * D_MODEL)), _resident((D_CONV, D_MODEL)), _resident((ATTN_W, D_MODEL)),
        _resident((D_MODEL, D_MODEL)), vec, vec,
    ]
    args = [h, conv_y, o_p, o_s, x, mods, wg, wc, wa, wo, g1.reshape(1, D_MODEL), b1.reshape(1, D_MODEL)]
    out_specs = [tile(D_MODEL), tile(D_MODEL)]
    out_shape = [jax.ShapeDtypeStruct((N_TOK, D_MODEL), F32),
                 jax.ShapeDtypeStruct((N_TOK, D_MODEL), F32 if routed else BF16)]
    scratch = []
    if routed:
        in_specs.append(_resident((2 * N_EXPERTS, D_MODEL)))
        args.append(router)
        out_specs += [pl.BlockSpec((N_EXPERTS, tm), lambda i: (0, i)), pl.BlockSpec((N_EXPERTS, LANES), lambda i: (0, 0))]
        out_shape += [jax.ShapeDtypeStruct((N_EXPERTS, N_TOK), F32), jax.ShapeDtypeStruct((N_EXPERTS, LANES), F32)]
        scratch = [pltpu.VMEM((N_EXPERTS, LANES), F32)]
    return pl.pallas_call(
        functools.partial(_merge_kernel, tm=tm, routed=routed),
        grid=(N_TOK // tm,),
        in_specs=in_specs, out_specs=out_specs, out_shape=out_shape,
        scratch_shapes=scratch,
        compiler_params=_params(1),
        name="merge_routed" if routed else "merge",
    )(*args)


def _swiglu(x, wg_ref, wu_ref, wd_ref):
    a = jnp.dot(x, wg_ref[...], preferred_element_type=F32)
    u = jnp.dot(x, wu_ref[...], preferred_element_type=F32)
    hid = (a * jax.nn.sigmoid(a) * u).astype(BF16)
    return jnp.dot(hid, wd_ref[...], preferred_element_type=F32)


def _ffn_kernel(h_ref, wg_ref, wu_ref, wd_ref, x_ref, mod_ref, g2_ref, b2_ref, nmod_ref, x2_ref, hn_ref, acc_ref, *, n_slabs):
    e = pl.program_id(1)
    f = _swiglu(h_ref[...], wg_ref, wu_ref, wd_ref)

    @pl.when(e == 0)
    def _():
        acc_ref[...] = f

    @pl.when(e > 0)
    def _():
        acc_ref[...] += f

    @pl.when(e == n_slabs - 1)
    def _():
        x2 = _layer_norm(ALPHA * x_ref[...] + mod_ref[0, 5:6, :] * acc_ref[...], g2_ref[...], b2_ref[...])
        x2_ref[...] = x2
        hn_ref[...] = (x2 * (1 + nmod_ref[0, 1:2, :]) + nmod_ref[0, 0:1, :]).astype(BF16)


def _ffn(h2, wg, wu, wd, x1, mods, g2, b2, next_mods):
    tm = 512
    n_slabs, _, d_slab = wg.shape
    tile = lambda w: pl.BlockSpec((tm, w), lambda i, e: (i, 0))
    vec = pl.BlockSpec((1, D_MODEL), lambda i, e: (0, 0))
    w_in = pl.BlockSpec((None, D_MODEL, d_slab), lambda i, e: (e, 0, 0))
    w_out = pl.BlockSpec((None, d_slab, D_MODEL), lambda i, e: (e, 0, 0))
    return pl.pallas_call(
        functools.partial(_ffn_kernel, n_slabs=n_slabs),
        grid=(N_TOK // tm, n_slabs),
        in_specs=[tile(D_MODEL), w_in, w_in, w_out, tile(D_MODEL), _mod_spec(tm), vec, vec, _mod_spec(tm)],
        out_specs=[tile(D_MODEL), tile(D_MODEL)],
        out_shape=[jax.ShapeDtypeStruct((N_TOK, D_MODEL), F32), jax.ShapeDtypeStruct((N_TOK, D_MODEL), BF16)],
        scratch_shapes=[pltpu.VMEM((tm, D_MODEL), F32)],
        compiler_params=_params(2),
        name="ffn",
    )(h2, wg, wu, wd, x1, mods, g2.reshape(1, D_MODEL), b2.reshape(1, D_MODEL), next_mods)


N_PAIRS = 2 * N_TOK
SLOT_TILE = 512
N_SLOT_TILES = N_PAIRS // SLOT_TILE
N_ITEMS = N_SLOT_TILES + N_EXPERTS - 1
DMA_UNROLL = 8


def _routing_tables(route, totals):
    counts = totals[:, 0].astype(jnp.int32)
    off = jnp.concatenate([jnp.zeros((1,), jnp.int32), jnp.cumsum(counts)])
    e1, e2 = route[0].astype(jnp.int32), route[1].astype(jnp.int32)
    slots = jnp.stack([off[e1] + route[2].astype(jnp.int32), off[e2] + route[3].astype(jnp.int32)], axis=1)
    weights = jnp.stack([route[4], route[5]], axis=1)
    t0 = jnp.arange(N_SLOT_TILES, dtype=jnp.int32)[:, None] * SLOT_TILE
    live = jnp.maximum(off[None, :-1], t0) < jnp.minimum(off[None, 1:], t0 + SLOT_TILE)
    n_items = jnp.sum(live).astype(jnp.int32)
    order = jnp.nonzero(live.reshape(-1), size=N_ITEMS, fill_value=0)[0].astype(jnp.int32)
    order = jnp.where(jnp.arange(N_ITEMS) < n_items, order, order[n_items - 1])
    return slots, weights, order // N_EXPERTS, order % N_EXPERTS, off, n_items.reshape(1)


def _dispatch_kernel(slot_ref, h_ref, xs_hbm, sem, *, tm):
    def row_copy(r, k):
        return pltpu.make_async_copy(h_ref.at[pl.ds(r, 1), :], xs_hbm.at[pl.ds(slot_ref[0, 2 * r + k], 1), :], sem)

    def issue(r, carry):
        for k in range(2):
            row_copy(r, k).start(priority=k)
        return carry

    lax.fori_loop(0, tm, issue, 0, unroll=DMA_UNROLL)
    for _ in range(2):
        pltpu.make_async_copy(h_ref, xs_hbm.at[pl.ds(0, tm), :], sem).wait()


def _dispatch(h2, slots):
    tm = 512
    return pl.pallas_call(
        functools.partial(_dispatch_kernel, tm=tm),
        grid=(N_TOK // tm,),
        in_specs=[pl.BlockSpec((None, 1, 2 * tm), lambda i: (i, 0, 0), memory_space=pltpu.SMEM),
                  pl.BlockSpec((tm, D_MODEL), lambda i: (i, 0))],
        out_specs=pl.BlockSpec(memory_space=pl.ANY),
        out_shape=jax.ShapeDtypeStruct((N_PAIRS, D_MODEL), F32),
        scratch_shapes=[pltpu.SemaphoreType.DMA(())],
        compiler_params=_params(1),
        name="moe_dispatch",
    )(slots.reshape(N_TOK // tm, 1, 2 * tm), h2)


def _moe_kernel(tile_ref, expert_ref, off_ref, n_ref, x_ref, wg_ref, wu_ref, wd_ref, y_ref):
    j = pl.program_id(0)

    @pl.when(j < n_ref[0])
    def _():
        t, e = tile_ref[j], expert_ref[j]
        row = lax.broadcasted_iota(jnp.int32, (SLOT_TILE, 1), 0) + t * SLOT_TILE
        mine = (row >= off_ref[e]) & (row < off_ref[e + 1])
        f = _swiglu(x_ref[...].astype(BF16), wg_ref, wu_ref, wd_ref)
        opens_tile = (j == 0) | (tile_ref[jnp.maximum(j - 1, 0)] != t)

        @pl.when(opens_tile)
        def _():
            y_ref[...] = jnp.where(mine, f, 0.0)

        @pl.when(jnp.logical_not(opens_tile))
        def _():
            y_ref[...] = jnp.where(mine, f, y_ref[...])


def _moe_ffn(x_sorted, item_tile, item_expert, off, n_items, wg, wu, wd):
    d_ff = wg.shape[-1]
    rows = pl.BlockSpec((SLOT_TILE, D_MODEL), lambda j, it, ie, off, n: (it[j], 0))
    w_in = pl.BlockSpec((None, D_MODEL, d_ff), lambda j, it, ie, off, n: (ie[j], 0, 0))
    w_out = pl.BlockSpec((None, d_ff, D_MODEL), lambda j, it, ie, off, n: (ie[j], 0, 0))
    return pl.pallas_call(
        _moe_kernel,
        grid_spec=pltpu.PrefetchScalarGridSpec(
            num_scalar_prefetch=4, grid=(N_ITEMS,),
            in_specs=[rows, w_in, w_in, w_out], out_specs=rows),
        out_shape=jax.ShapeDtypeStruct((N_PAIRS, D_MODEL), F32),
        compiler_params=_params(1),
        name="moe_ffn",
    )(item_tile, item_expert, off, n_items, x_sorted, wg, wu, wd)


def _combine_kernel(slot_ref, next_slot_ref, w_ref, x_ref, mod_ref, g2_ref, b2_ref, y_hbm, yp_ref, ys_ref, buf, sem, *, tm):
    i = pl.program_id(0)
    cur = i % 2

    def gather(slots, b):
        def issue(r, carry):
            for k in range(2):
                pltpu.make_async_copy(y_hbm.at[pl.ds(slots[0, 2 * r + k], 1), :], buf.at[b, k, pl.ds(r, 1), :],
                                      sem.at[b]).start(priority=k)
            return carry

        lax.fori_loop(0, tm, issue, 0, unroll=DMA_UNROLL)

    @pl.when(i == 0)
    def _():
        gather(slot_ref, 0)

    @pl.when(i + 1 < pl.num_programs(0))
    def _():
        gather(next_slot_ref, 1 - cur)

    for k in range(2):
        pltpu.make_async_copy(y_hbm.at[pl.ds(0, tm), :], buf.at[cur, k], sem.at[cur]).wait()
    w = w_ref[...]
    f = w[:, 0:1] * buf[cur, 0] + w[:, 1:2] * buf[cur, 1]
    x2 = _layer_norm(ALPHA * x_ref[...] + mod_ref[0, 5:6, :] * f, g2_ref[...], b2_ref[...])

    @pl.when(i < NP_TOK // tm)
    def _():
        yp_ref[...] = x2

    @pl.when(i >= NP_TOK // tm)
    def _():
        ys_ref[...] = x2


def _combine(y_sorted, slots, weights, x1, mods, g2, b2):
    tm = 256
    n_p = NP_TOK // tm
    n_tiles = N_TOK // tm
    vec = pl.BlockSpec((1, D_MODEL), lambda i: (0, 0))
    slots = slots.reshape(n_tiles, 1, 2 * tm)
    return pl.pallas_call(
        functools.partial(_combine_kernel, tm=tm),
        grid=(n_tiles,),
        in_specs=[pl.BlockSpec((None, 1, 2 * tm), lambda i: (i, 0, 0), memory_space=pltpu.SMEM),
                  pl.BlockSpec((None, 1, 2 * tm), lambda i: (jnp.minimum(i + 1, n_tiles - 1), 0, 0), memory_space=pltpu.SMEM),
                  pl.BlockSpec((tm, 2), lambda i: (i, 0)),
                  pl.BlockSpec((tm, D_MODEL), lambda i: (i, 0)), _mod_spec(tm), vec, vec,
                  pl.BlockSpec(memory_space=pl.ANY)],
        out_specs=[pl.BlockSpec((tm, D_MODEL), lambda i: (jnp.minimum(i, n_p - 1), 0)),
                   pl.BlockSpec((tm, D_MODEL), lambda i: (jnp.maximum(i - n_p, 0), 0))],
        out_shape=[jax.ShapeDtypeStruct((NP_TOK, D_MODEL), F32), jax.ShapeDtypeStruct((NS_TOK, D_MODEL), F32)],
        scratch_shapes=[pltpu.VMEM((2, 2, tm, D_MODEL), F32), pltpu.SemaphoreType.DMA((2,))],
        compiler_params=_params(1),
        name="moe_combine",
    )(slots, slots, weights, x1, mods, g2.reshape(1, D_MODEL), b2.reshape(1, D_MODEL), y_sorted)


def kernel(x_prompt, x_sample, cache_k, cache_v, c, c_ctx, ln_in_g, ln_in_b, ada_w, ada_b, w_in, conv_w, conv_b, w_conv_out, lam_q1, lam_k1, lam_q2, lam_k2, subln_g, w_attn_out, w_out, ln1_g, ln1_b, ln2_g, ln2_b, ffn_w_gate, ffn_w_up, ffn_w_down, moe_router, moe_w_gate, moe_w_up, moe_w_down):
    assert DEPTH == 2
    cvec = jnp.concatenate([c, c_ctx[None, :], jnp.zeros((MOD_ROWS - DEC_BATCH - 1, D_MODEL), F32)], axis=0)
    mods = _ada(cvec, ada_w, ada_b).reshape(DEPTH, MOD_ROWS, 6, D_MODEL)
    tables = _rope_tables()

    x, h = _ln_in(x_prompt.reshape(NP_TOK, D_MODEL), x_sample.reshape(NS_TOK, D_MODEL), ln_in_g, ln_in_b, mods[0])
    caches = None
    for l in range(DEPTH):
        lam_init = 0.8 - 0.6 * math.exp(-0.3 * l)
        w_l = w_in[l].astype(BF16)
        w3 = w_l[:, :3 * D_CONV]
        wqkv = w_l[:, 3 * D_CONV:3 * D_CONV + 3 * ATTN_W]
        wgate = w_l[:, 3 * D_CONV + 3 * ATTN_W:]
        lam_vecs = jnp.stack([lam_q1[l], lam_k1[l], lam_q2[l], lam_k2[l]]).astype(F32)
        g_sub = subln_g[l].reshape(1, V_DIM)

        conv_y = _conv_branch(h, w3, conv_w[l], conv_b[l])
        q, k, v, *caches = _qkv(h, wqkv, tables, l, caches)
        o_p = _attn_prompt(lam_vecs, g_sub, q, k, v, lam_init)
        o_s = _attn_sample(lam_vecs, g_sub, q, k, v, cache_k, cache_v, l, lam_init)

        mix_w = (wgate, w_conv_out[l].astype(BF16), w_attn_out[l].astype(BF16), w_out[l].astype(BF16), ln1_g[l], ln1_b[l])
        i = l // 2
        if l % 2 == 0:
            x1, h2 = _merge(h, conv_y, o_p, o_s, x, mods[l], *mix_w)
            n_slabs = D_FF // D_FF_EXPERT
            slab = lambda w: w.astype(BF16).reshape(D_MODEL, n_slabs, D_FF_EXPERT).transpose(1, 0, 2)
            x, h = _ffn(h2, slab(ffn_w_gate[i]), slab(ffn_w_up[i]),
                        ffn_w_down[i].astype(BF16).reshape(n_slabs, D_FF_EXPERT, D_MODEL),
                        x1, mods[l], ln2_g[l], ln2_b[l], mods[l + 1])
        else:
            r_t = moe_router[i].T
            r_hi = r_t.astype(BF16)
            router = jnp.concatenate([r_hi, (r_t - r_hi.astype(F32)).astype(BF16)], axis=0)
            x1, h2, route, totals = _merge(h, conv_y, o_p, o_s, x, mods[l], *mix_w, router)
            slots, weights, item_tile, item_expert, off, n_items = _routing_tables(route, totals)
            x_sorted = _dispatch(h2, slots)
            y_sorted = _moe_ffn(x_sorted, item_tile, item_expert, off, n_items,
                                moe_w_gate[i].astype(BF16), moe_w_up[i].astype(BF16), moe_w_down[i].astype(BF16))
            y_p, y_s = _combine(y_sorted, slots, weights, x1, mods[l], ln2_g[l], ln2_b[l])

    return (y_p.reshape(BATCH, SEQ, D_MODEL), y_s.reshape(DEC_BATCH, DEC_SEQ, D_MODEL), caches[0], caches[1])
```

```python
import functools
import math

import jax
import jax.numpy as jnp
from jax import lax
from jax.experimental import pallas as pl
from jax.experimental.pallas import tpu as pltpu

D_MODEL = 1024
BATCH = 32
SEQ = 256
DEPTH = 2
DEC_BATCH = 8
DEC_SEQ = 1024
PAST_LEN = 512
GRID_W = 64
D_CONV = 512
N_HEADS = 8
HEAD_DIM = 64
V_DIM = 2 * HEAD_DIM
ATTN_W = N_HEADS * V_DIM
AXIS_DIM = HEAD_DIM // 2
ROPE_BASE = 10000.0
D_FF = 2816
N_EXPERTS = 8
D_FF_EXPERT = 1408
ALPHA = (2 * DEPTH) ** 0.25
LN_EPS = 1e-5
QK_SCALE = HEAD_DIM ** -0.5 * math.log2(math.e)

NP_TOK = BATCH * SEQ
NS_TOK = DEC_BATCH * DEC_SEQ
N_TOK = NP_TOK + NS_TOK
MOD_ROWS = 16
CTX_ROW = DEC_BATCH
LANES = 128
VMEM_LIMIT = 56 * 1024 * 1024

F32 = jnp.float32
BF16 = jnp.bfloat16
_NT = (((1,), (1,)), ((), ()))


def _params(n_axes, vmem=VMEM_LIMIT):
    return pltpu.CompilerParams(dimension_semantics=("arbitrary",) * n_axes, vmem_limit_bytes=vmem)


def _resident(shape):
    return pl.BlockSpec(shape, lambda *_: (0,) * len(shape), pipeline_mode=pl.Buffered(1))


def _mod_row(i, tm):
    n_p = NP_TOK // tm
    return jnp.where(i < n_p, CTX_ROW, (i - n_p) // (DEC_SEQ // tm))


def _mod_spec(tm):
    return pl.BlockSpec((1, 6, D_MODEL), lambda i, *_: (_mod_row(i, tm), 0, 0))


def _layer_norm(x, g, b):
    mu = jnp.mean(x, axis=-1, keepdims=True)
    xc = x - mu
    var = jnp.mean(xc * xc, axis=-1, keepdims=True)
    return xc * lax.rsqrt(var + LN_EPS) * g + b


def _ada_kernel(c_ref, w_ref, b_ref, o_ref):
    c = c_ref[...]
    a = (c * jax.nn.sigmoid(c)).astype(BF16)
    o_ref[0] = jnp.dot(a, w_ref[0].astype(BF16), preferred_element_type=F32) + b_ref[0]


def _ada(cvec, ada_w, ada_b):
    tn = 1024
    return pl.pallas_call(
        _ada_kernel,
        grid=(DEPTH, 6 * D_MODEL // tn),
        in_specs=[
            pl.BlockSpec((MOD_ROWS, D_MODEL), lambda l, j: (0, 0)),
            pl.BlockSpec((1, D_MODEL, tn), lambda l, j: (l, 0, j)),
            pl.BlockSpec((1, 1, tn), lambda l, j: (l, 0, j)),
        ],
        out_specs=pl.BlockSpec((1, MOD_ROWS, tn), lambda l, j: (l, 0, j)),
        out_shape=jax.ShapeDtypeStruct((DEPTH, MOD_ROWS, 6 * D_MODEL), F32),
        compiler_params=_params(2),
        name="ada",
    )(cvec, ada_w, ada_b.reshape(DEPTH, 1, 6 * D_MODEL))


def _ln_in_kernel(xp_ref, xs_ref, g_ref, b_ref, mod_ref, x_ref, h_ref, *, n_p):
    i = pl.program_id(0)

    def emit(src_ref):
        y = _layer_norm(src_ref[...], g_ref[...], b_ref[...])
        x_ref[...] = y
        h_ref[...] = (y * (1 + mod_ref[0, 1:2, :]) + mod_ref[0, 0:1, :]).astype(BF16)

    @pl.when(i < n_p)
    def _():
        emit(xp_ref)

    @pl.when(i >= n_p)
    def _():
        emit(xs_ref)


def _ln_in(xp, xs, g, b, mods):
    tm = 512
    n_p = NP_TOK // tm
    tile = lambda i: (i, 0)
    return pl.pallas_call(
        functools.partial(_ln_in_kernel, n_p=n_p),
        grid=(N_TOK // tm,),
        in_specs=[
            pl.BlockSpec((tm, D_MODEL), lambda i: (jnp.minimum(i, n_p - 1), 0)),
            pl.BlockSpec((tm, D_MODEL), lambda i: (jnp.maximum(i - n_p, 0), 0)),
            pl.BlockSpec((1, D_MODEL), lambda i: (0, 0)),
            pl.BlockSpec((1, D_MODEL), lambda i: (0, 0)),
            _mod_spec(tm),
        ],
        out_specs=[pl.BlockSpec((tm, D_MODEL), tile), pl.BlockSpec((tm, D_MODEL), tile)],
        out_shape=[jax.ShapeDtypeStruct((N_TOK, D_MODEL), F32), jax.ShapeDtypeStruct((N_TOK, D_MODEL), BF16)],
        compiler_params=_params(1),
        name="ln_in",
    )(xp, xs, g.reshape(1, D_MODEL), b.reshape(1, D_MODEL), mods)


CONV_CHUNK = 256


def _conv_kernel(h_ref, w_ref, cw_ref, cb_ref, y_ref, *, tm):
    i = pl.program_id(0)
    h = h_ref[...]
    seq = jnp.where(i < NP_TOK // tm, SEQ, DEC_SEQ)
    pos = lax.broadcasted_iota(jnp.int32, (tm, 1), 0) & (seq - 1)
    proj = lambda c: tuple(jnp.dot(h, w_ref[:, part * D_CONV + c:part * D_CONV + c + CONV_CHUNK],
                                   preferred_element_type=F32) for part in range(3))
    pending = proj(0)
    for c in range(0, D_CONV, CONV_CHUNK):
        gate_b, gate_c, u = pending
        if c + CONV_CHUNK < D_CONV:
            pending = proj(c + CONV_CHUNK)
        cols = slice(c, c + CONV_CHUNK)
        pc = gate_c * u
        prev = jnp.where(pos == 0, 0.0, pltpu.roll(pc, 1, axis=0))
        nxt = jnp.where(pos == seq - 1, 0.0, pltpu.roll(pc, tm - 1, axis=0))
        conv = prev * cw_ref[0:1, cols] + pc * cw_ref[1:2, cols] + nxt * cw_ref[2:3, cols] + cb_ref[:, cols]
        y_ref[:, cols] = (gate_b * conv).astype(BF16)


def _conv_branch(h, w3, conv_w, conv_b):
    tm = DEC_SEQ
    return pl.pallas_call(
        functools.partial(_conv_kernel, tm=tm),
        grid=(N_TOK // tm,),
        in_specs=[
            pl.BlockSpec((tm, D_MODEL), lambda i: (i, 0)),
            _resident((D_MODEL, 3 * D_CONV)),
            pl.BlockSpec((3, D_CONV), lambda i: (0, 0)),
            pl.BlockSpec((1, D_CONV), lambda i: (0, 0)),
        ],
        out_specs=pl.BlockSpec((tm, D_CONV), lambda i: (i, 0)),
        out_shape=jax.ShapeDtypeStruct((N_TOK, D_CONV), BF16),
        compiler_params=_params(1),
        name="conv_branch",
    )(h, w3, conv_w, conv_b.reshape(1, D_CONV))


def _rope_tables():
    pos = jnp.arange(DEC_SEQ)
    row = (pos // GRID_W).astype(F32)
    col = (pos % GRID_W).astype(F32)
    inv_freq = ROPE_BASE ** (-jnp.arange(0, AXIS_DIM, 2, dtype=F32) / AXIS_DIM)
    ang_r = row[:, None] * inv_freq
    ang_c = col[:, None] * inv_freq
    lane = jnp.arange(V_DIM)
    sub = lane % HEAD_DIM
    ang = jnp.where((sub < AXIS_DIM)[None, :], ang_r[:, lane % (AXIS_DIM // 2)], ang_c[:, lane % (AXIS_DIM // 2)])
    first = ((lane % AXIS_DIM) < AXIS_DIM // 2)[None, :]
    cos, sin = jnp.cos(ang), jnp.sin(ang)
    return cos, jnp.where(first, -sin, 0.0), jnp.where(first, 0.0, sin)


QKV_CHUNK = 2 * V_DIM


def _project_chunks(h, w_ref, emit):
    n = w_ref.shape[-1] // QKV_CHUNK
    proj = lambda c: jnp.dot(h, w_ref[:, c * QKV_CHUNK:(c + 1) * QKV_CHUNK], preferred_element_type=F32)
    y_next = proj(0)
    for c in range(n):
        y = y_next
        if c + 1 < n:
            y_next = proj(c + 1)
        emit(c, y)


def _qkv_prompt_kernel(h_ref, w_ref, *rest, tm, layer, first):
    q_ref, k_ref, v_ref, kc_all, vc_all = rest[-5:]
    kc_ref, vc_ref = (kc_all.at[:, layer], vc_all.at[:, layer]) if first else (kc_all, vc_all)
    per_part = ATTN_W // QKV_CHUNK

    def emit(c, y):
        part, cols = c // per_part, slice((c % per_part) * QKV_CHUNK, (c % per_part + 1) * QKV_CHUNK)
        if part == 0:
            q_ref[:, cols] = (y * QK_SCALE).astype(BF16)
            return
        act_ref, cache_ref = (k_ref, kc_ref) if part == 1 else (v_ref, vc_ref)
        act_ref[:, cols] = y.astype(BF16)
        for s in range(tm // SEQ):
            for j in range(QKV_CHUNK // V_DIM):
                cache_ref[s, (c % per_part) * (QKV_CHUNK // V_DIM) + j] = y[s * SEQ:(s + 1) * SEQ, j * V_DIM:(j + 1) * V_DIM]

    _project_chunks(h_ref[...], w_ref, emit)
    if first:
        for other in range(DEPTH):
            if other != layer:
                kc_all[:, other] = jnp.zeros((tm // SEQ, N_HEADS, SEQ, V_DIM), F32)
                vc_all[:, other] = jnp.zeros((tm // SEQ, N_HEADS, SEQ, V_DIM), F32)


def _qkv_sample_kernel(h_ref, w_ref, cos_ref, sup_ref, sdn_ref, q_ref, k_ref, v_ref):
    cos, s_up, s_dn = cos_ref[...], sup_ref[...], sdn_ref[...]
    per_part = ATTN_W // QKV_CHUNK

    def rope(x):
        return x * cos + pltpu.roll(x, V_DIM - AXIS_DIM // 2, axis=1) * s_up + pltpu.roll(x, AXIS_DIM // 2, axis=1) * s_dn

    def emit(c, y):
        part, c0 = c // per_part, (c % per_part) * QKV_CHUNK
        if part == 2:
            v_ref[:, c0:c0 + QKV_CHUNK] = y.astype(BF16)
            return
        for j in range(QKV_CHUNK // V_DIM):
            r = rope(y[:, j * V_DIM:(j + 1) * V_DIM])
            if part == 0:
                q_ref[:, c0 + j * V_DIM:c0 + (j + 1) * V_DIM] = (r * QK_SCALE).astype(BF16)
            else:
                k_ref[:, c0 + j * V_DIM:c0 + (j + 1) * V_DIM] = r.astype(BF16)

    _project_chunks(h_ref[...], w_ref, emit)


def _qkv_prompt(h, wqkv, layer, caches):
    tm = 512
    tile = pl.BlockSpec((tm, ATTN_W), lambda i: (i, 0))
    first = caches is None
    if first:
        cache = pl.BlockSpec((tm // SEQ, DEPTH, N_HEADS, SEQ, V_DIM), lambda i: (i, 0, 0, 0, 0))
    else:
        cache = pl.BlockSpec((tm // SEQ, None, N_HEADS, SEQ, V_DIM), lambda i: (i, layer, 0, 0, 0))
    act = jax.ShapeDtypeStruct((NP_TOK, ATTN_W), BF16)
    ctx = jax.ShapeDtypeStruct((BATCH, DEPTH, N_HEADS, SEQ, V_DIM), F32)
    in_specs = [pl.BlockSpec((tm, D_MODEL), lambda i: (i, 0)), _resident((D_MODEL, 3 * ATTN_W))]
    args = [h, wqkv]
    aliases = {}
    if not first:
        aliases = {len(args): 3, len(args) + 1: 4}
        in_specs += [pl.BlockSpec(memory_space=pl.ANY)] * 2
        args += list(caches)
    return pl.pallas_call(
        functools.partial(_qkv_prompt_kernel, tm=tm, layer=layer, first=first),
        grid=(NP_TOK // tm,),
        in_specs=in_specs,
        out_specs=[tile, tile, tile, cache, cache],
        out_shape=[act, act, act, ctx, ctx],
        input_output_aliases=aliases,
        compiler_params=_params(1),
        name="qkv_prompt",
    )(*args)


def _qkv_sample(h, wqkv, tables):
    tm = 512
    first_tile = NP_TOK // tm
    tile = pl.BlockSpec((tm, ATTN_W), lambda i: (i, 0))
    tab = pl.BlockSpec((tm, V_DIM), lambda i: (i % (DEC_SEQ // tm), 0))
    act = jax.ShapeDtypeStruct((NS_TOK, ATTN_W), BF16)
    return pl.pallas_call(
        _qkv_sample_kernel,
        grid=(NS_TOK // tm,),
        in_specs=[pl.BlockSpec((tm, D_MODEL), lambda i: (first_tile + i, 0)), _resident((D_MODEL, 3 * ATTN_W)), tab, tab, tab],
        out_specs=[tile, tile, tile],
        out_shape=[act, act, act],
        compiler_params=_params(1),
        name="qkv_sample",
    )(h, wqkv, *tables)


def _lam(lam_ref, lam_init):
    a = jnp.sum(lam_ref[0:1, :] * lam_ref[1:2, :], axis=1, keepdims=True)
    b = jnp.sum(lam_ref[2:3, :] * lam_ref[3:4, :], axis=1, keepdims=True)
    return jnp.exp(a) - jnp.exp(b) + lam_init


def _scores(q, k):
    lo = lax.broadcasted_iota(jnp.int32, (1, V_DIM), 1) < HEAD_DIM
    zero = jnp.zeros_like(q)
    qq = jnp.concatenate([jnp.where(lo, q, zero), jnp.where(lo, zero, q)], axis=0)
    return lax.dot_general(qq, k, _NT, preferred_element_type=F32)


def _diff_probs(s, lam):
    tq = s.shape[0] // 2
    e = jnp.exp2(s - jnp.max(s, axis=-1, keepdims=True))
    l = jnp.sum(e, axis=-1, keepdims=True)
    p = e[:tq] - e[tq:] * (lam * l[:tq] / l[tq:])
    return p.astype(BF16), 1.0 / l[:tq]


def _head_out(p, inv_l1, v, g, lam_init):
    o = jnp.dot(p, v, preferred_element_type=F32) * inv_l1
    ms = jnp.mean(o * o, axis=-1, keepdims=True)
    return (o * lax.rsqrt(ms + LN_EPS) * g * (1 - lam_init)).astype(BF16)


def _diff_attn_tiles(tiles, lam, g, lam_init):
    n = len(tiles)
    scores = lambda t: _scores(tiles[t][0](), tiles[t][1]())
    s = {0: scores(0)}
    probs = {}
    for t in range(-1, n):
        if t + 2 < n:
            s[t + 2] = scores(t + 2)
        if t + 1 < n:
            if t + 1 not in s:
                s[t + 1] = scores(t + 1)
            probs[t + 1] = _diff_probs(s.pop(t + 1), lam)
        if t >= 0:
            p, inv_l1 = probs.pop(t)
            tiles[t][3](_head_out(p, inv_l1, tiles[t][2](), g, lam_init))


def _attn_prompt_kernel(lam_ref, g_ref, q_ref, k_ref, v_ref, o_ref, *, lam_init):
    def tile(hd):
        sl = slice(hd * V_DIM, (hd + 1) * V_DIM)

        def store(o):
            o_ref[:, sl] = o

        return (lambda: q_ref[:, sl], lambda: k_ref[:, sl], lambda: v_ref[:, sl], store)

    _diff_attn_tiles([tile(hd) for hd in range(N_HEADS)], _lam(lam_ref, lam_init), g_ref[...], lam_init)


def _attn_prompt(lam_vecs, g, q, k, v, lam_init):
    blk = pl.BlockSpec((SEQ, ATTN_W), lambda b: (b, 0))
    return pl.pallas_call(
        functools.partial(_attn_prompt_kernel, lam_init=lam_init),
        grid=(BATCH,),
        in_specs=[pl.BlockSpec((4, HEAD_DIM), lambda b: (0, 0)), pl.BlockSpec((1, V_DIM), lambda b: (0, 0)), blk, blk, blk],
        out_specs=blk,
        out_shape=jax.ShapeDtypeStruct((NP_TOK, ATTN_W), BF16),
        compiler_params=_params(1),
        name="attn_prompt",
    )(lam_vecs, g, q, k, v)


def _attn_sample_kernel(lam_ref, g_ref, q_ref, kn_ref, vn_ref, kc_ref, vc_ref, o_ref, k_s, v_s, *, lam_init, tq):
    k_s[0:PAST_LEN, :] = kc_ref[...].astype(BF16)
    k_s[PAST_LEN:, :] = kn_ref[...]
    v_s[0:PAST_LEN, :] = vc_ref[...].astype(BF16)
    v_s[PAST_LEN:, :] = vn_ref[...]

    def tile(t):
        rows = slice(t * tq, (t + 1) * tq)

        def store(o):
            o_ref[rows, :] = o

        return (lambda: q_ref[rows, :], lambda: k_s[...], lambda: v_s[...], store)

    _diff_attn_tiles([tile(t) for t in range(DEC_SEQ // tq)], _lam(lam_ref, lam_init), g_ref[...], lam_init)


def _attn_sample(lam_vecs, g, q, k, v, cache_k, cache_v, layer, lam_init):
    tq = 128
    new = pl.BlockSpec((DEC_SEQ, V_DIM), lambda b, h: (b, h))
    past = pl.BlockSpec((None, None, None, PAST_LEN, V_DIM), lambda b, h: (b, layer, h, 0, 0))
    return pl.pallas_call(
        functools.partial(_attn_sample_kernel, lam_init=lam_init, tq=tq),
        grid=(DEC_BATCH, N_HEADS),
        in_specs=[pl.BlockSpec((4, HEAD_DIM), lambda b, h: (0, 0)), pl.BlockSpec((1, V_DIM), lambda b, h: (0, 0)),
                  new, new, new, past, past],
        out_specs=pl.BlockSpec((DEC_SEQ, V_DIM), lambda b, h: (b, h)),
        out_shape=jax.ShapeDtypeStruct((NS_TOK, ATTN_W), BF16),
        scratch_shapes=[pltpu.VMEM((PAST_LEN + DEC_SEQ, V_DIM), BF16), pltpu.VMEM((PAST_LEN + DEC_SEQ, V_DIM), BF16)],
        compiler_params=_params(2),
        name="attn_sample",
    )(lam_vecs, g, q, k, v, cache_k, cache_v)


def _route(h2, router_ref, cnt_ref):
    tm = h2.shape[0]
    hi = h2.astype(BF16)
    lo = (h2 - hi.astype(F32)).astype(BF16)
    a = lax.dot_general(router_ref[...], hi, _NT, preferred_element_type=F32)
    b = lax.dot_general(router_ref[0:N_EXPERTS, :], lo, _NT, preferred_element_type=F32)
    logits = a[:N_EXPERTS] + a[N_EXPERTS:] + b
    e = jnp.exp(logits - jnp.max(logits, axis=0, keepdims=True))
    p = e / jnp.sum(e, axis=0, keepdims=True)
    row = lax.broadcasted_iota(jnp.int32, p.shape, 0)
    v1 = jnp.max(p, axis=0, keepdims=True)
    i1 = jnp.min(jnp.where(p == v1, row, N_EXPERTS), axis=0, keepdims=True)
    p2 = jnp.where(row == i1, -1.0, p)
    v2 = jnp.max(p2, axis=0, keepdims=True)
    i2 = jnp.min(jnp.where(p2 == v2, row, N_EXPERTS), axis=0, keepdims=True)
    den = v1 + v2
    pick1, pick2 = row == i1, row == i2
    picked = jnp.where(pick1 | pick2, 1.0, 0.0)
    before = lax.broadcasted_iota(jnp.int32, (tm, tm), 0) < lax.broadcasted_iota(jnp.int32, (tm, tm), 1)
    ahead = jnp.dot(picked.astype(BF16), jnp.where(before, 1.0, 0.0).astype(BF16), preferred_element_type=F32)
    ahead = ahead + cnt_ref[:, 0:1]
    rank1 = jnp.sum(jnp.where(pick1, ahead, 0.0), axis=0, keepdims=True)
    rank2 = jnp.sum(jnp.where(pick2, ahead, 0.0), axis=0, keepdims=True)
    cnt_ref[...] = cnt_ref[...] + jnp.sum(picked, axis=1, keepdims=True)
    zero = jnp.zeros_like(v1)
    return jnp.concatenate([i1.astype(F32), i2.astype(F32), rank1, rank2, v1 / den, v2 / den, zero, zero], axis=0)


def _merge_kernel(h_ref, cy_ref, op_ref, os_ref, x_ref, mod_ref, wg_ref, wc_ref, wa_ref, wo_ref, g1_ref, b1_ref,
                  *rest, tm, routed):
    if routed:
        router_ref, x1_ref, h2_ref, route_ref, total_ref, cnt_ref = rest
    else:
        x1_ref, h2_ref = rest
    i = pl.program_id(0)
    g = jnp.dot(h_ref[...], wg_ref[...], preferred_element_type=F32)
    y_conv = jnp.dot(cy_ref[...], wc_ref[...], preferred_element_type=F32)
    o = jnp.where(i < NP_TOK // tm, op_ref[...], os_ref[...])
    y_attn = jnp.dot(o, wa_ref[...], preferred_element_type=F32)
    merged = jax.nn.sigmoid(g[:, :D_MODEL]) * y_conv + jax.nn.sigmoid(g[:, D_MODEL:]) * y_attn
    m = jnp.dot(merged.astype(BF16), wo_ref[...], preferred_element_type=F32)
    x1 = _layer_norm(ALPHA * x_ref[...] + mod_ref[0, 2:3, :] * m, g1_ref[...], b1_ref[...])
    x1_ref[...] = x1
    h2 = x1 * (1 + mod_ref[0, 4:5, :]) + mod_ref[0, 3:4, :]
    h2_ref[...] = h2.astype(h2_ref.dtype)
    if routed:
        @pl.when(i == 0)
        def _():
            cnt_ref[...] = jnp.zeros_like(cnt_ref)

        route_ref[...] = _route(h2, router_ref, cnt_ref)
        total_ref[...] = cnt_ref[...]


def _merge(h, conv_y, o_p, o_s, x, mods, wg, wc, wa, wo, g1, b1, router=None):
    tm = 512
    n_p = NP_TOK // tm
    routed = router is not None
    tile = lambda w: pl.BlockSpec((tm, w), lambda i: (i, 0))
    vec = pl.BlockSpec((1, D_MODEL), lambda i: (0, 0))
    in_specs = [
        tile(D_MODEL), tile(D_CONV),
        pl.BlockSpec((tm, ATTN_W), lambda i: (jnp.minimum(i, n_p - 1), 0)),
        pl.BlockSpec((tm, ATTN_W), lambda i: (jnp.maximum(i - n_p, 0), 0)),
        tile(D_MODEL), _mod_spec(tm),
        _resident((D_MODEL, 2 * D_MODEL)), _resident((D_CONV, D_MODEL)), _resident((ATTN_W, D_MODEL)),
        _resident((D_MODEL, D_MODEL)), vec, vec,
    ]
    args = [h, conv_y, o_p, o_s, x, mods, wg, wc, wa, wo, g1.reshape(1, D_MODEL), b1.reshape(1, D_MODEL)]
    out_specs = [tile(D_MODEL), tile(D_MODEL)]
    out_shape = [jax.ShapeDtypeStruct((N_TOK, D_MODEL), F32),
                 jax.ShapeDtypeStruct((N_TOK, D_MODEL), F32 if routed else BF16)]
    scratch = []
    if routed:
        in_specs.append(_resident((2 * N_EXPERTS, D_MODEL)))
        args.append(router)
        out_specs += [pl.BlockSpec((N_EXPERTS, tm), lambda i: (0, i)), pl.BlockSpec((N_EXPERTS, LANES), lambda i: (0, 0))]
        out_shape += [jax.ShapeDtypeStruct((N_EXPERTS, N_TOK), F32), jax.ShapeDtypeStruct((N_EXPERTS, LANES), F32)]
        scratch = [pltpu.VMEM((N_EXPERTS, LANES), F32)]
    return pl.pallas_call(
        functools.partial(_merge_kernel, tm=tm, routed=routed),
        grid=(N_TOK // tm,),
        in_specs=in_specs, out_specs=out_specs, out_shape=out_shape,
        scratch_shapes=scratch,
        compiler_params=_params(1),
        name="merge_routed" if routed else "merge",
    )(*args)


FF_CHUNK = 512


def _swiglu(x, wg_ref, wu_ref, wd_ref):
    d_ff = wg_ref.shape[-1]
    bounds = [(c, min(c + FF_CHUNK, d_ff)) for c in range(0, d_ff, FF_CHUNK)]

    def up(lo, hi):
        return (jnp.dot(x, wg_ref[:, lo:hi], preferred_element_type=F32),
                jnp.dot(x, wu_ref[:, lo:hi], preferred_element_type=F32))

    f = None
    pending = up(*bounds[0])
    for c, (lo, hi) in enumerate(bounds):
        a, u = pending
        if c + 1 < len(bounds):
            pending = up(*bounds[c + 1])
        hid = (a * jax.nn.sigmoid(a) * u).astype(BF16)
        d = jnp.dot(hid, wd_ref[lo:hi, :], preferred_element_type=F32)
        f = d if f is None else f + d
    return f


def _ffn_kernel(h_ref, wg_ref, wu_ref, wd_ref, x_ref, mod_ref, g2_ref, b2_ref, nmod_ref, x2_ref, hn_ref):
    f = _swiglu(h_ref[...], wg_ref, wu_ref, wd_ref)
    x2 = _layer_norm(ALPHA * x_ref[...] + mod_ref[0, 5:6, :] * f, g2_ref[...], b2_ref[...])
    x2_ref[...] = x2
    hn_ref[...] = (x2 * (1 + nmod_ref[0, 1:2, :]) + nmod_ref[0, 0:1, :]).astype(BF16)


def _ffn(h2, wg, wu, wd, x1, mods, g2, b2, next_mods):
    tm = 512
    d_ff = wg.shape[-1]
    tile = lambda w: pl.BlockSpec((tm, w), lambda i: (i, 0))
    vec = pl.BlockSpec((1, D_MODEL), lambda i: (0, 0))
    return pl.pallas_call(
        _ffn_kernel,
        grid=(N_TOK // tm,),
        in_specs=[tile(D_MODEL), _resident((D_MODEL, d_ff)), _resident((D_MODEL, d_ff)), _resident((d_ff, D_MODEL)),
                  tile(D_MODEL), _mod_spec(tm), vec, vec, _mod_spec(tm)],
        out_specs=[tile(D_MODEL), tile(D_MODEL)],
        out_shape=[jax.ShapeDtypeStruct((N_TOK, D_MODEL), F32), jax.ShapeDtypeStruct((N_TOK, D_MODEL), BF16)],
        compiler_params=_params(1),
        name="ffn",
    )(h2, wg, wu, wd, x1, mods, g2.reshape(1, D_MODEL), b2.reshape(1, D_MODEL), next_mods)


N_PAIRS = 2 * N_TOK
SLOT_TILE = 512
N_SLOT_TILES = N_PAIRS // SLOT_TILE
N_ITEMS = N_SLOT_TILES + N_EXPERTS - 1
DMA_UNROLL = 8


def _routing_tables(route, totals):
    counts = totals[:, 0].astype(jnp.int32)
    off = jnp.concatenate([jnp.zeros((1,), jnp.int32), jnp.cumsum(counts)])
    e1, e2 = route[0].astype(jnp.int32), route[1].astype(jnp.int32)
    slots = jnp.stack([off[e1] + route[2].astype(jnp.int32), off[e2] + route[3].astype(jnp.int32)], axis=1)
    weights = jnp.stack([route[4], route[5]], axis=1)
    t0 = jnp.arange(N_SLOT_TILES, dtype=jnp.int32)[:, None] * SLOT_TILE
    live = jnp.maximum(off[None, :-1], t0) < jnp.minimum(off[None, 1:], t0 + SLOT_TILE)
    n_items = jnp.sum(live).astype(jnp.int32)
    order = jnp.nonzero(live.reshape(-1), size=N_ITEMS, fill_value=0)[0].astype(jnp.int32)
    order = jnp.where(jnp.arange(N_ITEMS) < n_items, order, order[n_items - 1])
    return slots, weights, order // N_EXPERTS, order % N_EXPERTS, off, n_items.reshape(1)


def _dispatch_kernel(slot_ref, h_ref, xs_hbm, sem, *, tm):
    def row_copy(r, k):
        return pltpu.make_async_copy(h_ref.at[pl.ds(r, 1), :], xs_hbm.at[pl.ds(slot_ref[0, 2 * r + k], 1), :], sem)

    def issue(r, carry):
        for k in range(2):
            row_copy(r, k).start(priority=k)
        return carry

    lax.fori_loop(0, tm, issue, 0, unroll=DMA_UNROLL)
    for _ in range(2):
        pltpu.make_async_copy(h_ref, xs_hbm.at[pl.ds(0, tm), :], sem).wait()


def _dispatch(h2, slots):
    tm = 512
    return pl.pallas_call(
        functools.partial(_dispatch_kernel, tm=tm),
        grid=(N_TOK // tm,),
        in_specs=[pl.BlockSpec((None, 1, 2 * tm), lambda i: (i, 0, 0), memory_space=pltpu.SMEM),
                  pl.BlockSpec((tm, D_MODEL), lambda i: (i, 0))],
        out_specs=pl.BlockSpec(memory_space=pl.ANY),
        out_shape=jax.ShapeDtypeStruct((N_PAIRS, D_MODEL), F32),
        scratch_shapes=[pltpu.SemaphoreType.DMA(())],
        compiler_params=_params(1),
        name="moe_dispatch",
    )(slots.reshape(N_TOK // tm, 1, 2 * tm), h2)


def _moe_kernel(tile_ref, expert_ref, off_ref, n_ref, x_ref, wg_ref, wu_ref, wd_ref, y_ref):
    j = pl.program_id(0)

    @pl.when(j < n_ref[0])
    def _():
        t, e = tile_ref[j], expert_ref[j]
        row = lax.broadcasted_iota(jnp.int32, (SLOT_TILE, 1), 0) + t * SLOT_TILE
        mine = (row >= off_ref[e]) & (row < off_ref[e + 1])
        f = _swiglu(x_ref[...].astype(BF16), wg_ref, wu_ref, wd_ref)
        opens_tile = (j == 0) | (tile_ref[jnp.maximum(j - 1, 0)] != t)

        @pl.when(opens_tile)
        def _():
            y_ref[...] = jnp.where(mine, f, 0.0)

        @pl.when(jnp.logical_not(opens_tile))
        def _():
            y_ref[...] = jnp.where(mine, f, y_ref[...])


def _moe_ffn(x_sorted, item_tile, item_expert, off, n_items, wg, wu, wd):
    d_ff = wg.shape[-1]
    rows = pl.BlockSpec((SLOT_TILE, D_MODEL), lambda j, it, ie, off, n: (it[j], 0))
    w_in = pl.BlockSpec((None, D_MODEL, d_ff), lambda j, it, ie, off, n: (ie[j], 0, 0))
    w_out = pl.BlockSpec((None, d_ff, D_MODEL), lambda j, it, ie, off, n: (ie[j], 0, 0))
    return pl.pallas_call(
        _moe_kernel,
        grid_spec=pltpu.PrefetchScalarGridSpec(
            num_scalar_prefetch=4, grid=(N_ITEMS,),
            in_specs=[rows, w_in, w_in, w_out], out_specs=rows),
        out_shape=jax.ShapeDtypeStruct((N_PAIRS, D_MODEL), F32),
        compiler_params=_params(1),
        name="moe_ffn",
    )(item_tile, item_expert, off, n_items, x_sorted, wg, wu, wd)


def _combine_kernel(slot_ref, next_slot_ref, w_ref, x_ref, mod_ref, g2_ref, b2_ref, y_hbm, yp_ref, ys_ref, buf, sem, *, tm):
    i = pl.program_id(0)
    cur = i % 2

    def gather(slots, b):
        def issue(r, carry):
            for k in range(2):
                pltpu.make_async_copy(y_hbm.at[pl.ds(slots[0, 2 * r + k], 1), :], buf.at[b, k, pl.ds(r, 1), :],
                                      sem.at[b]).start(priority=k)
            return carry

        lax.fori_loop(0, tm, issue, 0, unroll=DMA_UNROLL)

    @pl.when(i == 0)
    def _():
        gather(slot_ref, 0)

    @pl.when(i + 1 < pl.num_programs(0))
    def _():
        gather(next_slot_ref, 1 - cur)

    for k in range(2):
        pltpu.make_async_copy(y_hbm.at[pl.ds(0, tm), :], buf.at[cur, k], sem.at[cur]).wait()
    w = w_ref[...]
    f = w[:, 0:1] * buf[cur, 0] + w[:, 1:2] * buf[cur, 1]
    x2 = _layer_norm(ALPHA * x_ref[...] + mod_ref[0, 5:6, :] * f, g2_ref[...], b2_ref[...])

    @pl.when(i < NP_TOK // tm)
    def _():
        yp_ref[...] = x2

    @pl.when(i >= NP_TOK // tm)
    def _():
        ys_ref[...] = x2


def _combine(y_sorted, slots, weights, x1, mods, g2, b2):
    tm = 256
    n_p = NP_TOK // tm
    n_tiles = N_TOK // tm
    vec = pl.BlockSpec((1, D_MODEL), lambda i: (0, 0))
    slots = slots.reshape(n_tiles, 1, 2 * tm)
    return pl.pallas_call(
        functools.partial(_combine_kernel, tm=tm),
        grid=(n_tiles,),
        in_specs=[pl.BlockSpec((None, 1, 2 * tm), lambda i: (i, 0, 0), memory_space=pltpu.SMEM),
                  pl.BlockSpec((None, 1, 2 * tm), lambda i: (jnp.minimum(i + 1, n_tiles - 1), 0, 0), memory_space=pltpu.SMEM),
                  pl.BlockSpec((tm, 2), lambda i: (i, 0)),
                  pl.BlockSpec((tm, D_MODEL), lambda i: (i, 0)), _mod_spec(tm), vec, vec,
                  pl.BlockSpec(memory_space=pl.ANY)],
        out_specs=[pl.BlockSpec((tm, D_MODEL), lambda i: (jnp.minimum(i, n_p - 1), 0)),
                   pl.BlockSpec((tm, D_MODEL), lambda i: (jnp.maximum(i - n_p, 0), 0))],
        out_shape=[jax.ShapeDtypeStruct((NP_TOK, D_MODEL), F32), jax.ShapeDtypeStruct((NS_TOK, D_MODEL), F32)],
        scratch_shapes=[pltpu.VMEM((2, 2, tm, D_MODEL), F32), pltpu.SemaphoreType.DMA((2,))],
        compiler_params=_params(1),
        name="moe_combine",
    )(slots, slots, weights, x1, mods, g2.reshape(1, D_MODEL), b2.reshape(1, D_MODEL), y_sorted)


def kernel(x_prompt, x_sample, cache_k, cache_v, c, c_ctx, ln_in_g, ln_in_b, ada_w, ada_b, w_in, conv_w, conv_b, w_conv_out, lam_q1, lam_k1, lam_q2, lam_k2, subln_g, w_attn_out, w_out, ln1_g, ln1_b, ln2_g, ln2_b, ffn_w_gate, ffn_w_up, ffn_w_down, moe_router, moe_w_gate, moe_w_up, moe_w_down):
    assert DEPTH == 2
    cvec = jnp.concatenate([c, c_ctx[None, :], jnp.zeros((MOD_ROWS - DEC_BATCH - 1, D_MODEL), F32)], axis=0)
    mods = _ada(cvec, ada_w, ada_b).reshape(DEPTH, MOD_ROWS, 6, D_MODEL)
    tables = _rope_tables()

    x, h = _ln_in(x_prompt.reshape(NP_TOK, D_MODEL), x_sample.reshape(NS_TOK, D_MODEL), ln_in_g, ln_in_b, mods[0])
    caches = None
    for l in range(DEPTH):
        lam_init = 0.8 - 0.6 * math.exp(-0.3 * l)
        w_l = w_in[l].astype(BF16)
        w3 = w_l[:, :3 * D_CONV]
        wqkv = w_l[:, 3 * D_CONV:3 * D_CONV + 3 * ATTN_W]
        wgate = w_l[:, 3 * D_CONV + 3 * ATTN_W:]
        lam_vecs = jnp.stack([lam_q1[l], lam_k1[l], lam_q2[l], lam_k2[l]]).astype(F32)
        g_sub = subln_g[l].reshape(1, V_DIM)

        conv_y = _conv_branch(h, w3, conv_w[l], conv_b[l])
        q, k, v, *caches = _qkv_prompt(h, wqkv, l, caches)
        o_p = _attn_prompt(lam_vecs, g_sub, q, k, v, lam_init)
        q, k, v = _qkv_sample(h, wqkv, tables)
        o_s = _attn_sample(lam_vecs, g_sub, q, k, v, cache_k, cache_v, l, lam_init)

        mix_w = (wgate, w_conv_out[l].astype(BF16), w_attn_out[l].astype(BF16), w_out[l].astype(BF16), ln1_g[l], ln1_b[l])
        i = l // 2
        if l % 2 == 0:
            x1, h2 = _merge(h, conv_y, o_p, o_s, x, mods[l], *mix_w)
            x, h = _ffn(h2, ffn_w_gate[i].astype(BF16), ffn_w_up[i].astype(BF16), ffn_w_down[i].astype(BF16),
                        x1, mods[l], ln2_g[l], ln2_b[l], mods[l + 1])
        else:
            r_t = moe_router[i].T
            r_hi = r_t.astype(BF16)
            router = jnp.concatenate([r_hi, (r_t - r_hi.astype(F32)).astype(BF16)], axis=0)
            x1, h2, route, totals = _merge(h, conv_y, o_p, o_s, x, mods[l], *mix_w, router)
            slots, weights, item_tile, item_expert, off, n_items = _routing_tables(route, totals)
            x_sorted = _dispatch(h2, slots)
            y_sorted = _moe_ffn(x_sorted, item_tile, item_expert, off, n_items,
                                moe_w_gate[i].astype(BF16), moe_w_up[i].astype(BF16), moe_w_down[i].astype(BF16))
            y_p, y_s = _combine(y_sorted, slots, weights, x1, mods[l], ln2_g[l], ln2_b[l])

    return (y_p.reshape(BATCH, SEQ, D_MODEL), y_s.reshape(DEC_BATCH, DEC_SEQ, D_MODEL), caches[0], caches[1])
```

```python
import functools
import math

import jax
import jax.numpy as jnp
from jax import lax
from jax.experimental import pallas as pl
from jax.experimental.pallas import tpu as pltpu

D_MODEL = 1024
BATCH = 32
SEQ = 256
DEPTH = 2
DEC_BATCH = 8
DEC_SEQ = 1024
PAST_LEN = 512
GRID_W = 64
D_CONV = 512
N_HEADS = 8
HEAD_DIM = 64
V_DIM = 2 * HEAD_DIM
ATTN_W = N_HEADS * V_DIM
AXIS_DIM = HEAD_DIM // 2
ROPE_BASE = 10000.0
D_FF = 2816
N_EXPERTS = 8
D_FF_EXPERT = 1408
ALPHA = (2 * DEPTH) ** 0.25
LN_EPS = 1e-5
QK_SCALE = HEAD_DIM ** -0.5 * math.log2(math.e)

NP_TOK = BATCH * SEQ
NS_TOK = DEC_BATCH * DEC_SEQ
N_TOK = NP_TOK + NS_TOK
MOD_ROWS = 16
CTX_ROW = DEC_BATCH
LANES = 128
VMEM_LIMIT = 56 * 1024 * 1024

F32 = jnp.float32
BF16 = jnp.bfloat16
_NT = (((1,), (1,)), ((), ()))


def _params(n_axes, vmem=VMEM_LIMIT):
    return pltpu.CompilerParams(dimension_semantics=("arbitrary",) * n_axes, vmem_limit_bytes=vmem)


def _resident(shape):
    return pl.BlockSpec(shape, lambda *_: (0,) * len(shape), pipeline_mode=pl.Buffered(1))


def _mod_row(i, tm):
    n_p = NP_TOK // tm
    return jnp.where(i < n_p, CTX_ROW, (i - n_p) // (DEC_SEQ // tm))


def _mod_spec(tm):
    return pl.BlockSpec((1, 6, D_MODEL), lambda i, *_: (_mod_row(i, tm), 0, 0))


def _layer_norm(x, g, b):
    mu = jnp.mean(x, axis=-1, keepdims=True)
    xc = x - mu
    var = jnp.mean(xc * xc, axis=-1, keepdims=True)
    return xc * lax.rsqrt(var + LN_EPS) * g + b


def _ada_kernel(c_ref, w_ref, b_ref, o_ref):
    c = c_ref[...]
    a = (c * jax.nn.sigmoid(c)).astype(BF16)
    o_ref[0] = jnp.dot(a, w_ref[0].astype(BF16), preferred_element_type=F32) + b_ref[0]


def _ada(cvec, ada_w, ada_b):
    tn = 1024
    return pl.pallas_call(
        _ada_kernel,
        grid=(DEPTH, 6 * D_MODEL // tn),
        in_specs=[
            pl.BlockSpec((MOD_ROWS, D_MODEL), lambda l, j: (0, 0)),
            pl.BlockSpec((1, D_MODEL, tn), lambda l, j: (l, 0, j)),
            pl.BlockSpec((1, 1, tn), lambda l, j: (l, 0, j)),
        ],
        out_specs=pl.BlockSpec((1, MOD_ROWS, tn), lambda l, j: (l, 0, j)),
        out_shape=jax.ShapeDtypeStruct((DEPTH, MOD_ROWS, 6 * D_MODEL), F32),
        compiler_params=_params(2),
        name="ada",
    )(cvec, ada_w, ada_b.reshape(DEPTH, 1, 6 * D_MODEL))


def _ln_in_kernel(xp_ref, xs_ref, g_ref, b_ref, mod_ref, x_ref, h_ref, *, n_p):
    i = pl.program_id(0)

    def emit(src_ref):
        y = _layer_norm(src_ref[...], g_ref[...], b_ref[...])
        x_ref[...] = y
        h_ref[...] = (y * (1 + mod_ref[0, 1:2, :]) + mod_ref[0, 0:1, :]).astype(BF16)

    @pl.when(i < n_p)
    def _():
        emit(xp_ref)

    @pl.when(i >= n_p)
    def _():
        emit(xs_ref)


def _ln_in(xp, xs, g, b, mods):
    tm = 512
    n_p = NP_TOK // tm
    tile = lambda i: (i, 0)
    return pl.pallas_call(
        functools.partial(_ln_in_kernel, n_p=n_p),
        grid=(N_TOK // tm,),
        in_specs=[
            pl.BlockSpec((tm, D_MODEL), lambda i: (jnp.minimum(i, n_p - 1), 0)),
            pl.BlockSpec((tm, D_MODEL), lambda i: (jnp.maximum(i - n_p, 0), 0)),
            pl.BlockSpec((1, D_MODEL), lambda i: (0, 0)),
            pl.BlockSpec((1, D_MODEL), lambda i: (0, 0)),
            _mod_spec(tm),
        ],
        out_specs=[pl.BlockSpec((tm, D_MODEL), tile), pl.BlockSpec((tm, D_MODEL), tile)],
        out_shape=[jax.ShapeDtypeStruct((N_TOK, D_MODEL), F32), jax.ShapeDtypeStruct((N_TOK, D_MODEL), BF16)],
        compiler_params=_params(1),
        name="ln_in",
    )(xp, xs, g.reshape(1, D_MODEL), b.reshape(1, D_MODEL), mods)


CONV_CHUNK = 256


def _conv_kernel(h_ref, w_ref, cw_ref, cb_ref, y_ref, *, tm):
    i = pl.program_id(0)
    h = h_ref[...]
    seq = jnp.where(i < NP_TOK // tm, SEQ, DEC_SEQ)
    pos = lax.broadcasted_iota(jnp.int32, (tm, 1), 0) & (seq - 1)
    proj = lambda c: tuple(jnp.dot(h, w_ref[:, part * D_CONV + c:part * D_CONV + c + CONV_CHUNK],
                                   preferred_element_type=F32) for part in range(3))
    pending = proj(0)
    for c in range(0, D_CONV, CONV_CHUNK):
        gate_b, gate_c, u = pending
        if c + CONV_CHUNK < D_CONV:
            pending = proj(c + CONV_CHUNK)
        cols = slice(c, c + CONV_CHUNK)
        pc = gate_c * u
        prev = jnp.where(pos == 0, 0.0, pltpu.roll(pc, 1, axis=0))
        nxt = jnp.where(pos == seq - 1, 0.0, pltpu.roll(pc, tm - 1, axis=0))
        conv = prev * cw_ref[0:1, cols] + pc * cw_ref[1:2, cols] + nxt * cw_ref[2:3, cols] + cb_ref[:, cols]
        y_ref[:, cols] = (gate_b * conv).astype(BF16)


def _conv_branch(h, w3, conv_w, conv_b):
    tm = DEC_SEQ
    return pl.pallas_call(
        functools.partial(_conv_kernel, tm=tm),
        grid=(N_TOK // tm,),
        in_specs=[
            pl.BlockSpec((tm, D_MODEL), lambda i: (i, 0)),
            _resident((D_MODEL, 3 * D_CONV)),
            pl.BlockSpec((3, D_CONV), lambda i: (0, 0)),
            pl.BlockSpec((1, D_CONV), lambda i: (0, 0)),
        ],
        out_specs=pl.BlockSpec((tm, D_CONV), lambda i: (i, 0)),
        out_shape=jax.ShapeDtypeStruct((N_TOK, D_CONV), BF16),
        compiler_params=_params(1),
        name="conv_branch",
    )(h, w3, conv_w, conv_b.reshape(1, D_CONV))


def _rope_tables():
    pos = jnp.arange(DEC_SEQ)
    row = (pos // GRID_W).astype(F32)
    col = (pos % GRID_W).astype(F32)
    inv_freq = ROPE_BASE ** (-jnp.arange(0, AXIS_DIM, 2, dtype=F32) / AXIS_DIM)
    ang_r = row[:, None] * inv_freq
    ang_c = col[:, None] * inv_freq
    lane = jnp.arange(V_DIM)
    sub = lane % HEAD_DIM
    ang = jnp.where((sub < AXIS_DIM)[None, :], ang_r[:, lane % (AXIS_DIM // 2)], ang_c[:, lane % (AXIS_DIM // 2)])
    first = ((lane % AXIS_DIM) < AXIS_DIM // 2)[None, :]
    cos, sin = jnp.cos(ang), jnp.sin(ang)
    return cos, jnp.where(first, -sin, 0.0), jnp.where(first, 0.0, sin)


QKV_CHUNK = 2 * V_DIM


def _project_chunks(h, w_ref, emit):
    n = w_ref.shape[-1] // QKV_CHUNK
    proj = lambda c: jnp.dot(h, w_ref[:, c * QKV_CHUNK:(c + 1) * QKV_CHUNK], preferred_element_type=F32)
    y_next = proj(0)
    for c in range(n):
        y = y_next
        if c + 1 < n:
            y_next = proj(c + 1)
        emit(c, y)


def _qkv_prompt_kernel(h_ref, w_ref, *rest, tm, layer, first):
    q_ref, k_ref, v_ref, kc_all, vc_all = rest[-5:]
    kc_ref, vc_ref = (kc_all.at[:, layer], vc_all.at[:, layer]) if first else (kc_all, vc_all)
    per_part = ATTN_W // QKV_CHUNK

    def emit(c, y):
        part, cols = c // per_part, slice((c % per_part) * QKV_CHUNK, (c % per_part + 1) * QKV_CHUNK)
        if part == 0:
            q_ref[:, cols] = (y * QK_SCALE).astype(BF16)
            return
        act_ref, cache_ref = (k_ref, kc_ref) if part == 1 else (v_ref, vc_ref)
        act_ref[:, cols] = y.astype(BF16)
        for s in range(tm // SEQ):
            for j in range(QKV_CHUNK // V_DIM):
                cache_ref[s, (c % per_part) * (QKV_CHUNK // V_DIM) + j] = y[s * SEQ:(s + 1) * SEQ, j * V_DIM:(j + 1) * V_DIM]

    _project_chunks(h_ref[...], w_ref, emit)
    if first:
        for other in range(DEPTH):
            if other != layer:
                kc_all[:, other] = jnp.zeros((tm // SEQ, N_HEADS, SEQ, V_DIM), F32)
                vc_all[:, other] = jnp.zeros((tm // SEQ, N_HEADS, SEQ, V_DIM), F32)


def _qkv_sample_kernel(h_ref, w_ref, cos_ref, sup_ref, sdn_ref, q_ref, k_ref, v_ref):
    cos, s_up, s_dn = cos_ref[...], sup_ref[...], sdn_ref[...]
    per_part = ATTN_W // QKV_CHUNK

    def rope(x):
        return x * cos + pltpu.roll(x, V_DIM - AXIS_DIM // 2, axis=1) * s_up + pltpu.roll(x, AXIS_DIM // 2, axis=1) * s_dn

    def emit(c, y):
        part, c0 = c // per_part, (c % per_part) * QKV_CHUNK
        if part == 2:
            v_ref[:, c0:c0 + QKV_CHUNK] = y.astype(BF16)
            return
        for j in range(QKV_CHUNK // V_DIM):
            r = rope(y[:, j * V_DIM:(j + 1) * V_DIM])
            if part == 0:
                q_ref[:, c0 + j * V_DIM:c0 + (j + 1) * V_DIM] = (r * QK_SCALE).astype(BF16)
            else:
                k_ref[:, c0 + j * V_DIM:c0 + (j + 1) * V_DIM] = r.astype(BF16)

    _project_chunks(h_ref[...], w_ref, emit)


def _qkv_prompt(h, wqkv, layer, caches):
    tm = 512
    tile = pl.BlockSpec((tm, ATTN_W), lambda i: (i, 0))
    first = caches is None
    if first:
        cache = pl.BlockSpec((tm // SEQ, DEPTH, N_HEADS, SEQ, V_DIM), lambda i: (i, 0, 0, 0, 0))
    else:
        cache = pl.BlockSpec((tm // SEQ, None, N_HEADS, SEQ, V_DIM), lambda i: (i, layer, 0, 0, 0))
    act = jax.ShapeDtypeStruct((NP_TOK, ATTN_W), BF16)
    ctx = jax.ShapeDtypeStruct((BATCH, DEPTH, N_HEADS, SEQ, V_DIM), F32)
    in_specs = [pl.BlockSpec((tm, D_MODEL), lambda i: (i, 0)), _resident((D_MODEL, 3 * ATTN_W))]
    args = [h, wqkv]
    aliases = {}
    if not first:
        aliases = {len(args): 3, len(args) + 1: 4}
        in_specs += [pl.BlockSpec(memory_space=pl.ANY)] * 2
        args += list(caches)
    return pl.pallas_call(
        functools.partial(_qkv_prompt_kernel, tm=tm, layer=layer, first=first),
        grid=(NP_TOK // tm,),
        in_specs=in_specs,
        out_specs=[tile, tile, tile, cache, cache],
        out_shape=[act, act, act, ctx, ctx],
        input_output_aliases=aliases,
        compiler_params=_params(1),
        name="qkv_prompt",
    )(*args)


def _qkv_sample(h, wqkv, tables):
    tm = 512
    first_tile = NP_TOK // tm
    tile = pl.BlockSpec((tm, ATTN_W), lambda i: (i, 0))
    tab = pl.BlockSpec((tm, V_DIM), lambda i: (i % (DEC_SEQ // tm), 0))
    act = jax.ShapeDtypeStruct((NS_TOK, ATTN_W), BF16)
    return pl.pallas_call(
        _qkv_sample_kernel,
        grid=(NS_TOK // tm,),
        in_specs=[pl.BlockSpec((tm, D_MODEL), lambda i: (first_tile + i, 0)), _resident((D_MODEL, 3 * ATTN_W)), tab, tab, tab],
        out_specs=[tile, tile, tile],
        out_shape=[act, act, act],
        compiler_params=_params(1),
        name="qkv_sample",
    )(h, wqkv, *tables)


def _lam(lam_ref, lam_init):
    a = jnp.sum(lam_ref[0:1, :] * lam_ref[1:2, :], axis=1, keepdims=True)
    b = jnp.sum(lam_ref[2:3, :] * lam_ref[3:4, :], axis=1, keepdims=True)
    return jnp.exp(a) - jnp.exp(b) + lam_init


def _scores(q, k):
    lo = lax.broadcasted_iota(jnp.int32, (1, V_DIM), 1) < HEAD_DIM
    zero = jnp.zeros_like(q)
    qq = jnp.concatenate([jnp.where(lo, q, zero), jnp.where(lo, zero, q)], axis=0)
    return lax.dot_general(qq, k, _NT, preferred_element_type=F32)


def _diff_probs(s, lam):
    tq = s.shape[0] // 2
    e = jnp.exp2(s - jnp.max(s, axis=-1, keepdims=True))
    l = jnp.sum(e, axis=-1, keepdims=True)
    p = e[:tq] - e[tq:] * (lam * l[:tq] / l[tq:])
    return p.astype(BF16), 1.0 / l[:tq]


def _head_out(p, inv_l1, v, g, lam_init):
    o = jnp.dot(p, v, preferred_element_type=F32) * inv_l1
    ms = jnp.mean(o * o, axis=-1, keepdims=True)
    return (o * lax.rsqrt(ms + LN_EPS) * g * (1 - lam_init)).astype(BF16)


def _diff_attn_tiles(tiles, lam, g, lam_init):
    n = len(tiles)
    scores = lambda t: _scores(tiles[t][0](), tiles[t][1]())
    s = {0: scores(0)}
    probs = {}
    for t in range(-1, n):
        if t + 2 < n:
            s[t + 2] = scores(t + 2)
        if t + 1 < n:
            if t + 1 not in s:
                s[t + 1] = scores(t + 1)
            probs[t + 1] = _diff_probs(s.pop(t + 1), lam)
        if t >= 0:
            p, inv_l1 = probs.pop(t)
            tiles[t][3](_head_out(p, inv_l1, tiles[t][2](), g, lam_init))


ATTN_SEQS_PER_STEP = 2
ATTN_HEADS_PER_STEP = 2


def _attn_prompt_kernel(lam_ref, g_ref, q_ref, k_ref, v_ref, o_ref, *, lam_init):
    def tile(s, hd):
        rows, cols = slice(s * SEQ, (s + 1) * SEQ), slice(hd * V_DIM, (hd + 1) * V_DIM)

        def store(o):
            o_ref[rows, cols] = o

        return (lambda: q_ref[rows, cols], lambda: k_ref[rows, cols], lambda: v_ref[rows, cols], store)

    tiles = [tile(s, hd) for s in range(ATTN_SEQS_PER_STEP) for hd in range(N_HEADS)]
    _diff_attn_tiles(tiles, _lam(lam_ref, lam_init), g_ref[...], lam_init)


def _attn_prompt(lam_vecs, g, q, k, v, lam_init):
    blk = pl.BlockSpec((ATTN_SEQS_PER_STEP * SEQ, ATTN_W), lambda b: (b, 0))
    return pl.pallas_call(
        functools.partial(_attn_prompt_kernel, lam_init=lam_init),
        grid=(BATCH // ATTN_SEQS_PER_STEP,),
        in_specs=[pl.BlockSpec((4, HEAD_DIM), lambda b: (0, 0)), pl.BlockSpec((1, V_DIM), lambda b: (0, 0)), blk, blk, blk],
        out_specs=blk,
        out_shape=jax.ShapeDtypeStruct((NP_TOK, ATTN_W), BF16),
        compiler_params=_params(1),
        name="attn_prompt",
    )(lam_vecs, g, q, k, v)


def _attn_sample_kernel(lam_ref, g_ref, q_ref, kn_ref, vn_ref, kc_ref, vc_ref, o_ref, k_s, v_s, *, lam_init, tq):
    for hd in range(ATTN_HEADS_PER_STEP):
        cols = slice(hd * V_DIM, (hd + 1) * V_DIM)
        k_s[hd, 0:PAST_LEN, :] = kc_ref[hd].astype(BF16)
        k_s[hd, PAST_LEN:, :] = kn_ref[:, cols]
        v_s[hd, 0:PAST_LEN, :] = vc_ref[hd].astype(BF16)
        v_s[hd, PAST_LEN:, :] = vn_ref[:, cols]

    def tile(hd, t):
        rows, cols = slice(t * tq, (t + 1) * tq), slice(hd * V_DIM, (hd + 1) * V_DIM)

        def store(o):
            o_ref[rows, cols] = o

        return (lambda: q_ref[rows, cols], lambda: k_s[hd], lambda: v_s[hd], store)

    tiles = [tile(hd, t) for hd in range(ATTN_HEADS_PER_STEP) for t in range(DEC_SEQ // tq)]
    _diff_attn_tiles(tiles, _lam(lam_ref, lam_init), g_ref[...], lam_init)


def _attn_sample(lam_vecs, g, q, k, v, cache_k, cache_v, layer, lam_init):
    tq = 128
    hps = ATTN_HEADS_PER_STEP
    new = pl.BlockSpec((DEC_SEQ, hps * V_DIM), lambda b, h: (b, h))
    past = pl.BlockSpec((None, None, hps, PAST_LEN, V_DIM), lambda b, h: (b, layer, h, 0, 0))
    kv_all = pltpu.VMEM((hps, PAST_LEN + DEC_SEQ, V_DIM), BF16)
    return pl.pallas_call(
        functools.partial(_attn_sample_kernel, lam_init=lam_init, tq=tq),
        grid=(DEC_BATCH, N_HEADS // hps),
        in_specs=[pl.BlockSpec((4, HEAD_DIM), lambda b, h: (0, 0)), pl.BlockSpec((1, V_DIM), lambda b, h: (0, 0)),
                  new, new, new, past, past],
        out_specs=pl.BlockSpec((DEC_SEQ, hps * V_DIM), lambda b, h: (b, h)),
        out_shape=jax.ShapeDtypeStruct((NS_TOK, ATTN_W), BF16),
        scratch_shapes=[kv_all, kv_all],
        compiler_params=_params(2),
        name="attn_sample",
    )(lam_vecs, g, q, k, v, cache_k, cache_v)


MERGE_ROWS = 256


def _route(h2, router_ref, cnt_ref):
    tm = h2.shape[0]
    hi = h2.astype(BF16)
    lo = (h2 - hi.astype(F32)).astype(BF16)
    a = lax.dot_general(router_ref[...], hi, _NT, preferred_element_type=F32)
    b = lax.dot_general(router_ref[0:N_EXPERTS, :], lo, _NT, preferred_element_type=F32)
    logits = a[:N_EXPERTS] + a[N_EXPERTS:] + b
    e = jnp.exp(logits - jnp.max(logits, axis=0, keepdims=True))
    p = e / jnp.sum(e, axis=0, keepdims=True)
    row = lax.broadcasted_iota(jnp.int32, p.shape, 0)
    v1 = jnp.max(p, axis=0, keepdims=True)
    i1 = jnp.min(jnp.where(p == v1, row, N_EXPERTS), axis=0, keepdims=True)
    p2 = jnp.where(row == i1, -1.0, p)
    v2 = jnp.max(p2, axis=0, keepdims=True)
    i2 = jnp.min(jnp.where(p2 == v2, row, N_EXPERTS), axis=0, keepdims=True)
    den = v1 + v2
    pick1, pick2 = row == i1, row == i2
    picked = jnp.where(pick1 | pick2, 1.0, 0.0)
    before = lax.broadcasted_iota(jnp.int32, (tm, tm), 0) < lax.broadcasted_iota(jnp.int32, (tm, tm), 1)
    ahead = jnp.dot(picked.astype(BF16), jnp.where(before, 1.0, 0.0).astype(BF16), preferred_element_type=F32)
    ahead = ahead + cnt_ref[:, 0:1]
    rank1 = jnp.sum(jnp.where(pick1, ahead, 0.0), axis=0, keepdims=True)
    rank2 = jnp.sum(jnp.where(pick2, ahead, 0.0), axis=0, keepdims=True)
    cnt_ref[...] = cnt_ref[...] + jnp.sum(picked, axis=1, keepdims=True)
    zero = jnp.zeros_like(v1)
    return jnp.concatenate([i1.astype(F32), i2.astype(F32), rank1, rank2, v1 / den, v2 / den, zero, zero], axis=0)


def _merge_kernel(h_ref, cy_ref, op_ref, os_ref, x_ref, mod_ref, wg_ref, wc_ref, wa_ref, wo_ref, g1_ref, b1_ref,
                  *rest, tm, routed):
    if routed:
        router_ref, x1_ref, h2_ref, route_ref, total_ref, cnt_ref = rest
    else:
        x1_ref, h2_ref = rest
    i = pl.program_id(0)
    if routed:
        @pl.when(i == 0)
        def _():
            cnt_ref[...] = jnp.zeros_like(cnt_ref)

    def mix(rows):
        g = jnp.dot(h_ref[rows, :], wg_ref[...], preferred_element_type=F32)
        y_conv = jnp.dot(cy_ref[rows, :], wc_ref[...], preferred_element_type=F32)
        o = jnp.where(i < NP_TOK // tm, op_ref[rows, :], os_ref[rows, :])
        y_attn = jnp.dot(o, wa_ref[...], preferred_element_type=F32)
        merged = jax.nn.sigmoid(g[:, :D_MODEL]) * y_conv + jax.nn.sigmoid(g[:, D_MODEL:]) * y_attn
        return jnp.dot(merged.astype(BF16), wo_ref[...], preferred_element_type=F32)

    def finish(rows, m):
        x1 = _layer_norm(ALPHA * x_ref[rows, :] + mod_ref[0, 2:3, :] * m, g1_ref[...], b1_ref[...])
        x1_ref[rows, :] = x1
        h2 = x1 * (1 + mod_ref[0, 4:5, :]) + mod_ref[0, 3:4, :]
        h2_ref[rows, :] = h2.astype(h2_ref.dtype)
        if routed:
            route_ref[:, rows] = _route(h2, router_ref, cnt_ref)

    blocks = [slice(r, r + MERGE_ROWS) for r in range(0, tm, MERGE_ROWS)]
    m_next = mix(blocks[0])
    for b, rows in enumerate(blocks):
        m = m_next
        if b + 1 < len(blocks):
            m_next = mix(blocks[b + 1])
        finish(rows, m)
    if routed:
        total_ref[...] = cnt_ref[...]


def _merge(h, conv_y, o_p, o_s, x, mods, wg, wc, wa, wo, g1, b1, router=None):
    tm = 512
    n_p = NP_TOK // tm
    routed = router is not None
    tile = lambda w: pl.BlockSpec((tm, w), lambda i: (i, 0))
    vec = pl.BlockSpec((1, D_MODEL), lambda i: (0, 0))
    in_specs = [
        tile(D_MODEL), tile(D_CONV),
        pl.BlockSpec((tm, ATTN_W), lambda i: (jnp.minimum(i, n_p - 1), 0)),
        pl.BlockSpec((tm, ATTN_W), lambda i: (jnp.maximum(i - n_p, 0), 0)),
        tile(D_MODEL), _mod_spec(tm),
        _resident((D_MODEL, 2 * D_MODEL)), _resident((D_CONV, D_MODEL)), _resident((ATTN_W, D_MODEL)),
        _resident((D_MODEL, D_MODEL)), vec, vec,
    ]
    args = [h, conv_y, o_p, o_s, x, mods, wg, wc, wa, wo, g1.reshape(1, D_MODEL), b1.reshape(1, D_MODEL)]
    out_specs = [tile(D_MODEL), tile(D_MODEL)]
    out_shape = [jax.ShapeDtypeStruct((N_TOK, D_MODEL), F32),
                 jax.ShapeDtypeStruct((N_TOK, D_MODEL), F32 if routed else BF16)]
    scratch = []
    if routed:
        in_specs.append(_resident((2 * N_EXPERTS, D_MODEL)))
        args.append(router)
        out_specs += [pl.BlockSpec((N_EXPERTS, tm), lambda i: (0, i)), pl.BlockSpec((N_EXPERTS, LANES), lambda i: (0, 0))]
        out_shape += [jax.ShapeDtypeStruct((N_EXPERTS, N_TOK), F32), jax.ShapeDtypeStruct((N_EXPERTS, LANES), F32)]
        scratch = [pltpu.VMEM((N_EXPERTS, LANES), F32)]
    return pl.pallas_call(
        functools.partial(_merge_kernel, tm=tm, routed=routed),
        grid=(N_TOK // tm,),
        in_specs=in_specs, out_specs=out_specs, out_shape=out_shape,
        scratch_shapes=scratch,
        compiler_params=_params(1),
        name="merge_routed" if routed else "merge",
    )(*args)


FF_CHUNK = 512


def _swiglu(x, wg_ref, wu_ref, wd_ref):
    d_ff = wg_ref.shape[-1]
    bounds = [(c, min(c + FF_CHUNK, d_ff)) for c in range(0, d_ff, FF_CHUNK)]

    def up(lo, hi):
        return (jnp.dot(x, wg_ref[:, lo:hi], preferred_element_type=F32),
                jnp.dot(x, wu_ref[:, lo:hi], preferred_element_type=F32))

    f = None
    pending = up(*bounds[0])
    for c, (lo, hi) in enumerate(bounds):
        a, u = pending
        if c + 1 < len(bounds):
            pending = up(*bounds[c + 1])
        hid = (a * jax.nn.sigmoid(a) * u).astype(BF16)
        d = jnp.dot(hid, wd_ref[lo:hi, :], preferred_element_type=F32)
        f = d if f is None else f + d
    return f


def _ffn_kernel(h_ref, wg_ref, wu_ref, wd_ref, x_ref, mod_ref, g2_ref, b2_ref, nmod_ref, x2_ref, hn_ref):
    def finish(rows, f):
        x2 = _layer_norm(ALPHA * x_ref[rows, :] + mod_ref[0, 5:6, :] * f, g2_ref[...], b2_ref[...])
        x2_ref[rows, :] = x2
        hn_ref[rows, :] = (x2 * (1 + nmod_ref[0, 1:2, :]) + nmod_ref[0, 0:1, :]).astype(BF16)

    blocks = [slice(r, r + MERGE_ROWS) for r in range(0, h_ref.shape[0], MERGE_ROWS)]
    f_next = _swiglu(h_ref[blocks[0], :], wg_ref, wu_ref, wd_ref)
    for b, rows in enumerate(blocks):
        f = f_next
        if b + 1 < len(blocks):
            f_next = _swiglu(h_ref[blocks[b + 1], :], wg_ref, wu_ref, wd_ref)
        finish(rows, f)


def _ffn(h2, wg, wu, wd, x1, mods, g2, b2, next_mods):
    tm = 512
    d_ff = wg.shape[-1]
    tile = lambda w: pl.BlockSpec((tm, w), lambda i: (i, 0))
    vec = pl.BlockSpec((1, D_MODEL), lambda i: (0, 0))
    return pl.pallas_call(
        _ffn_kernel,
        grid=(N_TOK // tm,),
        in_specs=[tile(D_MODEL), _resident((D_MODEL, d_ff)), _resident((D_MODEL, d_ff)), _resident((d_ff, D_MODEL)),
                  tile(D_MODEL), _mod_spec(tm), vec, vec, _mod_spec(tm)],
        out_specs=[tile(D_MODEL), tile(D_MODEL)],
        out_shape=[jax.ShapeDtypeStruct((N_TOK, D_MODEL), F32), jax.ShapeDtypeStruct((N_TOK, D_MODEL), BF16)],
        compiler_params=_params(1),
        name="ffn",
    )(h2, wg, wu, wd, x1, mods, g2.reshape(1, D_MODEL), b2.reshape(1, D_MODEL), next_mods)


N_PAIRS = 2 * N_TOK
SLOT_TILE = 512
N_SLOT_TILES = N_PAIRS // SLOT_TILE
N_ITEMS = N_SLOT_TILES + N_EXPERTS - 1
DMA_UNROLL = 8


def _routing_tables(route, totals):
    counts = totals[:, 0].astype(jnp.int32)
    off = jnp.concatenate([jnp.zeros((1,), jnp.int32), jnp.cumsum(counts)])
    e1, e2 = route[0].astype(jnp.int32), route[1].astype(jnp.int32)
    slots = jnp.stack([off[e1] + route[2].astype(jnp.int32), off[e2] + route[3].astype(jnp.int32)], axis=1)
    weights = jnp.stack([route[4], route[5]], axis=1)
    t0 = jnp.arange(N_SLOT_TILES, dtype=jnp.int32)[:, None] * SLOT_TILE
    live = jnp.maximum(off[None, :-1], t0) < jnp.minimum(off[None, 1:], t0 + SLOT_TILE)
    n_items = jnp.sum(live).astype(jnp.int32)
    order = jnp.nonzero(live.reshape(-1), size=N_ITEMS, fill_value=0)[0].astype(jnp.int32)
    order = jnp.where(jnp.arange(N_ITEMS) < n_items, order, order[n_items - 1])
    return slots, weights, order // N_EXPERTS, order % N_EXPERTS, off, n_items.reshape(1)


def _dispatch_kernel(slot_ref, h_ref, xs_hbm, sem, *, tm):
    def row_copy(r, k):
        return pltpu.make_async_copy(h_ref.at[pl.ds(r, 1), :], xs_hbm.at[pl.ds(slot_ref[0, 2 * r + k], 1), :], sem)

    def issue(r, carry):
        for k in range(2):
            row_copy(r, k).start(priority=k)
        return carry

    lax.fori_loop(0, tm, issue, 0, unroll=DMA_UNROLL)
    for _ in range(2):
        pltpu.make_async_copy(h_ref, xs_hbm.at[pl.ds(0, tm), :], sem).wait()


def _dispatch(h2, slots):
    tm = 512
    return pl.pallas_call(
        functools.partial(_dispatch_kernel, tm=tm),
        grid=(N_TOK // tm,),
        in_specs=[pl.BlockSpec((None, 1, 2 * tm), lambda i: (i, 0, 0), memory_space=pltpu.SMEM),
                  pl.BlockSpec((tm, D_MODEL), lambda i: (i, 0))],
        out_specs=pl.BlockSpec(memory_space=pl.ANY),
        out_shape=jax.ShapeDtypeStruct((N_PAIRS, D_MODEL), F32),
        scratch_shapes=[pltpu.SemaphoreType.DMA(())],
        compiler_params=_params(1),
        name="moe_dispatch",
    )(slots.reshape(N_TOK // tm, 1, 2 * tm), h2)


def _moe_kernel(tile_ref, expert_ref, off_ref, n_ref, x_ref, wg_ref, wu_ref, wd_ref, y_ref):
    j = pl.program_id(0)

    @pl.when(j < n_ref[0])
    def _():
        t, e = tile_ref[j], expert_ref[j]
        row = lax.broadcasted_iota(jnp.int32, (SLOT_TILE, 1), 0) + t * SLOT_TILE
        mine = (row >= off_ref[e]) & (row < off_ref[e + 1])
        f = _swiglu(x_ref[...].astype(BF16), wg_ref, wu_ref, wd_ref)
        opens_tile = (j == 0) | (tile_ref[jnp.maximum(j - 1, 0)] != t)

        @pl.when(opens_tile)
        def _():
            y_ref[...] = jnp.where(mine, f, 0.0)

        @pl.when(jnp.logical_not(opens_tile))
        def _():
            y_ref[...] = jnp.where(mine, f, y_ref[...])


def _moe_ffn(x_sorted, item_tile, item_expert, off, n_items, wg, wu, wd):
    d_ff = wg.shape[-1]
    rows = pl.BlockSpec((SLOT_TILE, D_MODEL), lambda j, it, ie, off, n: (it[j], 0))
    w_in = pl.BlockSpec((None, D_MODEL, d_ff), lambda j, it, ie, off, n: (ie[j], 0, 0))
    w_out = pl.BlockSpec((None, d_ff, D_MODEL), lambda j, it, ie, off, n: (ie[j], 0, 0))
    return pl.pallas_call(
        _moe_kernel,
        grid_spec=pltpu.PrefetchScalarGridSpec(
            num_scalar_prefetch=4, grid=(N_ITEMS,),
            in_specs=[rows, w_in, w_in, w_out], out_specs=rows),
        out_shape=jax.ShapeDtypeStruct((N_PAIRS, D_MODEL), F32),
        compiler_params=_params(1),
        name="moe_ffn",
    )(item_tile, item_expert, off, n_items, x_sorted, wg, wu, wd)


def _combine_kernel(slot_ref, next_slot_ref, w_ref, x_ref, mod_ref, g2_ref, b2_ref, y_hbm, yp_ref, ys_ref, buf, sem, *, tm):
    i = pl.program_id(0)
    cur = i % 2

    def gather(slots, b):
        def issue(r, carry):
            for k in range(2):
                pltpu.make_async_copy(y_hbm.at[pl.ds(slots[0, 2 * r + k], 1), :], buf.at[b, k, pl.ds(r, 1), :],
                                      sem.at[b]).start(priority=k)
            return carry

        lax.fori_loop(0, tm, issue, 0, unroll=DMA_UNROLL)

    @pl.when(i == 0)
    def _():
        gather(slot_ref, 0)

    @pl.when(i + 1 < pl.num_programs(0))
    def _():
        gather(next_slot_ref, 1 - cur)

    for k in range(2):
        pltpu.make_async_copy(y_hbm.at[pl.ds(0, tm), :], buf.at[cur, k], sem.at[cur]).wait()
    w = w_ref[...]
    f = w[:, 0:1] * buf[cur, 0] + w[:, 1:2] * buf[cur, 1]
    x2 = _layer_norm(ALPHA * x_ref[...] + mod_ref[0, 5:6, :] * f, g2_ref[...], b2_ref[...])

    @pl.when(i < NP_TOK // tm)
    def _():
        yp_ref[...] = x2

    @pl.when(i >= NP_TOK // tm)
    def _():
        ys_ref[...] = x2


def _combine(y_sorted, slots, weights, x1, mods, g2, b2):
    tm = 512
    n_p = NP_TOK // tm
    n_tiles = N_TOK // tm
    vec = pl.BlockSpec((1, D_MODEL), lambda i: (0, 0))
    slots = slots.reshape(n_tiles, 1, 2 * tm)
    return pl.pallas_call(
        functools.partial(_combine_kernel, tm=tm),
        grid=(n_tiles,),
        in_specs=[pl.BlockSpec((None, 1, 2 * tm), lambda i: (i, 0, 0), memory_space=pltpu.SMEM),
                  pl.BlockSpec((None, 1, 2 * tm), lambda i: (jnp.minimum(i + 1, n_tiles - 1), 0, 0), memory_space=pltpu.SMEM),
                  pl.BlockSpec((tm, 2), lambda i: (i, 0)),
                  pl.BlockSpec((tm, D_MODEL), lambda i: (i, 0)), _mod_spec(tm), vec, vec,
                  pl.BlockSpec(memory_space=pl.ANY)],
        out_specs=[pl.BlockSpec((tm, D_MODEL), lambda i: (jnp.minimum(i, n_p - 1), 0)),
                   pl.BlockSpec((tm, D_MODEL), lambda i: (jnp.maximum(i - n_p, 0), 0))],
        out_shape=[jax.ShapeDtypeStruct((NP_TOK, D_MODEL), F32), jax.ShapeDtypeStruct((NS_TOK, D_MODEL), F32)],
        scratch_shapes=[pltpu.VMEM((2, 2, tm, D_MODEL), F32), pltpu.SemaphoreType.DMA((2,))],
        compiler_params=_params(1),
        name="moe_combine",
    )(slots, slots, weights, x1, mods, g2.reshape(1, D_MODEL), b2.reshape(1, D_MODEL), y_sorted)


def kernel(x_prompt, x_sample, cache_k, cache_v, c, c_ctx, ln_in_g, ln_in_b, ada_w, ada_b, w_in, conv_w, conv_b, w_conv_out, lam_q1, lam_k1, lam_q2, lam_k2, subln_g, w_attn_out, w_out, ln1_g, ln1_b, ln2_g, ln2_b, ffn_w_gate, ffn_w_up, ffn_w_down, moe_router, moe_w_gate, moe_w_up, moe_w_down):
    assert DEPTH == 2
    cvec = jnp.concatenate([c, c_ctx[None, :], jnp.zeros((MOD_ROWS - DEC_BATCH - 1, D_MODEL), F32)], axis=0)
    mods = _ada(cvec, ada_w, ada_b).reshape(DEPTH, MOD_ROWS, 6, D_MODEL)
    tables = _rope_tables()

    x, h = _ln_in(x_prompt.reshape(NP_TOK, D_MODEL), x_sample.reshape(NS_TOK, D_MODEL), ln_in_g, ln_in_b, mods[0])
    caches = None
    for l in range(DEPTH):
        lam_init = 0.8 - 0.6 * math.exp(-0.3 * l)
        w_l = w_in[l].astype(BF16)
        w3 = w_l[:, :3 * D_CONV]
        wqkv = w_l[:, 3 * D_CONV:3 * D_CONV + 3 * ATTN_W]
        wgate = w_l[:, 3 * D_CONV + 3 * ATTN_W:]
        lam_vecs = jnp.stack([lam_q1[l], lam_k1[l], lam_q2[l], lam_k2[l]]).astype(F32)
        g_sub = subln_g[l].reshape(1, V_DIM)

        conv_y = _conv_branch(h, w3, conv_w[l], conv_b[l])
        q, k, v, *caches = _qkv_prompt(h, wqkv, l, caches)
        o_p = _attn_prompt(lam_vecs, g_sub, q, k, v, lam_init)
        q, k, v = _qkv_sample(h, wqkv, tables)
        o_s = _attn_sample(lam_vecs, g_sub, q, k, v, cache_k, cache_v, l, lam_init)

        mix_w = (wgate, w_conv_out[l].astype(BF16), w_attn_out[l].astype(BF16), w_out[l].astype(BF16), ln1_g[l], ln1_b[l])
        i = l // 2
        if l % 2 == 0:
            x1, h2 = _merge(h, conv_y, o_p, o_s, x, mods[l], *mix_w)
            x, h = _ffn(h2, ffn_w_gate[i].astype(BF16), ffn_w_up[i].astype(BF16), ffn_w_down[i].astype(BF16),
                        x1, mods[l], ln2_g[l], ln2_b[l], mods[l + 1])
        else:
            r_t = moe_router[i].T
            r_hi = r_t.astype(BF16)
            router = jnp.concatenate([r_hi, (r_t - r_hi.astype(F32)).astype(BF16)], axis=0)
            x1, h2, route, totals = _merge(h, conv_y, o_p, o_s, x, mods[l], *mix_w, router)
            slots, weights, item_tile, item_expert, off, n_items = _routing_tables(route, totals)
            x_sorted = _dispatch(h2, slots)
            y_sorted = _moe_ffn(x_sorted, item_tile, item_expert, off, n_items,
                                moe_w_gate[i].astype(BF16), moe_w_up[i].astype(BF16), moe_w_down[i].astype(BF16))
            y_p, y_s = _combine(y_sorted, slots, weights, x1, mods[l], ln2_g[l], ln2_b[l])

    return (y_p.reshape(BATCH, SEQ, D_MODEL), y_s.reshape(DEC_BATCH, DEC_SEQ, D_MODEL), caches[0], caches[1])
```

```python
import functools
import math

import jax
import jax.numpy as jnp
from jax import lax
from jax.experimental import pallas as pl
from jax.experimental.pallas import tpu as pltpu

D_MODEL = 1024
BATCH = 32
SEQ = 256
DEPTH = 2
DEC_BATCH = 8
DEC_SEQ = 1024
PAST_LEN = 512
GRID_W = 64
D_CONV = 512
N_HEADS = 8
HEAD_DIM = 64
V_DIM = 2 * HEAD_DIM
ATTN_W = N_HEADS * V_DIM
AXIS_DIM = HEAD_DIM // 2
ROPE_BASE = 10000.0
D_FF = 2816
N_EXPERTS = 8
D_FF_EXPERT = 1408
ALPHA = (2 * DEPTH) ** 0.25
LN_EPS = 1e-5
QK_SCALE = HEAD_DIM ** -0.5 * math.log2(math.e)

NP_TOK = BATCH * SEQ
NS_TOK = DEC_BATCH * DEC_SEQ
N_TOK = NP_TOK + NS_TOK
MOD_ROWS = 16
CTX_ROW = DEC_BATCH
LANES = 128
VMEM_LIMIT = 56 * 1024 * 1024

F32 = jnp.float32
BF16 = jnp.bfloat16
_NT = (((1,), (1,)), ((), ()))


def _params(n_axes, vmem=VMEM_LIMIT):
    return pltpu.CompilerParams(dimension_semantics=("arbitrary",) * n_axes, vmem_limit_bytes=vmem)


def _resident(shape):
    return pl.BlockSpec(shape, lambda *_: (0,) * len(shape), pipeline_mode=pl.Buffered(1))


def _mod_row(i, tm):
    n_p = NP_TOK // tm
    return jnp.where(i < n_p, CTX_ROW, (i - n_p) // (DEC_SEQ // tm))


def _mod_spec(tm):
    return pl.BlockSpec((1, 6, D_MODEL), lambda i, *_: (_mod_row(i, tm), 0, 0))


def _layer_norm(x, g, b):
    mu = jnp.mean(x, axis=-1, keepdims=True)
    xc = x - mu
    var = jnp.mean(xc * xc, axis=-1, keepdims=True)
    return xc * lax.rsqrt(var + LN_EPS) * g + b


def _ada_kernel(c_ref, w_ref, b_ref, o_ref):
    c = c_ref[...]
    a = (c * jax.nn.sigmoid(c)).astype(BF16)
    o_ref[0] = jnp.dot(a, w_ref[0].astype(BF16), preferred_element_type=F32) + b_ref[0]


def _ada(cvec, ada_w, ada_b):
    tn = 1024
    return pl.pallas_call(
        _ada_kernel,
        grid=(DEPTH, 6 * D_MODEL // tn),
        in_specs=[
            pl.BlockSpec((MOD_ROWS, D_MODEL), lambda l, j: (0, 0)),
            pl.BlockSpec((1, D_MODEL, tn), lambda l, j: (l, 0, j)),
            pl.BlockSpec((1, 1, tn), lambda l, j: (l, 0, j)),
        ],
        out_specs=pl.BlockSpec((1, MOD_ROWS, tn), lambda l, j: (l, 0, j)),
        out_shape=jax.ShapeDtypeStruct((DEPTH, MOD_ROWS, 6 * D_MODEL), F32),
        compiler_params=_params(2),
        name="ada",
    )(cvec, ada_w, ada_b.reshape(DEPTH, 1, 6 * D_MODEL))


def _ln_in_kernel(xp_ref, xs_ref, g_ref, b_ref, mod_ref, x_ref, h_ref, *, n_p):
    i = pl.program_id(0)

    def emit(src_ref):
        y = _layer_norm(src_ref[...], g_ref[...], b_ref[...])
        x_ref[...] = y
        h_ref[...] = (y * (1 + mod_ref[0, 1:2, :]) + mod_ref[0, 0:1, :]).astype(BF16)

    @pl.when(i < n_p)
    def _():
        emit(xp_ref)

    @pl.when(i >= n_p)
    def _():
        emit(xs_ref)


def _ln_in(xp, xs, g, b, mods):
    tm = 512
    n_p = NP_TOK // tm
    tile = lambda i: (i, 0)
    return pl.pallas_call(
        functools.partial(_ln_in_kernel, n_p=n_p),
        grid=(N_TOK // tm,),
        in_specs=[
            pl.BlockSpec((tm, D_MODEL), lambda i: (jnp.minimum(i, n_p - 1), 0)),
            pl.BlockSpec((tm, D_MODEL), lambda i: (jnp.maximum(i - n_p, 0), 0)),
            pl.BlockSpec((1, D_MODEL), lambda i: (0, 0)),
            pl.BlockSpec((1, D_MODEL), lambda i: (0, 0)),
            _mod_spec(tm),
        ],
        out_specs=[pl.BlockSpec((tm, D_MODEL), tile), pl.BlockSpec((tm, D_MODEL), tile)],
        out_shape=[jax.ShapeDtypeStruct((N_TOK, D_MODEL), F32), jax.ShapeDtypeStruct((N_TOK, D_MODEL), BF16)],
        compiler_params=_params(1),
        name="ln_in",
    )(xp, xs, g.reshape(1, D_MODEL), b.reshape(1, D_MODEL), mods)


CONV_CHUNK = 256


def _conv_kernel(h_ref, w_ref, cw_ref, cb_ref, y_ref, *, tm):
    i = pl.program_id(0)
    h = h_ref[...]
    seq = jnp.where(i < NP_TOK // tm, SEQ, DEC_SEQ)
    pos = lax.broadcasted_iota(jnp.int32, (tm, 1), 0) & (seq - 1)
    proj = lambda c: tuple(jnp.dot(h, w_ref[:, part * D_CONV + c:part * D_CONV + c + CONV_CHUNK],
                                   preferred_element_type=F32) for part in range(3))
    pending = proj(0)
    for c in range(0, D_CONV, CONV_CHUNK):
        gate_b, gate_c, u = pending
        if c + CONV_CHUNK < D_CONV:
            pending = proj(c + CONV_CHUNK)
        cols = slice(c, c + CONV_CHUNK)
        pc = gate_c * u
        prev = jnp.where(pos == 0, 0.0, pltpu.roll(pc, 1, axis=0))
        nxt = jnp.where(pos == seq - 1, 0.0, pltpu.roll(pc, tm - 1, axis=0))
        conv = prev * cw_ref[0:1, cols] + pc * cw_ref[1:2, cols] + nxt * cw_ref[2:3, cols] + cb_ref[:, cols]
        y_ref[:, cols] = (gate_b * conv).astype(BF16)


def _conv_branch(h, w3, conv_w, conv_b):
    tm = DEC_SEQ
    return pl.pallas_call(
        functools.partial(_conv_kernel, tm=tm),
        grid=(N_TOK // tm,),
        in_specs=[
            pl.BlockSpec((tm, D_MODEL), lambda i: (i, 0)),
            _resident((D_MODEL, 3 * D_CONV)),
            pl.BlockSpec((3, D_CONV), lambda i: (0, 0)),
            pl.BlockSpec((1, D_CONV), lambda i: (0, 0)),
        ],
        out_specs=pl.BlockSpec((tm, D_CONV), lambda i: (i, 0)),
        out_shape=jax.ShapeDtypeStruct((N_TOK, D_CONV), BF16),
        compiler_params=_params(1),
        name="conv_branch",
    )(h, w3, conv_w, conv_b.reshape(1, D_CONV))


def _rope_tables():
    pos = jnp.arange(DEC_SEQ)
    row = (pos // GRID_W).astype(F32)
    col = (pos % GRID_W).astype(F32)
    inv_freq = ROPE_BASE ** (-jnp.arange(0, AXIS_DIM, 2, dtype=F32) / AXIS_DIM)
    ang_r = row[:, None] * inv_freq
    ang_c = col[:, None] * inv_freq
    lane = jnp.arange(V_DIM)
    sub = lane % HEAD_DIM
    ang = jnp.where((sub < AXIS_DIM)[None, :], ang_r[:, lane % (AXIS_DIM // 2)], ang_c[:, lane % (AXIS_DIM // 2)])
    first = ((lane % AXIS_DIM) < AXIS_DIM // 2)[None, :]
    cos, sin = jnp.cos(ang), jnp.sin(ang)
    return cos, jnp.where(first, -sin, 0.0), jnp.where(first, 0.0, sin)


QKV_CHUNK = 2 * V_DIM


def _project_chunks(h, w_ref, emit):
    n = w_ref.shape[-1] // QKV_CHUNK
    proj = lambda c: jnp.dot(h, w_ref[:, c * QKV_CHUNK:(c + 1) * QKV_CHUNK], preferred_element_type=F32)
    y_next = proj(0)
    for c in range(n):
        y = y_next
        if c + 1 < n:
            y_next = proj(c + 1)
        emit(c, y)


def _qkv_prompt_kernel(h_ref, w_ref, *rest, tm, layer, first):
    q_ref, k_ref, v_ref, kc_all, vc_all = rest[-5:]
    kc_ref, vc_ref = (kc_all.at[:, layer], vc_all.at[:, layer]) if first else (kc_all, vc_all)
    per_part = ATTN_W // QKV_CHUNK

    def emit(c, y):
        part, cols = c // per_part, slice((c % per_part) * QKV_CHUNK, (c % per_part + 1) * QKV_CHUNK)
        if part == 0:
            q_ref[:, cols] = (y * QK_SCALE).astype(BF16)
            return
        act_ref, cache_ref = (k_ref, kc_ref) if part == 1 else (v_ref, vc_ref)
        act_ref[:, cols] = y.astype(BF16)
        for s in range(tm // SEQ):
            for j in range(QKV_CHUNK // V_DIM):
                cache_ref[s, (c % per_part) * (QKV_CHUNK // V_DIM) + j] = y[s * SEQ:(s + 1) * SEQ, j * V_DIM:(j + 1) * V_DIM]

    _project_chunks(h_ref[...], w_ref, emit)
    if first:
        for other in range(DEPTH):
            if other != layer:
                kc_all[:, other] = jnp.zeros((tm // SEQ, N_HEADS, SEQ, V_DIM), F32)
                vc_all[:, other] = jnp.zeros((tm // SEQ, N_HEADS, SEQ, V_DIM), F32)


def _qkv_sample_kernel(h_ref, w_ref, cos_ref, sup_ref, sdn_ref, q_ref, k_ref, v_ref):
    cos, s_up, s_dn = cos_ref[...], sup_ref[...], sdn_ref[...]
    per_part = ATTN_W // QKV_CHUNK

    def rope(x):
        return x * cos + pltpu.roll(x, V_DIM - AXIS_DIM // 2, axis=1) * s_up + pltpu.roll(x, AXIS_DIM // 2, axis=1) * s_dn

    def emit(c, y):
        part, c0 = c // per_part, (c % per_part) * QKV_CHUNK
        if part == 2:
            v_ref[:, c0:c0 + QKV_CHUNK] = y.astype(BF16)
            return
        for j in range(QKV_CHUNK // V_DIM):
            r = rope(y[:, j * V_DIM:(j + 1) * V_DIM])
            if part == 0:
                q_ref[:, c0 + j * V_DIM:c0 + (j + 1) * V_DIM] = (r * QK_SCALE).astype(BF16)
            else:
                k_ref[:, c0 + j * V_DIM:c0 + (j + 1) * V_DIM] = r.astype(BF16)

    _project_chunks(h_ref[...], w_ref, emit)


def _qkv_prompt(h, wqkv, layer, caches):
    tm = 512
    tile = pl.BlockSpec((tm, ATTN_W), lambda i: (i, 0))
    first = caches is None
    if first:
        cache = pl.BlockSpec((tm // SEQ, DEPTH, N_HEADS, SEQ, V_DIM), lambda i: (i, 0, 0, 0, 0))
    else:
        cache = pl.BlockSpec((tm // SEQ, None, N_HEADS, SEQ, V_DIM), lambda i: (i, layer, 0, 0, 0))
    act = jax.ShapeDtypeStruct((NP_TOK, ATTN_W), BF16)
    ctx = jax.ShapeDtypeStruct((BATCH, DEPTH, N_HEADS, SEQ, V_DIM), F32)
    in_specs = [pl.BlockSpec((tm, D_MODEL), lambda i: (i, 0)), _resident((D_MODEL, 3 * ATTN_W))]
    args = [h, wqkv]
    aliases = {}
    if not first:
        aliases = {len(args): 3, len(args) + 1: 4}
        in_specs += [pl.BlockSpec(memory_space=pl.ANY)] * 2
        args += list(caches)
    return pl.pallas_call(
        functools.partial(_qkv_prompt_kernel, tm=tm, layer=layer, first=first),
        grid=(NP_TOK // tm,),
        in_specs=in_specs,
        out_specs=[tile, tile, tile, cache, cache],
        out_shape=[act, act, act, ctx, ctx],
        input_output_aliases=aliases,
        compiler_params=_params(1),
        name="qkv_prompt",
    )(*args)


def _qkv_sample(h, wqkv, tables):
    tm = 512
    first_tile = NP_TOK // tm
    tile = pl.BlockSpec((tm, ATTN_W), lambda i: (i, 0))
    tab = pl.BlockSpec((tm, V_DIM), lambda i: (i % (DEC_SEQ // tm), 0))
    act = jax.ShapeDtypeStruct((NS_TOK, ATTN_W), BF16)
    return pl.pallas_call(
        _qkv_sample_kernel,
        grid=(NS_TOK // tm,),
        in_specs=[pl.BlockSpec((tm, D_MODEL), lambda i: (first_tile + i, 0)), _resident((D_MODEL, 3 * ATTN_W)), tab, tab, tab],
        out_specs=[tile, tile, tile],
        out_shape=[act, act, act],
        compiler_params=_params(1),
        name="qkv_sample",
    )(h, wqkv, *tables)


def _lam(lam_ref, lam_init):
    a = jnp.sum(lam_ref[0:1, :] * lam_ref[1:2, :], axis=1, keepdims=True)
    b = jnp.sum(lam_ref[2:3, :] * lam_ref[3:4, :], axis=1, keepdims=True)
    return jnp.exp(a) - jnp.exp(b) + lam_init


def _scores(q, k):
    lo = lax.broadcasted_iota(jnp.int32, (1, V_DIM), 1) < HEAD_DIM
    zero = jnp.zeros_like(q)
    qq = jnp.concatenate([jnp.where(lo, q, zero), jnp.where(lo, zero, q)], axis=0)
    return lax.dot_general(qq, k, _NT, preferred_element_type=F32)


def _diff_probs(s, lam):
    tq = s.shape[0] // 2
    e = jnp.exp2(s - jnp.max(s, axis=-1, keepdims=True))
    l = jnp.sum(e, axis=-1, keepdims=True)
    p = e[:tq] - e[tq:] * (lam * l[:tq] / l[tq:])
    return p.astype(BF16), 1.0 / l[:tq]


def _head_out(p, inv_l1, v, g, lam_init):
    o = jnp.dot(p, v, preferred_element_type=F32) * inv_l1
    ms = jnp.mean(o * o, axis=-1, keepdims=True)
    return (o * lax.rsqrt(ms + LN_EPS) * g * (1 - lam_init)).astype(BF16)


def _diff_attn_tiles(tiles, lam, g, lam_init):
    n = len(tiles)
    scores = lambda t: _scores(tiles[t][0](), tiles[t][1]())
    s = {0: scores(0)}
    probs = {}
    for t in range(-1, n):
        if t + 2 < n:
            s[t + 2] = scores(t + 2)
        if t + 1 < n:
            if t + 1 not in s:
                s[t + 1] = scores(t + 1)
            probs[t + 1] = _diff_probs(s.pop(t + 1), lam)
        if t >= 0:
            p, inv_l1 = probs.pop(t)
            tiles[t][3](_head_out(p, inv_l1, tiles[t][2](), g, lam_init))


ATTN_SEQS_PER_STEP = 2
ATTN_HEADS_PER_STEP = 2


def _attn_prompt_kernel(lam_ref, g_ref, q_ref, k_ref, v_ref, o_ref, *, lam_init):
    def tile(s, hd):
        rows, cols = slice(s * SEQ, (s + 1) * SEQ), slice(hd * V_DIM, (hd + 1) * V_DIM)

        def store(o):
            o_ref[rows, cols] = o

        return (lambda: q_ref[rows, cols], lambda: k_ref[rows, cols], lambda: v_ref[rows, cols], store)

    tiles = [tile(s, hd) for s in range(ATTN_SEQS_PER_STEP) for hd in range(N_HEADS)]
    _diff_attn_tiles(tiles, _lam(lam_ref, lam_init), g_ref[...], lam_init)


def _attn_prompt(lam_vecs, g, q, k, v, lam_init):
    blk = pl.BlockSpec((ATTN_SEQS_PER_STEP * SEQ, ATTN_W), lambda b: (b, 0))
    return pl.pallas_call(
        functools.partial(_attn_prompt_kernel, lam_init=lam_init),
        grid=(BATCH // ATTN_SEQS_PER_STEP,),
        in_specs=[pl.BlockSpec((4, HEAD_DIM), lambda b: (0, 0)), pl.BlockSpec((1, V_DIM), lambda b: (0, 0)), blk, blk, blk],
        out_specs=blk,
        out_shape=jax.ShapeDtypeStruct((NP_TOK, ATTN_W), BF16),
        compiler_params=_params(1),
        name="attn_prompt",
    )(lam_vecs, g, q, k, v)


def _attn_sample_kernel(lam_ref, g_ref, q_ref, kn_ref, vn_ref, kc_ref, vc_ref, o_ref, k_s, v_s, *, lam_init, tq):
    for hd in range(ATTN_HEADS_PER_STEP):
        cols = slice(hd * V_DIM, (hd + 1) * V_DIM)
        k_s[hd, 0:PAST_LEN, :] = kc_ref[hd].astype(BF16)
        k_s[hd, PAST_LEN:, :] = kn_ref[:, cols]
        v_s[hd, 0:PAST_LEN, :] = vc_ref[hd].astype(BF16)
        v_s[hd, PAST_LEN:, :] = vn_ref[:, cols]

    def tile(hd, t):
        rows, cols = slice(t * tq, (t + 1) * tq), slice(hd * V_DIM, (hd + 1) * V_DIM)

        def store(o):
            o_ref[rows, cols] = o

        return (lambda: q_ref[rows, cols], lambda: k_s[hd], lambda: v_s[hd], store)

    tiles = [tile(hd, t) for hd in range(ATTN_HEADS_PER_STEP) for t in range(DEC_SEQ // tq)]
    _diff_attn_tiles(tiles, _lam(lam_ref, lam_init), g_ref[...], lam_init)


def _attn_sample(lam_vecs, g, q, k, v, cache_k, cache_v, layer, lam_init):
    tq = 128
    hps = ATTN_HEADS_PER_STEP
    new = pl.BlockSpec((DEC_SEQ, hps * V_DIM), lambda b, h: (b, h))
    past = pl.BlockSpec((None, None, hps, PAST_LEN, V_DIM), lambda b, h: (b, layer, h, 0, 0))
    kv_all = pltpu.VMEM((hps, PAST_LEN + DEC_SEQ, V_DIM), BF16)
    return pl.pallas_call(
        functools.partial(_attn_sample_kernel, lam_init=lam_init, tq=tq),
        grid=(DEC_BATCH, N_HEADS // hps),
        in_specs=[pl.BlockSpec((4, HEAD_DIM), lambda b, h: (0, 0)), pl.BlockSpec((1, V_DIM), lambda b, h: (0, 0)),
                  new, new, new, past, past],
        out_specs=pl.BlockSpec((DEC_SEQ, hps * V_DIM), lambda b, h: (b, h)),
        out_shape=jax.ShapeDtypeStruct((NS_TOK, ATTN_W), BF16),
        scratch_shapes=[kv_all, kv_all],
        compiler_params=_params(2),
        name="attn_sample",
    )(lam_vecs, g, q, k, v, cache_k, cache_v)


MERGE_ROWS = 256


def _route(h2, router_ref, cnt_ref):
    tm = h2.shape[0]
    hi = h2.astype(BF16)
    lo = (h2 - hi.astype(F32)).astype(BF16)
    a = lax.dot_general(router_ref[...], hi, _NT, preferred_element_type=F32)
    b = lax.dot_general(router_ref[0:N_EXPERTS, :], lo, _NT, preferred_element_type=F32)
    logits = a[:N_EXPERTS] + a[N_EXPERTS:] + b
    e = jnp.exp(logits - jnp.max(logits, axis=0, keepdims=True))
    p = e / jnp.sum(e, axis=0, keepdims=True)
    row = lax.broadcasted_iota(jnp.int32, p.shape, 0)
    v1 = jnp.max(p, axis=0, keepdims=True)
    i1 = jnp.min(jnp.where(p == v1, row, N_EXPERTS), axis=0, keepdims=True)
    p2 = jnp.where(row == i1, -1.0, p)
    v2 = jnp.max(p2, axis=0, keepdims=True)
    i2 = jnp.min(jnp.where(p2 == v2, row, N_EXPERTS), axis=0, keepdims=True)
    den = v1 + v2
    pick1, pick2 = row == i1, row == i2
    picked = jnp.where(pick1 | pick2, 1.0, 0.0)
    before = lax.broadcasted_iota(jnp.int32, (tm, tm), 0) < lax.broadcasted_iota(jnp.int32, (tm, tm), 1)
    ahead = jnp.dot(picked.astype(BF16), jnp.where(before, 1.0, 0.0).astype(BF16), preferred_element_type=F32)
    ahead = ahead + cnt_ref[:, 0:1]
    rank1 = jnp.sum(jnp.where(pick1, ahead, 0.0), axis=0, keepdims=True)
    rank2 = jnp.sum(jnp.where(pick2, ahead, 0.0), axis=0, keepdims=True)
    cnt_ref[...] = cnt_ref[...] + jnp.sum(picked, axis=1, keepdims=True)
    zero = jnp.zeros_like(v1)
    return jnp.concatenate([i1.astype(F32), i2.astype(F32), rank1, rank2, v1 / den, v2 / den, zero, zero], axis=0)


def _merge_kernel(h_ref, cy_ref, op_ref, os_ref, x_ref, mod_ref, wg_ref, wc_ref, wa_ref, wo_ref, g1_ref, b1_ref,
                  *rest, tm, routed):
    if routed:
        router_ref, x1_ref, h2_ref, route_ref, count_ref, cnt_ref = rest
    else:
        x1_ref, h2_ref = rest
    i = pl.program_id(0)
    if routed:
        cnt_ref[...] = jnp.zeros_like(cnt_ref)

    def mix(rows):
        g = jnp.dot(h_ref[rows, :], wg_ref[...], preferred_element_type=F32)
        y_conv = jnp.dot(cy_ref[rows, :], wc_ref[...], preferred_element_type=F32)
        o = jnp.where(i < NP_TOK // tm, op_ref[rows, :], os_ref[rows, :])
        y_attn = jnp.dot(o, wa_ref[...], preferred_element_type=F32)
        merged = jax.nn.sigmoid(g[:, :D_MODEL]) * y_conv + jax.nn.sigmoid(g[:, D_MODEL:]) * y_attn
        return jnp.dot(merged.astype(BF16), wo_ref[...], preferred_element_type=F32)

    def finish(rows, m):
        x1 = _layer_norm(ALPHA * x_ref[rows, :] + mod_ref[0, 2:3, :] * m, g1_ref[...], b1_ref[...])
        x1_ref[rows, :] = x1
        h2 = x1 * (1 + mod_ref[0, 4:5, :]) + mod_ref[0, 3:4, :]
        h2_ref[rows, :] = h2.astype(h2_ref.dtype)
        if routed:
            route_ref[:, rows] = _route(h2, router_ref, cnt_ref)

    blocks = [slice(r, r + MERGE_ROWS) for r in range(0, tm, MERGE_ROWS)]
    m_next = mix(blocks[0])
    for b, rows in enumerate(blocks):
        m = m_next
        if b + 1 < len(blocks):
            m_next = mix(blocks[b + 1])
        finish(rows, m)
    if routed:
        count_ref[...] = cnt_ref[...]


def _merge(h, conv_y, o_p, o_s, x, mods, wg, wc, wa, wo, g1, b1, router=None):
    tm = 512
    n_p = NP_TOK // tm
    routed = router is not None
    tile = lambda w: pl.BlockSpec((tm, w), lambda i: (i, 0))
    vec = pl.BlockSpec((1, D_MODEL), lambda i: (0, 0))
    in_specs = [
        tile(D_MODEL), tile(D_CONV),
        pl.BlockSpec((tm, ATTN_W), lambda i: (jnp.minimum(i, n_p - 1), 0)),
        pl.BlockSpec((tm, ATTN_W), lambda i: (jnp.maximum(i - n_p, 0), 0)),
        tile(D_MODEL), _mod_spec(tm),
        _resident((D_MODEL, 2 * D_MODEL)), _resident((D_CONV, D_MODEL)), _resident((ATTN_W, D_MODEL)),
        _resident((D_MODEL, D_MODEL)), vec, vec,
    ]
    args = [h, conv_y, o_p, o_s, x, mods, wg, wc, wa, wo, g1.reshape(1, D_MODEL), b1.reshape(1, D_MODEL)]
    out_specs = [tile(D_MODEL), tile(D_MODEL)]
    out_shape = [jax.ShapeDtypeStruct((N_TOK, D_MODEL), F32), jax.ShapeDtypeStruct((N_TOK, D_MODEL), BF16)]
    scratch = []
    if routed:
        in_specs.append(_resident((2 * N_EXPERTS, D_MODEL)))
        args.append(router)
        out_specs += [pl.BlockSpec((N_EXPERTS, tm), lambda i: (0, i)), pl.BlockSpec((N_EXPERTS, LANES), lambda i: (i, 0))]
        out_shape += [jax.ShapeDtypeStruct((N_EXPERTS, N_TOK), F32),
                      jax.ShapeDtypeStruct((N_TOK // tm * N_EXPERTS, LANES), F32)]
        scratch = [pltpu.VMEM((N_EXPERTS, LANES), F32)]
    return pl.pallas_call(
        functools.partial(_merge_kernel, tm=tm, routed=routed),
        grid=(N_TOK // tm,),
        in_specs=in_specs, out_specs=out_specs, out_shape=out_shape,
        scratch_shapes=scratch,
        compiler_params=_params(1),
        name="merge_routed" if routed else "merge",
    )(*args)


FF_CHUNK = 256


def _swiglu(x, wg_ref, wu_ref, wd_ref):
    d_ff = wg_ref.shape[-1]
    bounds = [(c, min(c + FF_CHUNK, d_ff)) for c in range(0, d_ff, FF_CHUNK)]

    def up(lo, hi):
        return (jnp.dot(x, wg_ref[:, lo:hi], preferred_element_type=F32),
                jnp.dot(x, wu_ref[:, lo:hi], preferred_element_type=F32))

    f = None
    pending = up(*bounds[0])
    for c, (lo, hi) in enumerate(bounds):
        a, u = pending
        if c + 1 < len(bounds):
            pending = up(*bounds[c + 1])
        hid = (a * jax.nn.sigmoid(a) * u).astype(BF16)
        d = jnp.dot(hid, wd_ref[lo:hi, :], preferred_element_type=F32)
        f = d if f is None else f + d
    return f


def _ffn_kernel(h_ref, wg_ref, wu_ref, wd_ref, x_ref, mod_ref, g2_ref, b2_ref, nmod_ref, x2_ref, hn_ref):
    def finish(rows, f):
        x2 = _layer_norm(ALPHA * x_ref[rows, :] + mod_ref[0, 5:6, :] * f, g2_ref[...], b2_ref[...])
        x2_ref[rows, :] = x2
        hn_ref[rows, :] = (x2 * (1 + nmod_ref[0, 1:2, :]) + nmod_ref[0, 0:1, :]).astype(BF16)

    blocks = [slice(r, r + MERGE_ROWS) for r in range(0, h_ref.shape[0], MERGE_ROWS)]
    f_next = _swiglu(h_ref[blocks[0], :], wg_ref, wu_ref, wd_ref)
    for b, rows in enumerate(blocks):
        f = f_next
        if b + 1 < len(blocks):
            f_next = _swiglu(h_ref[blocks[b + 1], :], wg_ref, wu_ref, wd_ref)
        finish(rows, f)


def _ffn(h2, wg, wu, wd, x1, mods, g2, b2, next_mods):
    tm = 512
    d_ff = wg.shape[-1]
    tile = lambda w: pl.BlockSpec((tm, w), lambda i: (i, 0))
    vec = pl.BlockSpec((1, D_MODEL), lambda i: (0, 0))
    return pl.pallas_call(
        _ffn_kernel,
        grid=(N_TOK // tm,),
        in_specs=[tile(D_MODEL), _resident((D_MODEL, d_ff)), _resident((D_MODEL, d_ff)), _resident((d_ff, D_MODEL)),
                  tile(D_MODEL), _mod_spec(tm), vec, vec, _mod_spec(tm)],
        out_specs=[tile(D_MODEL), tile(D_MODEL)],
        out_shape=[jax.ShapeDtypeStruct((N_TOK, D_MODEL), F32), jax.ShapeDtypeStruct((N_TOK, D_MODEL), BF16)],
        compiler_params=_params(1),
        name="ffn",
    )(h2, wg, wu, wd, x1, mods, g2.reshape(1, D_MODEL), b2.reshape(1, D_MODEL), next_mods)


N_PAIRS = 2 * N_TOK
ROUTE_TILE = 512
N_ROUTE_TILES = N_TOK // ROUTE_TILE
GROUP_ALIGN = 16
LOCAL_ROWS = 2 * ROUTE_TILE + N_EXPERTS * GROUP_ALIGN
N_SLOTS = N_PAIRS + N_ROUTE_TILES * N_EXPERTS * GROUP_ALIGN
SLOT_TILE = 512
N_SLOT_TILES = N_SLOTS // SLOT_TILE
N_ITEMS = N_SLOT_TILES + N_EXPERTS
COPY_BITS = (ROUTE_TILE // GROUP_ALIGN).bit_length()
TAIL_EXPERT = N_EXPERTS


def _routing_tables(route, counts):
    n = counts.reshape(N_ROUTE_TILES, N_EXPERTS, LANES)[:, :, 0].astype(jnp.int32)
    g = (n + GROUP_ALIGN - 1) // GROUP_ALIGN * GROUP_ALIGN
    local_start = jnp.cumsum(g, axis=1) - g
    region = jnp.concatenate([jnp.zeros((1,), jnp.int32), jnp.cumsum(jnp.sum(g, axis=0))])
    global_start = region[None, :-1] + jnp.cumsum(g, axis=0) - g
    off = jnp.concatenate([region, jnp.full((1,), N_SLOTS, jnp.int32)])
    tile_of_token = jnp.arange(N_TOK, dtype=jnp.int32) // ROUTE_TILE
    starts = local_start[tile_of_token]
    local = [jnp.take_along_axis(starts, route[k].astype(jnp.int32)[:, None], axis=1)[:, 0]
             + route[2 + k].astype(jnp.int32) for k in range(2)]
    t0 = jnp.arange(N_SLOT_TILES, dtype=jnp.int32)[:, None] * SLOT_TILE
    live = jnp.maximum(off[None, :-1], t0) < jnp.minimum(off[None, 1:], t0 + SLOT_TILE)
    n_items = jnp.sum(live).astype(jnp.int32)
    order = jnp.nonzero(live.reshape(-1), size=N_ITEMS, fill_value=0)[0].astype(jnp.int32)
    order = jnp.where(jnp.arange(N_ITEMS) < n_items, order, order[n_items - 1])
    return dict(
        units=(g // GROUP_ALIGN).reshape(-1), local_start=local_start.reshape(-1), global_start=global_start.reshape(-1),
        used=region[-1:], off=off, n_items=n_items.reshape(1),
        item_tile=order // (N_EXPERTS + 1), item_expert=order % (N_EXPERTS + 1),
        local_by_lane=jnp.stack(local), local_by_row=jnp.stack(local, axis=1),
        weight_by_lane=jnp.stack([route[4], route[5]]))


def _for_each_chunk(units, fn):
    for b in range(COPY_BITS):
        @pl.when(((units >> b) & 1) == 1)
        def _():
            fn(pl.multiple_of((units & ((1 << b) - 1)) * GROUP_ALIGN, GROUP_ALIGN), GROUP_ALIGN << b)


def _group_copies(tile, units_ref, local_ref, global_ref, make, act):
    for e in range(N_EXPERTS):
        g = tile * N_EXPERTS + e

        def chunk(off, rows, g=g):
            act(make(pl.multiple_of(local_ref[g] + off, GROUP_ALIGN), pl.multiple_of(global_ref[g] + off, GROUP_ALIGN), rows))

        _for_each_chunk(units_ref[g], chunk)


def _dispatch_kernel(units_ref, local_ref, global_ref, used_ref, loc_ref, h_ref, xs_hbm, xl, zeros, sem):
    i = pl.program_id(0)
    slot = lax.broadcasted_iota(jnp.int32, (LOCAL_ROWS, ROUTE_TILE), 0)
    one_hot = jnp.where((slot == loc_ref[0:1, :]) | (slot == loc_ref[1:2, :]), 1.0, 0.0).astype(BF16)
    xl[...] = jnp.dot(one_hot, h_ref[...], preferred_element_type=F32).astype(BF16)

    make = lambda l, g, rows: pltpu.make_async_copy(xl.at[pl.ds(l, rows), :], xs_hbm.at[pl.ds(g, rows), :], sem)
    _group_copies(i, units_ref, local_ref, global_ref, make, lambda c: c.start())
    _group_copies(i, units_ref, local_ref, global_ref, make, lambda c: c.wait())

    @pl.when(i == pl.num_programs(0) - 1)
    def _():
        zeros[...] = jnp.zeros_like(zeros)
        used = used_ref[0]
        tail = (N_SLOTS - used) // GROUP_ALIGN
        small, n_big = tail % (SLOT_TILE // GROUP_ALIGN), tail // (SLOT_TILE // GROUP_ALIGN)
        big0 = used + small * GROUP_ALIGN

        def fill(act):
            _for_each_chunk(small, lambda off, rows: act(pltpu.make_async_copy(
                zeros.at[pl.ds(0, rows), :], xs_hbm.at[pl.ds(pl.multiple_of(used + off, GROUP_ALIGN), rows), :], sem)))
            for k in range((N_SLOTS - N_PAIRS) // SLOT_TILE):
                @pl.when(k < n_big)
                def _():
                    act(pltpu.make_async_copy(
                        zeros, xs_hbm.at[pl.ds(pl.multiple_of(big0 + k * SLOT_TILE, GROUP_ALIGN), SLOT_TILE), :], sem))

        fill(lambda c: c.start())
        fill(lambda c: c.wait())


def _dispatch(h2, rt):
    return pl.pallas_call(
        _dispatch_kernel,
        grid_spec=pltpu.PrefetchScalarGridSpec(
            num_scalar_prefetch=4, grid=(N_ROUTE_TILES,),
            in_specs=[pl.BlockSpec((2, ROUTE_TILE), lambda i, *_: (0, i)),
                      pl.BlockSpec((ROUTE_TILE, D_MODEL), lambda i, *_: (i, 0))],
            out_specs=pl.BlockSpec(memory_space=pl.ANY),
            scratch_shapes=[pltpu.VMEM((LOCAL_ROWS, D_MODEL), BF16), pltpu.VMEM((SLOT_TILE, D_MODEL), BF16),
                            pltpu.SemaphoreType.DMA(())]),
        out_shape=jax.ShapeDtypeStruct((N_SLOTS, D_MODEL), BF16),
        compiler_params=_params(1),
        name="moe_dispatch",
    )(rt["units"], rt["local_start"], rt["global_start"], rt["used"], rt["local_by_lane"], h2)


def _moe_kernel(tile_ref, expert_ref, off_ref, n_ref, x_ref, wg_ref, wu_ref, wd_ref, y_ref):
    j = pl.program_id(0)

    @pl.when(j < n_ref[0])
    def _():
        t, e = tile_ref[j], expert_ref[j]
        lo, hi = off_ref[e] - t * SLOT_TILE, off_ref[e + 1] - t * SLOT_TILE
        opens_tile = (j == 0) | (tile_ref[jnp.maximum(j - 1, 0)] != t)
        is_tail = e == TAIL_EXPERT

        def run(keep_other_rows, compute):
            blocks = [slice(r, r + MERGE_ROWS) for r in range(0, SLOT_TILE, MERGE_ROWS)]
            zero_rows = jnp.zeros((MERGE_ROWS, D_MODEL), F32)
            expert = lambda rows: _swiglu(x_ref[rows, :], wg_ref, wu_ref, wd_ref) if compute else zero_rows
            f_next = expert(blocks[0])
            for b, rows in enumerate(blocks):
                f = f_next
                if b + 1 < len(blocks):
                    f_next = expert(blocks[b + 1])
                row = lax.broadcasted_iota(jnp.int32, (MERGE_ROWS, 1), 0) + rows.start
                mine = (row >= lo) & (row < hi)
                y_ref[rows, :] = jnp.where(mine, f, y_ref[rows, :] if keep_other_rows else 0.0)

        for keep in (False, True):
            for tail in (False, True):
                in_case = (jnp.logical_not(opens_tile) if keep else opens_tile) & (is_tail if tail else jnp.logical_not(is_tail))

                @pl.when(in_case)
                def _(keep=keep, tail=tail):
                    run(keep, not tail)


def _moe_ffn(x_sorted, rt, wg, wu, wd):
    d_ff = wg.shape[-1]
    rows = pl.BlockSpec((SLOT_TILE, D_MODEL), lambda j, it, ie, off, n: (it[j], 0))
    expert = lambda j, it, ie, off, n: (jnp.minimum(ie[j], N_EXPERTS - 1), 0, 0)
    w_in = pl.BlockSpec((None, D_MODEL, d_ff), expert)
    w_out = pl.BlockSpec((None, d_ff, D_MODEL), expert)
    return pl.pallas_call(
        _moe_kernel,
        grid_spec=pltpu.PrefetchScalarGridSpec(
            num_scalar_prefetch=4, grid=(N_ITEMS,),
            in_specs=[rows, w_in, w_in, w_out], out_specs=rows),
        out_shape=jax.ShapeDtypeStruct((N_SLOTS, D_MODEL), F32),
        compiler_params=_params(1),
        name="moe_ffn",
    )(rt["item_tile"], rt["item_expert"], rt["off"], rt["n_items"], x_sorted, wg, wu, wd)


def _combine_kernel(units_ref, local_ref, global_ref, loc_lane_ref, w_lane_ref, loc_row_ref, x_ref, mod_ref, g2_ref, b2_ref,
                    y_hbm, yp_ref, ys_ref, yl, sem):
    i = pl.program_id(0)
    cur = i % 2

    def gather(tile, b, act):
        make = lambda l, g, rows: pltpu.make_async_copy(y_hbm.at[pl.ds(g, rows), :], yl.at[b, pl.ds(l, rows), :], sem.at[b])
        _group_copies(tile, units_ref, local_ref, global_ref, make, act)

    @pl.when(i == 0)
    def _():
        yl[...] = jnp.zeros_like(yl)
        gather(0, 0, lambda c: c.start())

    @pl.when(i + 1 < pl.num_programs(0))
    def _():
        gather(i + 1, 1 - cur, lambda c: c.start())

    gather(i, cur, lambda c: c.wait())
    slot = lax.broadcasted_iota(jnp.int32, (LOCAL_ROWS, ROUTE_TILE), 0)
    gate = jnp.sum(jnp.where(slot == loc_lane_ref[0:1, :], w_lane_ref[0:1, :], 0.0)
                   + jnp.where(slot == loc_lane_ref[1:2, :], w_lane_ref[1:2, :], 0.0), axis=1, keepdims=True)
    y = yl[cur] * gate
    hi = y.astype(BF16)
    lo = (y - hi.astype(F32)).astype(BF16)
    slot_t = lax.broadcasted_iota(jnp.int32, (ROUTE_TILE, LOCAL_ROWS), 1)
    picks = jnp.where((slot_t == loc_row_ref[:, 0:1]) | (slot_t == loc_row_ref[:, 1:2]), 1.0, 0.0).astype(BF16)
    f = jnp.dot(picks, hi, preferred_element_type=F32) + jnp.dot(picks, lo, preferred_element_type=F32)
    x2 = _layer_norm(ALPHA * x_ref[...] + mod_ref[0, 5:6, :] * f, g2_ref[...], b2_ref[...])

    @pl.when(i < NP_TOK // ROUTE_TILE)
    def _():
        yp_ref[...] = x2

    @pl.when(i >= NP_TOK // ROUTE_TILE)
    def _():
        ys_ref[...] = x2


def _combine(y_sorted, rt, x1, mods, g2, b2):
    tm = ROUTE_TILE
    n_p = NP_TOK // tm
    vec = pl.BlockSpec((1, D_MODEL), lambda i, *_: (0, 0))
    lanes = pl.BlockSpec((2, tm), lambda i, *_: (0, i))
    return pl.pallas_call(
        _combine_kernel,
        grid_spec=pltpu.PrefetchScalarGridSpec(
            num_scalar_prefetch=3, grid=(N_ROUTE_TILES,),
            in_specs=[lanes, lanes, pl.BlockSpec((tm, 2), lambda i, *_: (i, 0)),
                      pl.BlockSpec((tm, D_MODEL), lambda i, *_: (i, 0)), _mod_spec(tm), vec, vec,
                      pl.BlockSpec(memory_space=pl.ANY)],
            out_specs=[pl.BlockSpec((tm, D_MODEL), lambda i, *_: (jnp.minimum(i, n_p - 1), 0)),
                       pl.BlockSpec((tm, D_MODEL), lambda i, *_: (jnp.maximum(i - n_p, 0), 0))],
            scratch_shapes=[pltpu.VMEM((2, LOCAL_ROWS, D_MODEL), F32), pltpu.SemaphoreType.DMA((2,))]),
        out_shape=[jax.ShapeDtypeStruct((NP_TOK, D_MODEL), F32), jax.ShapeDtypeStruct((NS_TOK, D_MODEL), F32)],
        compiler_params=_params(1),
        name="moe_combine",
    )(rt["units"], rt["local_start"], rt["global_start"], rt["local_by_lane"], rt["weight_by_lane"], rt["local_by_row"],
      x1, mods, g2.reshape(1, D_MODEL), b2.reshape(1, D_MODEL), y_sorted)


def kernel(x_prompt, x_sample, cache_k, cache_v, c, c_ctx, ln_in_g, ln_in_b, ada_w, ada_b, w_in, conv_w, conv_b, w_conv_out, lam_q1, lam_k1, lam_q2, lam_k2, subln_g, w_attn_out, w_out, ln1_g, ln1_b, ln2_g, ln2_b, ffn_w_gate, ffn_w_up, ffn_w_down, moe_router, moe_w_gate, moe_w_up, moe_w_down):
    assert DEPTH == 2
    cvec = jnp.concatenate([c, c_ctx[None, :], jnp.zeros((MOD_ROWS - DEC_BATCH - 1, D_MODEL), F32)], axis=0)
    mods = _ada(cvec, ada_w, ada_b).reshape(DEPTH, MOD_ROWS, 6, D_MODEL)
    tables = _rope_tables()

    x, h = _ln_in(x_prompt.reshape(NP_TOK, D_MODEL), x_sample.reshape(NS_TOK, D_MODEL), ln_in_g, ln_in_b, mods[0])
    caches = None
    for l in range(DEPTH):
        lam_init = 0.8 - 0.6 * math.exp(-0.3 * l)
        w_l = w_in[l].astype(BF16)
        w3 = w_l[:, :3 * D_CONV]
        wqkv = w_l[:, 3 * D_CONV:3 * D_CONV + 3 * ATTN_W]
        wgate = w_l[:, 3 * D_CONV + 3 * ATTN_W:]
        lam_vecs = jnp.stack([lam_q1[l], lam_k1[l], lam_q2[l], lam_k2[l]]).astype(F32)
        g_sub = subln_g[l].reshape(1, V_DIM)

        conv_y = _conv_branch(h, w3, conv_w[l], conv_b[l])
        q, k, v, *caches = _qkv_prompt(h, wqkv, l, caches)
        o_p = _attn_prompt(lam_vecs, g_sub, q, k, v, lam_init)
        q, k, v = _qkv_sample(h, wqkv, tables)
        o_s = _attn_sample(lam_vecs, g_sub, q, k, v, cache_k, cache_v, l, lam_init)

        mix_w = (wgate, w_conv_out[l].astype(BF16), w_attn_out[l].astype(BF16), w_out[l].astype(BF16), ln1_g[l], ln1_b[l])
        i = l // 2
        if l % 2 == 0:
            x1, h2 = _merge(h, conv_y, o_p, o_s, x, mods[l], *mix_w)
            x, h = _ffn(h2, ffn_w_gate[i].astype(BF16), ffn_w_up[i].astype(BF16), ffn_w_down[i].astype(BF16),
                        x1, mods[l], ln2_g[l], ln2_b[l], mods[l + 1])
        else:
            r_t = moe_router[i].T
            r_hi = r_t.astype(BF16)
            router = jnp.concatenate([r_hi, (r_t - r_hi.astype(F32)).astype(BF16)], axis=0)
            x1, h2, route, counts = _merge(h, conv_y, o_p, o_s, x, mods[l], *mix_w, router)
            rt = _routing_tables(route, counts)
            x_sorted = _dispatch(h2, rt)
            y_sorted = _moe_ffn(x_sorted, rt, moe_w_gate[i].astype(BF16), moe_w_up[i].astype(BF16), moe_w_down[i].astype(BF16))
            y_p, y_s = _combine(y_sorted, rt, x1, mods[l], ln2_g[l], ln2_b[l])

    return (y_p.reshape(BATCH, SEQ, D_MODEL), y_s.reshape(DEC_BATCH, DEC_SEQ, D_MODEL), caches[0], caches[1])
```

```python
import functools
import math

import jax
import jax.numpy as jnp
from jax import lax
from jax.experimental import pallas as pl
from jax.experimental.pallas import tpu as pltpu

D_MODEL = 1024
BATCH = 32
SEQ = 256
DEPTH = 2
DEC_BATCH = 8
DEC_SEQ = 1024
PAST_LEN = 512
GRID_W = 64
D_CONV = 512
N_HEADS = 8
HEAD_DIM = 64
V_DIM = 2 * HEAD_DIM
ATTN_W = N_HEADS * V_DIM
AXIS_DIM = HEAD_DIM // 2
ROPE_BASE = 10000.0
D_FF = 2816
N_EXPERTS = 8
D_FF_EXPERT = 1408
ALPHA = (2 * DEPTH) ** 0.25
LN_EPS = 1e-5
QK_SCALE = HEAD_DIM ** -0.5 * math.log2(math.e)

NP_TOK = BATCH * SEQ
NS_TOK = DEC_BATCH * DEC_SEQ
N_TOK = NP_TOK + NS_TOK
MOD_ROWS = 16
CTX_ROW = DEC_BATCH
LANES = 128
VMEM_LIMIT = 56 * 1024 * 1024

F32 = jnp.float32
BF16 = jnp.bfloat16
_NT = (((1,), (1,)), ((), ()))


def _params(n_axes, vmem=VMEM_LIMIT):
    return pltpu.CompilerParams(dimension_semantics=("arbitrary",) * n_axes, vmem_limit_bytes=vmem)


def _resident(shape):
    return pl.BlockSpec(shape, lambda *_: (0,) * len(shape), pipeline_mode=pl.Buffered(1))


def _mod_row(i, tm):
    n_p = NP_TOK // tm
    return jnp.where(i < n_p, CTX_ROW, (i - n_p) // (DEC_SEQ // tm))


def _mod_spec(tm):
    return pl.BlockSpec((1, 6, D_MODEL), lambda i, *_: (_mod_row(i, tm), 0, 0))


def _layer_norm(x, g, b):
    mu = jnp.mean(x, axis=-1, keepdims=True)
    xc = x - mu
    var = jnp.mean(xc * xc, axis=-1, keepdims=True)
    return xc * lax.rsqrt(var + LN_EPS) * g + b


def _ada_kernel(c_ref, w_ref, b_ref, o_ref):
    c = c_ref[...]
    a = (c * jax.nn.sigmoid(c)).astype(BF16)
    o_ref[0] = jnp.dot(a, w_ref[0].astype(BF16), preferred_element_type=F32) + b_ref[0]


def _ada(cvec, ada_w, ada_b):
    tn = 1024
    return pl.pallas_call(
        _ada_kernel,
        grid=(DEPTH, 6 * D_MODEL // tn),
        in_specs=[
            pl.BlockSpec((MOD_ROWS, D_MODEL), lambda l, j: (0, 0)),
            pl.BlockSpec((1, D_MODEL, tn), lambda l, j: (l, 0, j)),
            pl.BlockSpec((1, 1, tn), lambda l, j: (l, 0, j)),
        ],
        out_specs=pl.BlockSpec((1, MOD_ROWS, tn), lambda l, j: (l, 0, j)),
        out_shape=jax.ShapeDtypeStruct((DEPTH, MOD_ROWS, 6 * D_MODEL), F32),
        compiler_params=_params(2),
        name="ada",
    )(cvec, ada_w, ada_b.reshape(DEPTH, 1, 6 * D_MODEL))


def _ln_in_kernel(xp_ref, xs_ref, g_ref, b_ref, mod_ref, x_ref, h_ref, *, n_p):
    i = pl.program_id(0)

    def emit(src_ref):
        y = _layer_norm(src_ref[...], g_ref[...], b_ref[...])
        x_ref[...] = y
        h_ref[...] = (y * (1 + mod_ref[0, 1:2, :]) + mod_ref[0, 0:1, :]).astype(BF16)

    @pl.when(i < n_p)
    def _():
        emit(xp_ref)

    @pl.when(i >= n_p)
    def _():
        emit(xs_ref)


def _ln_in(xp, xs, g, b, mods):
    tm = 512
    n_p = NP_TOK // tm
    tile = lambda i: (i, 0)
    return pl.pallas_call(
        functools.partial(_ln_in_kernel, n_p=n_p),
        grid=(N_TOK // tm,),
        in_specs=[
            pl.BlockSpec((tm, D_MODEL), lambda i: (jnp.minimum(i, n_p - 1), 0)),
            pl.BlockSpec((tm, D_MODEL), lambda i: (jnp.maximum(i - n_p, 0), 0)),
            pl.BlockSpec((1, D_MODEL), lambda i: (0, 0)),
            pl.BlockSpec((1, D_MODEL), lambda i: (0, 0)),
            _mod_spec(tm),
        ],
        out_specs=[pl.BlockSpec((tm, D_MODEL), tile), pl.BlockSpec((tm, D_MODEL), tile)],
        out_shape=[jax.ShapeDtypeStruct((N_TOK, D_MODEL), F32), jax.ShapeDtypeStruct((N_TOK, D_MODEL), BF16)],
        compiler_params=_params(1),
        name="ln_in",
    )(xp, xs, g.reshape(1, D_MODEL), b.reshape(1, D_MODEL), mods)


CONV_CHUNK = 256


def _conv_kernel(h_ref, w_ref, cw_ref, cb_ref, y_ref, *, tm):
    i = pl.program_id(0)
    h = h_ref[...]
    seq = jnp.where(i < NP_TOK // tm, SEQ, DEC_SEQ)
    pos = lax.broadcasted_iota(jnp.int32, (tm, 1), 0) & (seq - 1)
    proj = lambda c: tuple(jnp.dot(h, w_ref[:, part * D_CONV + c:part * D_CONV + c + CONV_CHUNK],
                                   preferred_element_type=F32) for part in range(3))
    pending = proj(0)
    for c in range(0, D_CONV, CONV_CHUNK):
        gate_b, gate_c, u = pending
        if c + CONV_CHUNK < D_CONV:
            pending = proj(c + CONV_CHUNK)
        cols = slice(c, c + CONV_CHUNK)
        pc = gate_c * u
        prev = jnp.where(pos == 0, 0.0, pltpu.roll(pc, 1, axis=0))
        nxt = jnp.where(pos == seq - 1, 0.0, pltpu.roll(pc, tm - 1, axis=0))
        conv = prev * cw_ref[0:1, cols] + pc * cw_ref[1:2, cols] + nxt * cw_ref[2:3, cols] + cb_ref[:, cols]
        y_ref[:, cols] = (gate_b * conv).astype(BF16)


def _conv_branch(h, w3, conv_w, conv_b):
    tm = DEC_SEQ
    return pl.pallas_call(
        functools.partial(_conv_kernel, tm=tm),
        grid=(N_TOK // tm,),
        in_specs=[
            pl.BlockSpec((tm, D_MODEL), lambda i: (i, 0)),
            _resident((D_MODEL, 3 * D_CONV)),
            pl.BlockSpec((3, D_CONV), lambda i: (0, 0)),
            pl.BlockSpec((1, D_CONV), lambda i: (0, 0)),
        ],
        out_specs=pl.BlockSpec((tm, D_CONV), lambda i: (i, 0)),
        out_shape=jax.ShapeDtypeStruct((N_TOK, D_CONV), BF16),
        compiler_params=_params(1),
        name="conv_branch",
    )(h, w3, conv_w, conv_b.reshape(1, D_CONV))


def _rope_tables():
    pos = jnp.arange(DEC_SEQ)
    row = (pos // GRID_W).astype(F32)
    col = (pos % GRID_W).astype(F32)
    inv_freq = ROPE_BASE ** (-jnp.arange(0, AXIS_DIM, 2, dtype=F32) / AXIS_DIM)
    ang_r = row[:, None] * inv_freq
    ang_c = col[:, None] * inv_freq
    lane = jnp.arange(V_DIM)
    sub = lane % HEAD_DIM
    ang = jnp.where((sub < AXIS_DIM)[None, :], ang_r[:, lane % (AXIS_DIM // 2)], ang_c[:, lane % (AXIS_DIM // 2)])
    first = ((lane % AXIS_DIM) < AXIS_DIM // 2)[None, :]
    cos, sin = jnp.cos(ang), jnp.sin(ang)
    return cos, jnp.where(first, -sin, 0.0), jnp.where(first, 0.0, sin)


QKV_CHUNK = 2 * V_DIM


def _project_chunks(h, w_ref, emit):
    n = w_ref.shape[-1] // QKV_CHUNK
    proj = lambda c: jnp.dot(h, w_ref[:, c * QKV_CHUNK:(c + 1) * QKV_CHUNK], preferred_element_type=F32)
    y_next = proj(0)
    for c in range(n):
        y = y_next
        if c + 1 < n:
            y_next = proj(c + 1)
        emit(c, y)


def _qkv_prompt_kernel(h_ref, w_ref, *rest, tm, layer, first):
    q_ref, k_ref, v_ref, kc_all, vc_all = rest[-5:]
    kc_ref, vc_ref = (kc_all.at[:, layer], vc_all.at[:, layer]) if first else (kc_all, vc_all)
    per_part = ATTN_W // QKV_CHUNK

    def emit(c, y):
        part, cols = c // per_part, slice((c % per_part) * QKV_CHUNK, (c % per_part + 1) * QKV_CHUNK)
        if part == 0:
            q_ref[:, cols] = (y * QK_SCALE).astype(BF16)
            return
        act_ref, cache_ref = (k_ref, kc_ref) if part == 1 else (v_ref, vc_ref)
        act_ref[:, cols] = y.astype(BF16)
        for s in range(tm // SEQ):
            for j in range(QKV_CHUNK // V_DIM):
                cache_ref[s, (c % per_part) * (QKV_CHUNK // V_DIM) + j] = y[s * SEQ:(s + 1) * SEQ, j * V_DIM:(j + 1) * V_DIM]

    _project_chunks(h_ref[...], w_ref, emit)
    if first:
        for other in range(DEPTH):
            if other != layer:
                kc_all[:, other] = jnp.zeros((tm // SEQ, N_HEADS, SEQ, V_DIM), F32)
                vc_all[:, other] = jnp.zeros((tm // SEQ, N_HEADS, SEQ, V_DIM), F32)


def _qkv_sample_kernel(h_ref, w_ref, cos_ref, sup_ref, sdn_ref, q_ref, k_ref, v_ref):
    cos, s_up, s_dn = cos_ref[...], sup_ref[...], sdn_ref[...]
    per_part = ATTN_W // QKV_CHUNK

    def rope(x):
        return x * cos + pltpu.roll(x, V_DIM - AXIS_DIM // 2, axis=1) * s_up + pltpu.roll(x, AXIS_DIM // 2, axis=1) * s_dn

    def emit(c, y):
        part, c0 = c // per_part, (c % per_part) * QKV_CHUNK
        if part == 2:
            v_ref[:, c0:c0 + QKV_CHUNK] = y.astype(BF16)
            return
        for j in range(QKV_CHUNK // V_DIM):
            r = rope(y[:, j * V_DIM:(j + 1) * V_DIM])
            if part == 0:
                q_ref[:, c0 + j * V_DIM:c0 + (j + 1) * V_DIM] = (r * QK_SCALE).astype(BF16)
            else:
                k_ref[:, c0 + j * V_DIM:c0 + (j + 1) * V_DIM] = r.astype(BF16)

    _project_chunks(h_ref[...], w_ref, emit)


def _qkv_prompt(h, wqkv, layer, caches):
    tm = 512
    tile = pl.BlockSpec((tm, ATTN_W), lambda i: (i, 0))
    first = caches is None
    if first:
        cache = pl.BlockSpec((tm // SEQ, DEPTH, N_HEADS, SEQ, V_DIM), lambda i: (i, 0, 0, 0, 0))
    else:
        cache = pl.BlockSpec((tm // SEQ, None, N_HEADS, SEQ, V_DIM), lambda i: (i, layer, 0, 0, 0))
    act = jax.ShapeDtypeStruct((NP_TOK, ATTN_W), BF16)
    ctx = jax.ShapeDtypeStruct((BATCH, DEPTH, N_HEADS, SEQ, V_DIM), F32)
    in_specs = [pl.BlockSpec((tm, D_MODEL), lambda i: (i, 0)), _resident((D_MODEL, 3 * ATTN_W))]
    args = [h, wqkv]
    aliases = {}
    if not first:
        aliases = {len(args): 3, len(args) + 1: 4}
        in_specs += [pl.BlockSpec(memory_space=pl.ANY)] * 2
        args += list(caches)
    return pl.pallas_call(
        functools.partial(_qkv_prompt_kernel, tm=tm, layer=layer, first=first),
        grid=(NP_TOK // tm,),
        in_specs=in_specs,
        out_specs=[tile, tile, tile, cache, cache],
        out_shape=[act, act, act, ctx, ctx],
        input_output_aliases=aliases,
        compiler_params=_params(1),
        name="qkv_prompt",
    )(*args)


def _qkv_sample(h, wqkv, tables):
    tm = 512
    first_tile = NP_TOK // tm
    tile = pl.BlockSpec((tm, ATTN_W), lambda i: (i, 0))
    tab = pl.BlockSpec((tm, V_DIM), lambda i: (i % (DEC_SEQ // tm), 0))
    act = jax.ShapeDtypeStruct((NS_TOK, ATTN_W), BF16)
    return pl.pallas_call(
        _qkv_sample_kernel,
        grid=(NS_TOK // tm,),
        in_specs=[pl.BlockSpec((tm, D_MODEL), lambda i: (first_tile + i, 0)), _resident((D_MODEL, 3 * ATTN_W)), tab, tab, tab],
        out_specs=[tile, tile, tile],
        out_shape=[act, act, act],
        compiler_params=_params(1),
        name="qkv_sample",
    )(h, wqkv, *tables)


def _lam(lam_ref, lam_init):
    a = jnp.sum(lam_ref[0:1, :] * lam_ref[1:2, :], axis=1, keepdims=True)
    b = jnp.sum(lam_ref[2:3, :] * lam_ref[3:4, :], axis=1, keepdims=True)
    return jnp.exp(a) - jnp.exp(b) + lam_init


def _scores(q, k):
    lo = lax.broadcasted_iota(jnp.int32, (1, V_DIM), 1) < HEAD_DIM
    zero = jnp.zeros_like(q)
    qq = jnp.concatenate([jnp.where(lo, q, zero), jnp.where(lo, zero, q)], axis=0)
    return lax.dot_general(qq, k, _NT, preferred_element_type=F32)


def _diff_probs(s, lam):
    tq = s.shape[0] // 2
    e = jnp.exp2(s - jnp.max(s, axis=-1, keepdims=True))
    l = jnp.sum(e, axis=-1, keepdims=True)
    p = e[:tq] - e[tq:] * (lam * l[:tq] / l[tq:])
    return p.astype(BF16), 1.0 / l[:tq]


def _head_out(p, inv_l1, v, g, lam_init):
    o = jnp.dot(p, v, preferred_element_type=F32) * inv_l1
    ms = jnp.mean(o * o, axis=-1, keepdims=True)
    return (o * lax.rsqrt(ms + LN_EPS) * g * (1 - lam_init)).astype(BF16)


def _diff_attn_tiles(tiles, lam, g, lam_init):
    n = len(tiles)
    scores = lambda t: _scores(tiles[t][0](), tiles[t][1]())
    s = {0: scores(0)}
    probs = {}
    for t in range(-1, n):
        if t + 2 < n:
            s[t + 2] = scores(t + 2)
        if t + 1 < n:
            if t + 1 not in s:
                s[t + 1] = scores(t + 1)
            probs[t + 1] = _diff_probs(s.pop(t + 1), lam)
        if t >= 0:
            p, inv_l1 = probs.pop(t)
            tiles[t][3](_head_out(p, inv_l1, tiles[t][2](), g, lam_init))


ATTN_SEQS_PER_STEP = 2
ATTN_HEADS_PER_STEP = 2


def _attn_prompt_kernel(lam_ref, g_ref, q_ref, k_ref, v_ref, o_ref, *, lam_init):
    def tile(s, hd):
        rows, cols = slice(s * SEQ, (s + 1) * SEQ), slice(hd * V_DIM, (hd + 1) * V_DIM)

        def store(o):
            o_ref[rows, cols] = o

        return (lambda: q_ref[rows, cols], lambda: k_ref[rows, cols], lambda: v_ref[rows, cols], store)

    tiles = [tile(s, hd) for s in range(ATTN_SEQS_PER_STEP) for hd in range(N_HEADS)]
    _diff_attn_tiles(tiles, _lam(lam_ref, lam_init), g_ref[...], lam_init)


def _attn_prompt(lam_vecs, g, q, k, v, lam_init):
    blk = pl.BlockSpec((ATTN_SEQS_PER_STEP * SEQ, ATTN_W), lambda b: (b, 0))
    return pl.pallas_call(
        functools.partial(_attn_prompt_kernel, lam_init=lam_init),
        grid=(BATCH // ATTN_SEQS_PER_STEP,),
        in_specs=[pl.BlockSpec((4, HEAD_DIM), lambda b: (0, 0)), pl.BlockSpec((1, V_DIM), lambda b: (0, 0)), blk, blk, blk],
        out_specs=blk,
        out_shape=jax.ShapeDtypeStruct((NP_TOK, ATTN_W), BF16),
        compiler_params=_params(1),
        name="attn_prompt",
    )(lam_vecs, g, q, k, v)


def _attn_sample_kernel(lam_ref, g_ref, q_ref, kn_ref, vn_ref, kc_ref, vc_ref, o_ref, k_s, v_s, *, lam_init, tq):
    for hd in range(ATTN_HEADS_PER_STEP):
        cols = slice(hd * V_DIM, (hd + 1) * V_DIM)
        k_s[hd, 0:PAST_LEN, :] = kc_ref[hd].astype(BF16)
        k_s[hd, PAST_LEN:, :] = kn_ref[:, cols]
        v_s[hd, 0:PAST_LEN, :] = vc_ref[hd].astype(BF16)
        v_s[hd, PAST_LEN:, :] = vn_ref[:, cols]

    def tile(hd, t):
        rows, cols = slice(t * tq, (t + 1) * tq), slice(hd * V_DIM, (hd + 1) * V_DIM)

        def store(o):
            o_ref[rows, cols] = o

        return (lambda: q_ref[rows, cols], lambda: k_s[hd], lambda: v_s[hd], store)

    tiles = [tile(hd, t) for hd in range(ATTN_HEADS_PER_STEP) for t in range(DEC_SEQ // tq)]
    _diff_attn_tiles(tiles, _lam(lam_ref, lam_init), g_ref[...], lam_init)


def _attn_sample(lam_vecs, g, q, k, v, cache_k, cache_v, layer, lam_init):
    tq = 128
    hps = ATTN_HEADS_PER_STEP
    new = pl.BlockSpec((DEC_SEQ, hps * V_DIM), lambda b, h: (b, h))
    past = pl.BlockSpec((None, None, hps, PAST_LEN, V_DIM), lambda b, h: (b, layer, h, 0, 0))
    kv_all = pltpu.VMEM((hps, PAST_LEN + DEC_SEQ, V_DIM), BF16)
    return pl.pallas_call(
        functools.partial(_attn_sample_kernel, lam_init=lam_init, tq=tq),
        grid=(DEC_BATCH, N_HEADS // hps),
        in_specs=[pl.BlockSpec((4, HEAD_DIM), lambda b, h: (0, 0)), pl.BlockSpec((1, V_DIM), lambda b, h: (0, 0)),
                  new, new, new, past, past],
        out_specs=pl.BlockSpec((DEC_SEQ, hps * V_DIM), lambda b, h: (b, h)),
        out_shape=jax.ShapeDtypeStruct((NS_TOK, ATTN_W), BF16),
        scratch_shapes=[kv_all, kv_all],
        compiler_params=_params(2),
        name="attn_sample",
    )(lam_vecs, g, q, k, v, cache_k, cache_v)


MERGE_ROWS = 256


def _route(h2, router_ref, cnt_ref):
    tm = h2.shape[0]
    hi = h2.astype(BF16)
    lo = (h2 - hi.astype(F32)).astype(BF16)
    a = lax.dot_general(router_ref[...], hi, _NT, preferred_element_type=F32)
    b = lax.dot_general(router_ref[0:N_EXPERTS, :], lo, _NT, preferred_element_type=F32)
    logits = a[:N_EXPERTS] + a[N_EXPERTS:] + b
    e = jnp.exp(logits - jnp.max(logits, axis=0, keepdims=True))
    p = e / jnp.sum(e, axis=0, keepdims=True)
    row = lax.broadcasted_iota(jnp.int32, p.shape, 0)
    v1 = jnp.max(p, axis=0, keepdims=True)
    i1 = jnp.min(jnp.where(p == v1, row, N_EXPERTS), axis=0, keepdims=True)
    p2 = jnp.where(row == i1, -1.0, p)
    v2 = jnp.max(p2, axis=0, keepdims=True)
    i2 = jnp.min(jnp.where(p2 == v2, row, N_EXPERTS), axis=0, keepdims=True)
    den = v1 + v2
    pick1, pick2 = row == i1, row == i2
    picked = jnp.where(pick1 | pick2, 1.0, 0.0)
    before = lax.broadcasted_iota(jnp.int32, (tm, tm), 0) < lax.broadcasted_iota(jnp.int32, (tm, tm), 1)
    ahead = jnp.dot(picked.astype(BF16), jnp.where(before, 1.0, 0.0).astype(BF16), preferred_element_type=F32)
    ahead = ahead + cnt_ref[:, 0:1]
    rank1 = jnp.sum(jnp.where(pick1, ahead, 0.0), axis=0, keepdims=True)
    rank2 = jnp.sum(jnp.where(pick2, ahead, 0.0), axis=0, keepdims=True)
    cnt_ref[...] = cnt_ref[...] + jnp.sum(picked, axis=1, keepdims=True)
    zero = jnp.zeros_like(v1)
    return jnp.concatenate([i1.astype(F32), i2.astype(F32), rank1, rank2, v1 / den, v2 / den, zero, zero], axis=0)


def _merge_kernel(h_ref, cy_ref, op_ref, os_ref, x_ref, mod_ref, wg_ref, wc_ref, wa_ref, wo_ref, g1_ref, b1_ref,
                  *rest, tm, routed):
    if routed:
        router_ref, x1_ref, h2_ref, route_ref, count_ref, cnt_ref = rest
    else:
        x1_ref, h2_ref = rest
    i = pl.program_id(0)
    if routed:
        cnt_ref[...] = jnp.zeros_like(cnt_ref)

    def mix(rows):
        g = jnp.dot(h_ref[rows, :], wg_ref[...], preferred_element_type=F32)
        y_conv = jnp.dot(cy_ref[rows, :], wc_ref[...], preferred_element_type=F32)
        o = jnp.where(i < NP_TOK // tm, op_ref[rows, :], os_ref[rows, :])
        y_attn = jnp.dot(o, wa_ref[...], preferred_element_type=F32)
        merged = jax.nn.sigmoid(g[:, :D_MODEL]) * y_conv + jax.nn.sigmoid(g[:, D_MODEL:]) * y_attn
        return jnp.dot(merged.astype(BF16), wo_ref[...], preferred_element_type=F32)

    def finish(rows, m):
        x1 = _layer_norm(ALPHA * x_ref[rows, :] + mod_ref[0, 2:3, :] * m, g1_ref[...], b1_ref[...])
        x1_ref[rows, :] = x1
        h2 = x1 * (1 + mod_ref[0, 4:5, :]) + mod_ref[0, 3:4, :]
        h2_ref[rows, :] = h2.astype(h2_ref.dtype)
        if routed:
            route_ref[:, rows] = _route(h2, router_ref, cnt_ref)

    blocks = [slice(r, r + MERGE_ROWS) for r in range(0, tm, MERGE_ROWS)]
    m_next = mix(blocks[0])
    for b, rows in enumerate(blocks):
        m = m_next
        if b + 1 < len(blocks):
            m_next = mix(blocks[b + 1])
        finish(rows, m)
    if routed:
        count_ref[...] = cnt_ref[...]
        group = jnp.ceil(cnt_ref[:, 0:1] * (1.0 / GROUP_ALIGN)) * GROUP_ALIGN
        expert = lax.broadcasted_iota(jnp.int32, (N_EXPERTS, 1), 0)
        start = jnp.zeros_like(group)
        for e in range(N_EXPERTS - 1):
            start = start + jnp.where(expert > e, group[e:e + 1, :], 0.0)
        expert_f = expert.astype(F32)
        for k in range(2):
            mine = route_ref[k:k + 1, :] == expert_f
            route_ref[6 + k:7 + k, :] = jnp.sum(jnp.where(mine, start, 0.0), axis=0, keepdims=True) + route_ref[2 + k:3 + k, :]


def _merge(h, conv_y, o_p, o_s, x, mods, wg, wc, wa, wo, g1, b1, router=None):
    tm = 512
    n_p = NP_TOK // tm
    routed = router is not None
    tile = lambda w: pl.BlockSpec((tm, w), lambda i: (i, 0))
    vec = pl.BlockSpec((1, D_MODEL), lambda i: (0, 0))
    in_specs = [
        tile(D_MODEL), tile(D_CONV),
        pl.BlockSpec((tm, ATTN_W), lambda i: (jnp.minimum(i, n_p - 1), 0)),
        pl.BlockSpec((tm, ATTN_W), lambda i: (jnp.maximum(i - n_p, 0), 0)),
        tile(D_MODEL), _mod_spec(tm),
        _resident((D_MODEL, 2 * D_MODEL)), _resident((D_CONV, D_MODEL)), _resident((ATTN_W, D_MODEL)),
        _resident((D_MODEL, D_MODEL)), vec, vec,
    ]
    args = [h, conv_y, o_p, o_s, x, mods, wg, wc, wa, wo, g1.reshape(1, D_MODEL), b1.reshape(1, D_MODEL)]
    out_specs = [tile(D_MODEL), tile(D_MODEL)]
    out_shape = [jax.ShapeDtypeStruct((N_TOK, D_MODEL), F32), jax.ShapeDtypeStruct((N_TOK, D_MODEL), BF16)]
    scratch = []
    if routed:
        in_specs.append(_resident((2 * N_EXPERTS, D_MODEL)))
        args.append(router)
        out_specs += [pl.BlockSpec((N_EXPERTS, tm), lambda i: (0, i)), pl.BlockSpec((N_EXPERTS, LANES), lambda i: (i, 0))]
        out_shape += [jax.ShapeDtypeStruct((N_EXPERTS, N_TOK), F32),
                      jax.ShapeDtypeStruct((N_TOK // tm * N_EXPERTS, LANES), F32)]
        scratch = [pltpu.VMEM((N_EXPERTS, LANES), F32)]
    return pl.pallas_call(
        functools.partial(_merge_kernel, tm=tm, routed=routed),
        grid=(N_TOK // tm,),
        in_specs=in_specs, out_specs=out_specs, out_shape=out_shape,
        scratch_shapes=scratch,
        compiler_params=_params(1),
        name="merge_routed" if routed else "merge",
    )(*args)


FF_CHUNK = 256


def _swiglu(x, wg_ref, wu_ref, wd_ref):
    d_ff = wg_ref.shape[-1]
    bounds = [(c, min(c + FF_CHUNK, d_ff)) for c in range(0, d_ff, FF_CHUNK)]

    def up(lo, hi):
        return (jnp.dot(x, wg_ref[:, lo:hi], preferred_element_type=F32),
                jnp.dot(x, wu_ref[:, lo:hi], preferred_element_type=F32))

    f = None
    pending = up(*bounds[0])
    for c, (lo, hi) in enumerate(bounds):
        a, u = pending
        if c + 1 < len(bounds):
            pending = up(*bounds[c + 1])
        hid = (a * jax.nn.sigmoid(a) * u).astype(BF16)
        d = jnp.dot(hid, wd_ref[lo:hi, :], preferred_element_type=F32)
        f = d if f is None else f + d
    return f


def _ffn_kernel(h_ref, wg_ref, wu_ref, wd_ref, x_ref, mod_ref, g2_ref, b2_ref, nmod_ref, x2_ref, hn_ref):
    def finish(rows, f):
        x2 = _layer_norm(ALPHA * x_ref[rows, :] + mod_ref[0, 5:6, :] * f, g2_ref[...], b2_ref[...])
        x2_ref[rows, :] = x2
        hn_ref[rows, :] = (x2 * (1 + nmod_ref[0, 1:2, :]) + nmod_ref[0, 0:1, :]).astype(BF16)

    blocks = [slice(r, r + MERGE_ROWS) for r in range(0, h_ref.shape[0], MERGE_ROWS)]
    f_next = _swiglu(h_ref[blocks[0], :], wg_ref, wu_ref, wd_ref)
    for b, rows in enumerate(blocks):
        f = f_next
        if b + 1 < len(blocks):
            f_next = _swiglu(h_ref[blocks[b + 1], :], wg_ref, wu_ref, wd_ref)
        finish(rows, f)


def _ffn(h2, wg, wu, wd, x1, mods, g2, b2, next_mods):
    tm = 512
    d_ff = wg.shape[-1]
    tile = lambda w: pl.BlockSpec((tm, w), lambda i: (i, 0))
    vec = pl.BlockSpec((1, D_MODEL), lambda i: (0, 0))
    return pl.pallas_call(
        _ffn_kernel,
        grid=(N_TOK // tm,),
        in_specs=[tile(D_MODEL), _resident((D_MODEL, d_ff)), _resident((D_MODEL, d_ff)), _resident((d_ff, D_MODEL)),
                  tile(D_MODEL), _mod_spec(tm), vec, vec, _mod_spec(tm)],
        out_specs=[tile(D_MODEL), tile(D_MODEL)],
        out_shape=[jax.ShapeDtypeStruct((N_TOK, D_MODEL), F32), jax.ShapeDtypeStruct((N_TOK, D_MODEL), BF16)],
        compiler_params=_params(1),
        name="ffn",
    )(h2, wg, wu, wd, x1, mods, g2.reshape(1, D_MODEL), b2.reshape(1, D_MODEL), next_mods)


N_PAIRS = 2 * N_TOK
ROUTE_TILE = 512
N_ROUTE_TILES = N_TOK // ROUTE_TILE
GROUP_ALIGN = 16
LOCAL_ROWS = 2 * ROUTE_TILE + N_EXPERTS * GROUP_ALIGN
N_SLOTS = N_PAIRS + N_ROUTE_TILES * N_EXPERTS * GROUP_ALIGN
SLOT_TILE = 512
N_SLOT_TILES = N_SLOTS // SLOT_TILE
N_ITEMS = N_SLOT_TILES + N_EXPERTS
COPY_BITS = (ROUTE_TILE // GROUP_ALIGN).bit_length()
TAIL_EXPERT = N_EXPERTS

def _routing_tables(route, counts):
    n = counts.reshape(N_ROUTE_TILES, N_EXPERTS, LANES)[:, :, 0].astype(jnp.int32)
    g = (n + GROUP_ALIGN - 1) // GROUP_ALIGN * GROUP_ALIGN
    local_start = jnp.cumsum(g, axis=1) - g
    region = jnp.concatenate([jnp.zeros((1,), jnp.int32), jnp.cumsum(jnp.sum(g, axis=0))])
    global_start = region[None, :-1] + jnp.cumsum(g, axis=0) - g
    off = jnp.concatenate([region, jnp.full((1,), N_SLOTS, jnp.int32)])
    local = route[6:8].astype(jnp.int32)
    t0 = jnp.arange(N_SLOT_TILES, dtype=jnp.int32)[:, None] * SLOT_TILE
    live = jnp.maximum(off[None, :-1], t0) < jnp.minimum(off[None, 1:], t0 + SLOT_TILE)
    n_items = jnp.sum(live).astype(jnp.int32)
    order = jnp.nonzero(live.reshape(-1), size=N_ITEMS, fill_value=0)[0].astype(jnp.int32)
    order = jnp.where(jnp.arange(N_ITEMS) < n_items, order, order[n_items - 1])
    return dict(
        units=(g // GROUP_ALIGN).reshape(-1), local_start=local_start.reshape(-1), global_start=global_start.reshape(-1),
        used=region[-1:], off=off, n_items=n_items.reshape(1),
        item_tile=order // (N_EXPERTS + 1), item_expert=order % (N_EXPERTS + 1),
        local_by_lane=local, local_by_row=local.T, weight_by_lane=route[4:6])


def _for_each_chunk(units, fn):
    for b in range(COPY_BITS):
        @pl.when(((units >> b) & 1) == 1)
        def _():
            fn(pl.multiple_of((units & ((1 << b) - 1)) * GROUP_ALIGN, GROUP_ALIGN), GROUP_ALIGN << b)


def _group_copies(tile, units_ref, local_ref, global_ref, make, act):
    for e in range(N_EXPERTS):
        g = tile * N_EXPERTS + e

        def chunk(off, rows, g=g):
            act(make(pl.multiple_of(local_ref[g] + off, GROUP_ALIGN), pl.multiple_of(global_ref[g] + off, GROUP_ALIGN), rows))

        _for_each_chunk(units_ref[g], chunk)


def _dispatch_kernel(units_ref, local_ref, global_ref, used_ref, loc_ref, h_ref, xs_hbm, xl, zeros, sem):
    i = pl.program_id(0)
    cur = i % 2
    last = pl.num_programs(0) - 1
    slot = lax.broadcasted_iota(jnp.int32, (LOCAL_ROWS, ROUTE_TILE), 0)
    one_hot = jnp.where((slot == loc_ref[0:1, :]) | (slot == loc_ref[1:2, :]), 1.0, 0.0).astype(BF16)
    xl[cur] = jnp.dot(one_hot, h_ref[...], preferred_element_type=F32).astype(BF16)

    def copies(tile, b, act):
        make = lambda l, g, rows: pltpu.make_async_copy(xl.at[b, pl.ds(l, rows), :], xs_hbm.at[pl.ds(g, rows), :], sem.at[b])
        _group_copies(tile, units_ref, local_ref, global_ref, make, act)

    copies(i, cur, lambda c: c.start())

    @pl.when(i > 0)
    def _():
        copies(i - 1, 1 - cur, lambda c: c.wait())

    @pl.when(i == last)
    def _():
        copies(i, cur, lambda c: c.wait())

    @pl.when(i == last)
    def _():
        zeros[...] = jnp.zeros_like(zeros)
        used = used_ref[0]
        tail = (N_SLOTS - used) // GROUP_ALIGN
        small, n_big = tail % (SLOT_TILE // GROUP_ALIGN), tail // (SLOT_TILE // GROUP_ALIGN)
        big0 = used + small * GROUP_ALIGN

        fill_sem = sem.at[0]

        def fill(act):
            _for_each_chunk(small, lambda off, rows: act(pltpu.make_async_copy(
                zeros.at[pl.ds(0, rows), :], xs_hbm.at[pl.ds(pl.multiple_of(used + off, GROUP_ALIGN), rows), :], fill_sem)))
            for k in range((N_SLOTS - N_PAIRS) // SLOT_TILE):
                @pl.when(k < n_big)
                def _():
                    act(pltpu.make_async_copy(
                        zeros, xs_hbm.at[pl.ds(pl.multiple_of(big0 + k * SLOT_TILE, GROUP_ALIGN), SLOT_TILE), :], fill_sem))

        fill(lambda c: c.start())
        fill(lambda c: c.wait())


def _dispatch(h2, rt):
    return pl.pallas_call(
        _dispatch_kernel,
        grid_spec=pltpu.PrefetchScalarGridSpec(
            num_scalar_prefetch=4, grid=(N_ROUTE_TILES,),
            in_specs=[pl.BlockSpec((2, ROUTE_TILE), lambda i, *_: (0, i)),
                      pl.BlockSpec((ROUTE_TILE, D_MODEL), lambda i, *_: (i, 0))],
            out_specs=pl.BlockSpec(memory_space=pl.ANY),
            scratch_shapes=[pltpu.VMEM((2, LOCAL_ROWS, D_MODEL), BF16), pltpu.VMEM((SLOT_TILE, D_MODEL), BF16),
                            pltpu.SemaphoreType.DMA((2,))]),
        out_shape=jax.ShapeDtypeStruct((N_SLOTS, D_MODEL), BF16),
        compiler_params=_params(1),
        name="moe_dispatch",
    )(rt["units"], rt["local_start"], rt["global_start"], rt["used"], rt["local_by_lane"], h2)


def _moe_kernel(tile_ref, expert_ref, off_ref, n_ref, x_ref, wg_ref, wu_ref, wd_ref, y_ref):
    j = pl.program_id(0)

    @pl.when(j < n_ref[0])
    def _():
        t, e = tile_ref[j], expert_ref[j]
        lo, hi = off_ref[e] - t * SLOT_TILE, off_ref[e + 1] - t * SLOT_TILE
        opens_tile = (j == 0) | (tile_ref[jnp.maximum(j - 1, 0)] != t)
        is_tail = e == TAIL_EXPERT

        def run(keep_other_rows, compute):
            blocks = [slice(r, r + MERGE_ROWS) for r in range(0, SLOT_TILE, MERGE_ROWS)]
            zero_rows = jnp.zeros((MERGE_ROWS, D_MODEL), F32)
            expert = lambda rows: _swiglu(x_ref[rows, :], wg_ref, wu_ref, wd_ref) if compute else zero_rows
            f_next = expert(blocks[0])
            for b, rows in enumerate(blocks):
                f = f_next
                if b + 1 < len(blocks):
                    f_next = expert(blocks[b + 1])
                row = lax.broadcasted_iota(jnp.int32, (MERGE_ROWS, 1), 0) + rows.start
                mine = (row >= lo) & (row < hi)
                y_ref[rows, :] = jnp.where(mine, f, y_ref[rows, :] if keep_other_rows else 0.0)

        for keep in (False, True):
            for tail in (False, True):
                in_case = (jnp.logical_not(opens_tile) if keep else opens_tile) & (is_tail if tail else jnp.logical_not(is_tail))

                @pl.when(in_case)
                def _(keep=keep, tail=tail):
                    run(keep, not tail)


def _moe_ffn(x_sorted, rt, wg, wu, wd):
    d_ff = wg.shape[-1]
    rows = pl.BlockSpec((SLOT_TILE, D_MODEL), lambda j, it, ie, off, n: (it[j], 0))
    expert = lambda j, it, ie, off, n: (jnp.minimum(ie[j], N_EXPERTS - 1), 0, 0)
    w_in = pl.BlockSpec((None, D_MODEL, d_ff), expert)
    w_out = pl.BlockSpec((None, d_ff, D_MODEL), expert)
    return pl.pallas_call(
        _moe_kernel,
        grid_spec=pltpu.PrefetchScalarGridSpec(
            num_scalar_prefetch=4, grid=(N_ITEMS,),
            in_specs=[rows, w_in, w_in, w_out], out_specs=rows),
        out_shape=jax.ShapeDtypeStruct((N_SLOTS, D_MODEL), F32),
        compiler_params=_params(1),
        name="moe_ffn",
    )(rt["item_tile"], rt["item_expert"], rt["off"], rt["n_items"], x_sorted, wg, wu, wd)


def _combine_kernel(units_ref, local_ref, global_ref, loc_lane_ref, w_lane_ref, loc_row_ref, x_ref, mod_ref, g2_ref, b2_ref,
                    y_hbm, yp_ref, ys_ref, yl, sem):
    i = pl.program_id(0)
    cur = i % 2

    def gather(tile, b, act):
        make = lambda l, g, rows: pltpu.make_async_copy(y_hbm.at[pl.ds(g, rows), :], yl.at[b, pl.ds(l, rows), :], sem.at[b])
        _group_copies(tile, units_ref, local_ref, global_ref, make, act)

    @pl.when(i == 0)
    def _():
        yl[...] = jnp.zeros_like(yl)
        gather(0, 0, lambda c: c.start())

    @pl.when(i + 1 < pl.num_programs(0))
    def _():
        gather(i + 1, 1 - cur, lambda c: c.start())

    gather(i, cur, lambda c: c.wait())
    slot = lax.broadcasted_iota(jnp.int32, (LOCAL_ROWS, ROUTE_TILE), 0)
    gate = jnp.sum(jnp.where(slot == loc_lane_ref[0:1, :], w_lane_ref[0:1, :], 0.0)
                   + jnp.where(slot == loc_lane_ref[1:2, :], w_lane_ref[1:2, :], 0.0), axis=1, keepdims=True)
    y = yl[cur] * gate
    hi = y.astype(BF16)
    lo = (y - hi.astype(F32)).astype(BF16)
    slot_t = lax.broadcasted_iota(jnp.int32, (ROUTE_TILE, LOCAL_ROWS), 1)
    picks = jnp.where((slot_t == loc_row_ref[:, 0:1]) | (slot_t == loc_row_ref[:, 1:2]), 1.0, 0.0).astype(BF16)
    f = jnp.dot(picks, hi, preferred_element_type=F32) + jnp.dot(picks, lo, preferred_element_type=F32)
    x2 = _layer_norm(ALPHA * x_ref[...] + mod_ref[0, 5:6, :] * f, g2_ref[...], b2_ref[...])

    @pl.when(i < NP_TOK // ROUTE_TILE)
    def _():
        yp_ref[...] = x2

    @pl.when(i >= NP_TOK // ROUTE_TILE)
    def _():
        ys_ref[...] = x2


def _combine(y_sorted, rt, x1, mods, g2, b2):
    tm = ROUTE_TILE
    n_p = NP_TOK // tm
    vec = pl.BlockSpec((1, D_MODEL), lambda i, *_: (0, 0))
    lanes = pl.BlockSpec((2, tm), lambda i, *_: (0, i))
    return pl.pallas_call(
        _combine_kernel,
        grid_spec=pltpu.PrefetchScalarGridSpec(
            num_scalar_prefetch=3, grid=(N_ROUTE_TILES,),
            in_specs=[lanes, lanes, pl.BlockSpec((tm, 2), lambda i, *_: (i, 0)),
                      pl.BlockSpec((tm, D_MODEL), lambda i, *_: (i, 0)), _mod_spec(tm), vec, vec,
                      pl.BlockSpec(memory_space=pl.ANY)],
            out_specs=[pl.BlockSpec((tm, D_MODEL), lambda i, *_: (jnp.minimum(i, n_p - 1), 0)),
                       pl.BlockSpec((tm, D_MODEL), lambda i, *_: (jnp.maximum(i - n_p, 0), 0))],
            scratch_shapes=[pltpu.VMEM((2, LOCAL_ROWS, D_MODEL), F32), pltpu.SemaphoreType.DMA((2,))]),
        out_shape=[jax.ShapeDtypeStruct((NP_TOK, D_MODEL), F32), jax.ShapeDtypeStruct((NS_TOK, D_MODEL), F32)],
        compiler_params=_params(1),
        name="moe_combine",
    )(rt["units"], rt["local_start"], rt["global_start"], rt["local_by_lane"], rt["weight_by_lane"], rt["local_by_row"],
      x1, mods, g2.reshape(1, D_MODEL), b2.reshape(1, D_MODEL), y_sorted)


def kernel(x_prompt, x_sample, cache_k, cache_v, c, c_ctx, ln_in_g, ln_in_b, ada_w, ada_b, w_in, conv_w, conv_b, w_conv_out, lam_q1, lam_k1, lam_q2, lam_k2, subln_g, w_attn_out, w_out, ln1_g, ln1_b, ln2_g, ln2_b, ffn_w_gate, ffn_w_up, ffn_w_down, moe_router, moe_w_gate, moe_w_up, moe_w_down):
    assert DEPTH == 2
    cvec = jnp.concatenate([c, c_ctx[None, :], jnp.zeros((MOD_ROWS - DEC_BATCH - 1, D_MODEL), F32)], axis=0)
    mods = _ada(cvec, ada_w, ada_b).reshape(DEPTH, MOD_ROWS, 6, D_MODEL)
    tables = _rope_tables()

    x, h = _ln_in(x_prompt.reshape(NP_TOK, D_MODEL), x_sample.reshape(NS_TOK, D_MODEL), ln_in_g, ln_in_b, mods[0])
    caches = None
    for l in range(DEPTH):
        lam_init = 0.8 - 0.6 * math.exp(-0.3 * l)
        w_l = w_in[l].astype(BF16)
        w3 = w_l[:, :3 * D_CONV]
        wqkv = w_l[:, 3 * D_CONV:3 * D_CONV + 3 * ATTN_W]
        wgate = w_l[:, 3 * D_CONV + 3 * ATTN_W:]
        lam_vecs = jnp.stack([lam_q1[l], lam_k1[l], lam_q2[l], lam_k2[l]]).astype(F32)
        g_sub = subln_g[l].reshape(1, V_DIM)

        conv_y = _conv_branch(h, w3, conv_w[l], conv_b[l])
        q, k, v, *caches = _qkv_prompt(h, wqkv, l, caches)
        o_p = _attn_prompt(lam_vecs, g_sub, q, k, v, lam_init)
        q, k, v = _qkv_sample(h, wqkv, tables)
        o_s = _attn_sample(lam_vecs, g_sub, q, k, v, cache_k, cache_v, l, lam_init)

        mix_w = (wgate, w_conv_out[l].astype(BF16), w_attn_out[l].astype(BF16), w_out[l].astype(BF16), ln1_g[l], ln1_b[l])
        i = l // 2
        if l % 2 == 0:
            x1, h2 = _merge(h, conv_y, o_p, o_s, x, mods[l], *mix_w)
            x, h = _ffn(h2, ffn_w_gate[i].astype(BF16), ffn_w_up[i].astype(BF16), ffn_w_down[i].astype(BF16),
                        x1, mods[l], ln2_g[l], ln2_b[l], mods[l + 1])
        else:
            r_t = moe_router[i].T
            r_hi = r_t.astype(BF16)
            router = jnp.concatenate([r_hi, (r_t - r_hi.astype(F32)).astype(BF16)], axis=0)
            x1, h2, route, counts = _merge(h, conv_y, o_p, o_s, x, mods[l], *mix_w, router)
            rt = _routing_tables(route, counts)
            x_sorted = _dispatch(h2, rt)
            y_sorted = _moe_ffn(x_sorted, rt, moe_w_gate[i].astype(BF16), moe_w_up[i].astype(BF16), moe_w_down[i].astype(BF16))
            y_p, y_s = _combine(y_sorted, rt, x1, mods[l], ln2_g[l], ln2_b[l])

    return (y_p.reshape(BATCH, SEQ, D_MODEL), y_s.reshape(DEC_BATCH, DEC_SEQ, D_MODEL), caches[0], caches[1])
```

```python
import functools
import math

import jax
import jax.numpy as jnp
from jax import lax
from jax.experimental import pallas as pl
from jax.experimental.pallas import tpu as pltpu

D_MODEL = 1024
BATCH = 32
SEQ = 256
DEPTH = 2
DEC_BATCH = 8
DEC_SEQ = 1024
PAST_LEN = 512
GRID_W = 64
D_CONV = 512
N_HEADS = 8
HEAD_DIM = 64
V_DIM = 2 * HEAD_DIM
ATTN_W = N_HEADS * V_DIM
AXIS_DIM = HEAD_DIM // 2
ROPE_BASE = 10000.0
D_FF = 2816
N_EXPERTS = 8
D_FF_EXPERT = 1408
ALPHA = (2 * DEPTH) ** 0.25
LN_EPS = 1e-5
QK_SCALE = HEAD_DIM ** -0.5 * math.log2(math.e)

NP_TOK = BATCH * SEQ
NS_TOK = DEC_BATCH * DEC_SEQ
N_TOK = NP_TOK + NS_TOK
MOD_ROWS = 16
CTX_ROW = DEC_BATCH
LANES = 128
VMEM_LIMIT = 56 * 1024 * 1024

F32 = jnp.float32
BF16 = jnp.bfloat16
_NT = (((1,), (1,)), ((), ()))


def _params(n_axes, vmem=VMEM_LIMIT):
    return pltpu.CompilerParams(dimension_semantics=("arbitrary",) * n_axes, vmem_limit_bytes=vmem)


def _resident(shape):
    return pl.BlockSpec(shape, lambda *_: (0,) * len(shape), pipeline_mode=pl.Buffered(1))


def _weight(shape, *index):
    lead = len(index) - len(shape)
    return pl.BlockSpec((None,) * lead + tuple(shape), lambda *_: tuple(index), pipeline_mode=pl.Buffered(1))


def _dot(a, w):
    return jnp.dot(a.astype(w.dtype), w, preferred_element_type=F32)


def _mod_row(i, tm):
    n_p = NP_TOK // tm
    return jnp.where(i < n_p, CTX_ROW, (i - n_p) // (DEC_SEQ // tm))


def _mod_spec(tm):
    return pl.BlockSpec((1, 6, D_MODEL), lambda i, *_: (_mod_row(i, tm), 0, 0))


def _layer_norm(x, g, b):
    mu = jnp.mean(x, axis=-1, keepdims=True)
    xc = x - mu
    var = jnp.mean(xc * xc, axis=-1, keepdims=True)
    return xc * lax.rsqrt(var + LN_EPS) * g + b


def _ada_kernel(c_ref, w_ref, b_ref, o_ref):
    c = c_ref[...]
    a = (c * jax.nn.sigmoid(c)).astype(BF16)
    o_ref[0] = jnp.dot(a, w_ref[0].astype(BF16), preferred_element_type=F32) + b_ref[0]


def _ada(cvec, ada_w, ada_b):
    tn = 1024
    return pl.pallas_call(
        _ada_kernel,
        grid=(DEPTH, 6 * D_MODEL // tn),
        in_specs=[
            pl.BlockSpec((MOD_ROWS, D_MODEL), lambda l, j: (0, 0)),
            pl.BlockSpec((1, D_MODEL, tn), lambda l, j: (l, 0, j)),
            pl.BlockSpec((1, 1, tn), lambda l, j: (l, 0, j)),
        ],
        out_specs=pl.BlockSpec((1, MOD_ROWS, tn), lambda l, j: (l, 0, j)),
        out_shape=jax.ShapeDtypeStruct((DEPTH, MOD_ROWS, 6 * D_MODEL), F32),
        compiler_params=_params(2),
        name="ada",
    )(cvec, ada_w, ada_b.reshape(DEPTH, 1, 6 * D_MODEL))


def _ln_in_kernel(xp_ref, xs_ref, g_ref, b_ref, mod_ref, x_ref, h_ref, *, n_p):
    i = pl.program_id(0)

    def emit(src_ref):
        y = _layer_norm(src_ref[...], g_ref[...], b_ref[...])
        x_ref[...] = y
        h_ref[...] = (y * (1 + mod_ref[0, 1:2, :]) + mod_ref[0, 0:1, :]).astype(BF16)

    @pl.when(i < n_p)
    def _():
        emit(xp_ref)

    @pl.when(i >= n_p)
    def _():
        emit(xs_ref)


def _ln_in(xp, xs, g, b, mods):
    tm = 512
    n_p = NP_TOK // tm
    tile = lambda i: (i, 0)
    return pl.pallas_call(
        functools.partial(_ln_in_kernel, n_p=n_p),
        grid=(N_TOK // tm,),
        in_specs=[
            pl.BlockSpec((tm, D_MODEL), lambda i: (jnp.minimum(i, n_p - 1), 0)),
            pl.BlockSpec((tm, D_MODEL), lambda i: (jnp.maximum(i - n_p, 0), 0)),
            pl.BlockSpec((1, D_MODEL), lambda i: (0, 0)),
            pl.BlockSpec((1, D_MODEL), lambda i: (0, 0)),
            _mod_spec(tm),
        ],
        out_specs=[pl.BlockSpec((tm, D_MODEL), tile), pl.BlockSpec((tm, D_MODEL), tile)],
        out_shape=[jax.ShapeDtypeStruct((N_TOK, D_MODEL), F32), jax.ShapeDtypeStruct((N_TOK, D_MODEL), BF16)],
        compiler_params=_params(1),
        name="ln_in",
    )(xp, xs, g.reshape(1, D_MODEL), b.reshape(1, D_MODEL), mods)


CONV_CHUNK = 256


def _conv_kernel(h_ref, w_ref, cw_ref, cb_ref, y_ref, *, tm):
    i = pl.program_id(0)
    h = h_ref[...].astype(F32)
    seq = jnp.where(i < NP_TOK // tm, SEQ, DEC_SEQ)
    pos = lax.broadcasted_iota(jnp.int32, (tm, 1), 0) & (seq - 1)
    proj = lambda c: tuple(_dot(h, w_ref[:, part * D_CONV + c:part * D_CONV + c + CONV_CHUNK]) for part in range(3))
    pending = proj(0)
    for c in range(0, D_CONV, CONV_CHUNK):
        gate_b, gate_c, u = pending
        if c + CONV_CHUNK < D_CONV:
            pending = proj(c + CONV_CHUNK)
        cols = slice(c, c + CONV_CHUNK)
        pc = gate_c * u
        prev = jnp.where(pos == 0, 0.0, pltpu.roll(pc, 1, axis=0))
        nxt = jnp.where(pos == seq - 1, 0.0, pltpu.roll(pc, tm - 1, axis=0))
        conv = prev * cw_ref[0:1, cols] + pc * cw_ref[1:2, cols] + nxt * cw_ref[2:3, cols] + cb_ref[:, cols]
        y_ref[:, cols] = (gate_b * conv).astype(BF16)


def _conv_branch(h, w_in, conv_w, conv_b, layer):
    tm = DEC_SEQ
    return pl.pallas_call(
        functools.partial(_conv_kernel, tm=tm),
        grid=(N_TOK // tm,),
        in_specs=[
            pl.BlockSpec((tm, D_MODEL), lambda i: (i, 0)),
            _weight((D_MODEL, 3 * D_CONV), layer, 0, 0),
            pl.BlockSpec((None, 3, D_CONV), lambda i: (layer, 0, 0)),
            pl.BlockSpec((None, 1, D_CONV), lambda i: (layer, 0, 0)),
        ],
        out_specs=pl.BlockSpec((tm, D_CONV), lambda i: (i, 0)),
        out_shape=jax.ShapeDtypeStruct((N_TOK, D_CONV), BF16),
        compiler_params=_params(1),
        name="conv_branch",
    )(h, w_in, conv_w, conv_b.reshape(DEPTH, 1, D_CONV))


def _rope_tables():
    pos = jnp.arange(DEC_SEQ)
    row = (pos // GRID_W).astype(F32)
    col = (pos % GRID_W).astype(F32)
    inv_freq = ROPE_BASE ** (-jnp.arange(0, AXIS_DIM, 2, dtype=F32) / AXIS_DIM)
    ang_r = row[:, None] * inv_freq
    ang_c = col[:, None] * inv_freq
    lane = jnp.arange(V_DIM)
    sub = lane % HEAD_DIM
    ang = jnp.where((sub < AXIS_DIM)[None, :], ang_r[:, lane % (AXIS_DIM // 2)], ang_c[:, lane % (AXIS_DIM // 2)])
    first = ((lane % AXIS_DIM) < AXIS_DIM // 2)[None, :]
    cos, sin = jnp.cos(ang), jnp.sin(ang)
    return cos, jnp.where(first, -sin, 0.0), jnp.where(first, 0.0, sin)


QKV_CHUNK = 2 * V_DIM


def _project_chunks(h, w_refs, emit):
    per_ref = w_refs[0].shape[-1] // QKV_CHUNK
    n = per_ref * len(w_refs)
    h = h.astype(w_refs[0].dtype)
    proj = lambda c: _dot(h, w_refs[c // per_ref][:, (c % per_ref) * QKV_CHUNK:(c % per_ref + 1) * QKV_CHUNK])
    y_next = proj(0)
    for c in range(n):
        y = y_next
        if c + 1 < n:
            y_next = proj(c + 1)
        emit(c, y)


def _qkv_prompt_kernel(h_ref, wa_ref, wb_ref, *rest, tm, layer, first):
    q_ref, k_ref, v_ref, kc_all, vc_all = rest[-5:]
    kc_ref, vc_ref = (kc_all.at[:, layer], vc_all.at[:, layer]) if first else (kc_all, vc_all)
    per_part = ATTN_W // QKV_CHUNK

    def emit(c, y):
        part, cols = c // per_part, slice((c % per_part) * QKV_CHUNK, (c % per_part + 1) * QKV_CHUNK)
        if part == 0:
            q_ref[:, cols] = (y * QK_SCALE).astype(BF16)
            return
        act_ref, cache_ref = (k_ref, kc_ref) if part == 1 else (v_ref, vc_ref)
        act_ref[:, cols] = y.astype(BF16)
        for s in range(tm // SEQ):
            for j in range(QKV_CHUNK // V_DIM):
                cache_ref[s, (c % per_part) * (QKV_CHUNK // V_DIM) + j] = y[s * SEQ:(s + 1) * SEQ, j * V_DIM:(j + 1) * V_DIM]

    _project_chunks(h_ref[...], (wa_ref, wb_ref), emit)
    if first:
        for other in range(DEPTH):
            if other != layer:
                kc_all[:, other] = jnp.zeros((tm // SEQ, N_HEADS, SEQ, V_DIM), F32)
                vc_all[:, other] = jnp.zeros((tm // SEQ, N_HEADS, SEQ, V_DIM), F32)


def _qkv_sample_kernel(h_ref, wa_ref, wb_ref, cos_ref, sup_ref, sdn_ref, q_ref, k_ref, v_ref):
    cos, s_up, s_dn = cos_ref[...], sup_ref[...], sdn_ref[...]
    per_part = ATTN_W // QKV_CHUNK

    def rope(x):
        return x * cos + pltpu.roll(x, V_DIM - AXIS_DIM // 2, axis=1) * s_up + pltpu.roll(x, AXIS_DIM // 2, axis=1) * s_dn

    def emit(c, y):
        part, c0 = c // per_part, (c % per_part) * QKV_CHUNK
        if part == 2:
            v_ref[:, c0:c0 + QKV_CHUNK] = y.astype(BF16)
            return
        for j in range(QKV_CHUNK // V_DIM):
            r = rope(y[:, j * V_DIM:(j + 1) * V_DIM])
            if part == 0:
                q_ref[:, c0 + j * V_DIM:c0 + (j + 1) * V_DIM] = (r * QK_SCALE).astype(BF16)
            else:
                k_ref[:, c0 + j * V_DIM:c0 + (j + 1) * V_DIM] = r.astype(BF16)

    _project_chunks(h_ref[...], (wa_ref, wb_ref), emit)


def _qkv_weights(layer):
    assert 2 * 3 * D_CONV == 3 * ATTN_W
    return [_weight((D_MODEL, 3 * D_CONV), layer, 0, 1), _weight((D_MODEL, 3 * D_CONV), layer, 0, 2)]


def _qkv_prompt(h, w_in, layer, caches):
    tm = 512
    tile = pl.BlockSpec((tm, ATTN_W), lambda i: (i, 0))
    first = caches is None
    if first:
        cache = pl.BlockSpec((tm // SEQ, DEPTH, N_HEADS, SEQ, V_DIM), lambda i: (i, 0, 0, 0, 0))
    else:
        cache = pl.BlockSpec((tm // SEQ, None, N_HEADS, SEQ, V_DIM), lambda i: (i, layer, 0, 0, 0))
    act = jax.ShapeDtypeStruct((NP_TOK, ATTN_W), BF16)
    ctx = jax.ShapeDtypeStruct((BATCH, DEPTH, N_HEADS, SEQ, V_DIM), F32)
    in_specs = [pl.BlockSpec((tm, D_MODEL), lambda i: (i, 0)), *_qkv_weights(layer)]
    args = [h, w_in, w_in]
    aliases = {}
    if not first:
        aliases = {len(args): 3, len(args) + 1: 4}
        in_specs += [pl.BlockSpec(memory_space=pl.ANY)] * 2
        args += list(caches)
    return pl.pallas_call(
        functools.partial(_qkv_prompt_kernel, tm=tm, layer=layer, first=first),
        grid=(NP_TOK // tm,),
        in_specs=in_specs,
        out_specs=[tile, tile, tile, cache, cache],
        out_shape=[act, act, act, ctx, ctx],
        input_output_aliases=aliases,
        compiler_params=_params(1),
        name="qkv_prompt",
    )(*args)


def _qkv_sample(h, w_in, tables, layer):
    tm = 512
    first_tile = NP_TOK // tm
    tile = pl.BlockSpec((tm, ATTN_W), lambda i: (i, 0))
    tab = pl.BlockSpec((tm, V_DIM), lambda i: (i % (DEC_SEQ // tm), 0))
    act = jax.ShapeDtypeStruct((NS_TOK, ATTN_W), BF16)
    return pl.pallas_call(
        _qkv_sample_kernel,
        grid=(NS_TOK // tm,),
        in_specs=[pl.BlockSpec((tm, D_MODEL), lambda i: (first_tile + i, 0)), *_qkv_weights(layer), tab, tab, tab],
        out_specs=[tile, tile, tile],
        out_shape=[act, act, act],
        compiler_params=_params(1),
        name="qkv_sample",
    )(h, w_in, w_in, *tables)


def _lam(lam_ref, lam_init):
    a = jnp.sum(lam_ref[0:1, :] * lam_ref[1:2, :], axis=1, keepdims=True)
    b = jnp.sum(lam_ref[2:3, :] * lam_ref[3:4, :], axis=1, keepdims=True)
    return jnp.exp(a) - jnp.exp(b) + lam_init


def _scores(q, k):
    lo = lax.broadcasted_iota(jnp.int32, (1, V_DIM), 1) < HEAD_DIM
    zero = jnp.zeros_like(q)
    qq = jnp.concatenate([jnp.where(lo, q, zero), jnp.where(lo, zero, q)], axis=0)
    return lax.dot_general(qq, k, _NT, preferred_element_type=F32)


def _diff_probs(s, lam):
    tq = s.shape[0] // 2
    e = jnp.exp2(s - jnp.max(s, axis=-1, keepdims=True))
    l = jnp.sum(e, axis=-1, keepdims=True)
    p = e[:tq] - e[tq:] * (lam * l[:tq] / l[tq:])
    return p.astype(BF16), 1.0 / l[:tq]


def _head_out(p, inv_l1, v, g, lam_init):
    o = jnp.dot(p, v, preferred_element_type=F32) * inv_l1
    ms = jnp.mean(o * o, axis=-1, keepdims=True)
    return (o * lax.rsqrt(ms + LN_EPS) * g * (1 - lam_init)).astype(BF16)


def _diff_attn_tiles(tiles, lam, g, lam_init):
    n = len(tiles)
    scores = lambda t: _scores(tiles[t][0](), tiles[t][1]())
    s = {0: scores(0)}
    probs = {}
    for t in range(-1, n):
        if t + 2 < n:
            s[t + 2] = scores(t + 2)
        if t + 1 < n:
            if t + 1 not in s:
                s[t + 1] = scores(t + 1)
            probs[t + 1] = _diff_probs(s.pop(t + 1), lam)
        if t >= 0:
            p, inv_l1 = probs.pop(t)
            tiles[t][3](_head_out(p, inv_l1, tiles[t][2](), g, lam_init))


ATTN_SEQS_PER_STEP = 2
ATTN_HEADS_PER_STEP = 2


def _attn_prompt_kernel(lam_ref, g_ref, q_ref, k_ref, v_ref, o_ref, *, lam_init):
    def tile(s, hd):
        rows, cols = slice(s * SEQ, (s + 1) * SEQ), slice(hd * V_DIM, (hd + 1) * V_DIM)

        def store(o):
            o_ref[rows, cols] = o

        return (lambda: q_ref[rows, cols], lambda: k_ref[rows, cols], lambda: v_ref[rows, cols], store)

    tiles = [tile(s, hd) for s in range(ATTN_SEQS_PER_STEP) for hd in range(N_HEADS)]
    _diff_attn_tiles(tiles, _lam(lam_ref, lam_init), g_ref[...], lam_init)


def _attn_prompt(lam_vecs, g, q, k, v, lam_init):
    blk = pl.BlockSpec((ATTN_SEQS_PER_STEP * SEQ, ATTN_W), lambda b: (b, 0))
    return pl.pallas_call(
        functools.partial(_attn_prompt_kernel, lam_init=lam_init),
        grid=(BATCH // ATTN_SEQS_PER_STEP,),
        in_specs=[pl.BlockSpec((4, HEAD_DIM), lambda b: (0, 0)), pl.BlockSpec((1, V_DIM), lambda b: (0, 0)), blk, blk, blk],
        out_specs=blk,
        out_shape=jax.ShapeDtypeStruct((NP_TOK, ATTN_W), BF16),
        compiler_params=_params(1),
        name="attn_prompt",
    )(lam_vecs, g, q, k, v)


def _attn_sample_kernel(lam_ref, g_ref, q_ref, kn_ref, vn_ref, kc_ref, vc_ref, o_ref, k_s, v_s, *, lam_init, tq):
    for hd in range(ATTN_HEADS_PER_STEP):
        cols = slice(hd * V_DIM, (hd + 1) * V_DIM)
        k_s[hd, 0:PAST_LEN, :] = kc_ref[hd].astype(BF16)
        k_s[hd, PAST_LEN:, :] = kn_ref[:, cols]
        v_s[hd, 0:PAST_LEN, :] = vc_ref[hd].astype(BF16)
        v_s[hd, PAST_LEN:, :] = vn_ref[:, cols]

    def tile(hd, t):
        rows, cols = slice(t * tq, (t + 1) * tq), slice(hd * V_DIM, (hd + 1) * V_DIM)

        def store(o):
            o_ref[rows, cols] = o

        return (lambda: q_ref[rows, cols], lambda: k_s[hd], lambda: v_s[hd], store)

    tiles = [tile(hd, t) for hd in range(ATTN_HEADS_PER_STEP) for t in range(DEC_SEQ // tq)]
    _diff_attn_tiles(tiles, _lam(lam_ref, lam_init), g_ref[...], lam_init)


def _attn_sample(lam_vecs, g, q, k, v, cache_k, cache_v, layer, lam_init):
    tq = 128
    hps = ATTN_HEADS_PER_STEP
    new = pl.BlockSpec((DEC_SEQ, hps * V_DIM), lambda b, h: (b, h))
    past = pl.BlockSpec((None, None, hps, PAST_LEN, V_DIM), lambda b, h: (b, layer, h, 0, 0))
    kv_all = pltpu.VMEM((hps, PAST_LEN + DEC_SEQ, V_DIM), BF16)
    return pl.pallas_call(
        functools.partial(_attn_sample_kernel, lam_init=lam_init, tq=tq),
        grid=(DEC_BATCH, N_HEADS // hps),
        in_specs=[pl.BlockSpec((4, HEAD_DIM), lambda b, h: (0, 0)), pl.BlockSpec((1, V_DIM), lambda b, h: (0, 0)),
                  new, new, new, past, past],
        out_specs=pl.BlockSpec((DEC_SEQ, hps * V_DIM), lambda b, h: (b, h)),
        out_shape=jax.ShapeDtypeStruct((NS_TOK, ATTN_W), BF16),
        scratch_shapes=[kv_all, kv_all],
        compiler_params=_params(2),
        name="attn_sample",
    )(lam_vecs, g, q, k, v, cache_k, cache_v)


MERGE_ROWS = 256


def _route(h2, router_ref, cnt_ref):
    tm = h2.shape[0]
    hi = h2.astype(BF16)
    lo = (h2 - hi.astype(F32)).astype(BF16)
    a = lax.dot_general(router_ref[...], hi, _NT, preferred_element_type=F32)
    b = lax.dot_general(router_ref[0:N_EXPERTS, :], lo, _NT, preferred_element_type=F32)
    logits = a[:N_EXPERTS] + a[N_EXPERTS:] + b
    e = jnp.exp(logits - jnp.max(logits, axis=0, keepdims=True))
    p = e / jnp.sum(e, axis=0, keepdims=True)
    row = lax.broadcasted_iota(jnp.int32, p.shape, 0)
    v1 = jnp.max(p, axis=0, keepdims=True)
    i1 = jnp.min(jnp.where(p == v1, row, N_EXPERTS), axis=0, keepdims=True)
    p2 = jnp.where(row == i1, -1.0, p)
    v2 = jnp.max(p2, axis=0, keepdims=True)
    i2 = jnp.min(jnp.where(p2 == v2, row, N_EXPERTS), axis=0, keepdims=True)
    den = v1 + v2
    pick1, pick2 = row == i1, row == i2
    picked = jnp.where(pick1 | pick2, 1.0, 0.0)
    before = lax.broadcasted_iota(jnp.int32, (tm, tm), 0) < lax.broadcasted_iota(jnp.int32, (tm, tm), 1)
    ahead = jnp.dot(picked.astype(BF16), jnp.where(before, 1.0, 0.0).astype(BF16), preferred_element_type=F32)
    ahead = ahead + cnt_ref[:, 0:1]
    rank1 = jnp.sum(jnp.where(pick1, ahead, 0.0), axis=0, keepdims=True)
    rank2 = jnp.sum(jnp.where(pick2, ahead, 0.0), axis=0, keepdims=True)
    cnt_ref[...] = cnt_ref[...] + jnp.sum(picked, axis=1, keepdims=True)
    zero = jnp.zeros_like(v1)
    return jnp.concatenate([i1.astype(F32), i2.astype(F32), rank1, rank2, v1 / den, v2 / den, zero, zero], axis=0)


def _merge_kernel(h_ref, cy_ref, op_ref, os_ref, x_ref, mod_ref, wga_ref, wgb_ref, wc_ref, wa_ref, wo_ref, g1_ref, b1_ref,
                  *rest, tm, routed):
    if routed:
        router_ref, x1_ref, h2_ref, route_ref, count_ref, cnt_ref = rest
    else:
        x1_ref, h2_ref = rest
    i = pl.program_id(0)
    if routed:
        cnt_ref[...] = jnp.zeros_like(cnt_ref)

    def mix(rows):
        h = h_ref[rows, :]
        g = jnp.concatenate([_dot(h, wga_ref[...]), _dot(h, wgb_ref[...])], axis=1)
        y_conv = _dot(cy_ref[rows, :], wc_ref[...])
        o = jnp.where(i < NP_TOK // tm, op_ref[rows, :], os_ref[rows, :])
        y_attn = _dot(o, wa_ref[...])
        merged = jax.nn.sigmoid(g[:, :D_MODEL]) * y_conv + jax.nn.sigmoid(g[:, D_MODEL:]) * y_attn
        return _dot(merged.astype(BF16), wo_ref[...])

    def finish(rows, m):
        x1 = _layer_norm(ALPHA * x_ref[rows, :] + mod_ref[0, 2:3, :] * m, g1_ref[...], b1_ref[...])
        x1_ref[rows, :] = x1
        h2 = x1 * (1 + mod_ref[0, 4:5, :]) + mod_ref[0, 3:4, :]
        h2_ref[rows, :] = h2.astype(h2_ref.dtype)
        if routed:
            route_ref[:, rows] = _route(h2, router_ref, cnt_ref)

    blocks = [slice(r, r + MERGE_ROWS) for r in range(0, tm, MERGE_ROWS)]
    m_next = mix(blocks[0])
    for b, rows in enumerate(blocks):
        m = m_next
        if b + 1 < len(blocks):
            m_next = mix(blocks[b + 1])
        finish(rows, m)
    if routed:
        count_ref[...] = cnt_ref[...]
        group = jnp.ceil(cnt_ref[:, 0:1] * (1.0 / GROUP_ALIGN)) * GROUP_ALIGN
        expert = lax.broadcasted_iota(jnp.int32, (N_EXPERTS, 1), 0)
        start = jnp.zeros_like(group)
        for e in range(N_EXPERTS - 1):
            start = start + jnp.where(expert > e, group[e:e + 1, :], 0.0)
        expert_f = expert.astype(F32)
        for k in range(2):
            mine = route_ref[k:k + 1, :] == expert_f
            route_ref[6 + k:7 + k, :] = jnp.sum(jnp.where(mine, start, 0.0), axis=0, keepdims=True) + route_ref[2 + k:3 + k, :]


def _merge(h, conv_y, o_p, o_s, x, mods, w_in, w_conv_out, w_attn_out, w_out, g1, b1, layer, router=None):
    tm = 512
    n_p = NP_TOK // tm
    routed = router is not None
    tile = lambda w: pl.BlockSpec((tm, w), lambda i: (i, 0))
    vec = pl.BlockSpec((None, 1, D_MODEL), lambda i: (layer, 0, 0))
    gate0 = 3 * D_CONV + 3 * ATTN_W
    rest = 2 * D_MODEL - 3 * D_CONV
    assert gate0 % (3 * D_CONV) == 0 and (gate0 + 3 * D_CONV) % rest == 0
    in_specs = [
        tile(D_MODEL), tile(D_CONV),
        pl.BlockSpec((tm, ATTN_W), lambda i: (jnp.minimum(i, n_p - 1), 0)),
        pl.BlockSpec((tm, ATTN_W), lambda i: (jnp.maximum(i - n_p, 0), 0)),
        tile(D_MODEL), _mod_spec(tm),
        _weight((D_MODEL, 3 * D_CONV), layer, 0, gate0 // (3 * D_CONV)),
        _weight((D_MODEL, rest), layer, 0, (gate0 + 3 * D_CONV) // rest),
        _weight((D_CONV, D_MODEL), layer, 0, 0), _weight((ATTN_W, D_MODEL), layer, 0, 0),
        _weight((D_MODEL, D_MODEL), layer, 0, 0), vec, vec,
    ]
    args = [h, conv_y, o_p, o_s, x, mods, w_in, w_in, w_conv_out, w_attn_out, w_out,
            g1.reshape(DEPTH, 1, D_MODEL), b1.reshape(DEPTH, 1, D_MODEL)]
    out_specs = [tile(D_MODEL), tile(D_MODEL)]
    out_shape = [jax.ShapeDtypeStruct((N_TOK, D_MODEL), F32), jax.ShapeDtypeStruct((N_TOK, D_MODEL), BF16)]
    scratch = []
    if routed:
        in_specs.append(_resident((2 * N_EXPERTS, D_MODEL)))
        args.append(router)
        out_specs += [pl.BlockSpec((N_EXPERTS, tm), lambda i: (0, i)), pl.BlockSpec((N_EXPERTS, LANES), lambda i: (i, 0))]
        out_shape += [jax.ShapeDtypeStruct((N_EXPERTS, N_TOK), F32),
                      jax.ShapeDtypeStruct((N_TOK // tm * N_EXPERTS, LANES), F32)]
        scratch = [pltpu.VMEM((N_EXPERTS, LANES), F32)]
    return pl.pallas_call(
        functools.partial(_merge_kernel, tm=tm, routed=routed),
        grid=(N_TOK // tm,),
        in_specs=in_specs, out_specs=out_specs, out_shape=out_shape,
        scratch_shapes=scratch,
        compiler_params=_params(1),
        name="merge_routed" if routed else "merge",
    )(*args)


FF_CHUNK = 256


def _swiglu(x, wg_ref, wu_ref, wd_ref):
    d_ff = wg_ref.shape[-1]
    bounds = [(c, min(c + FF_CHUNK, d_ff)) for c in range(0, d_ff, FF_CHUNK)]
    x = x.astype(wg_ref.dtype)

    def up(lo, hi):
        return (jnp.dot(x, wg_ref[:, lo:hi], preferred_element_type=F32),
                jnp.dot(x, wu_ref[:, lo:hi], preferred_element_type=F32))

    f = None
    pending = up(*bounds[0])
    for c, (lo, hi) in enumerate(bounds):
        a, u = pending
        if c + 1 < len(bounds):
            pending = up(*bounds[c + 1])
        hid = (a * jax.nn.sigmoid(a) * u).astype(BF16).astype(wd_ref.dtype)
        d = jnp.dot(hid, wd_ref[lo:hi, :], preferred_element_type=F32)
        f = d if f is None else f + d
    return f


def _ffn_kernel(h_ref, wg_ref, wu_ref, wd_ref, x_ref, mod_ref, g2_ref, b2_ref, nmod_ref, x2_ref, hn_ref):
    def finish(rows, f):
        x2 = _layer_norm(ALPHA * x_ref[rows, :] + mod_ref[0, 5:6, :] * f, g2_ref[...], b2_ref[...])
        x2_ref[rows, :] = x2
        hn_ref[rows, :] = (x2 * (1 + nmod_ref[0, 1:2, :]) + nmod_ref[0, 0:1, :]).astype(BF16)

    blocks = [slice(r, r + MERGE_ROWS) for r in range(0, h_ref.shape[0], MERGE_ROWS)]
    f_next = _swiglu(h_ref[blocks[0], :], wg_ref, wu_ref, wd_ref)
    for b, rows in enumerate(blocks):
        f = f_next
        if b + 1 < len(blocks):
            f_next = _swiglu(h_ref[blocks[b + 1], :], wg_ref, wu_ref, wd_ref)
        finish(rows, f)


def _ffn(h2, wg, wu, wd, x1, mods, g2, b2, next_mods, index):
    tm = 512
    d_ff = wg.shape[-1]
    tile = lambda w: pl.BlockSpec((tm, w), lambda i: (i, 0))
    vec = pl.BlockSpec((1, D_MODEL), lambda i: (0, 0))
    return pl.pallas_call(
        _ffn_kernel,
        grid=(N_TOK // tm,),
        in_specs=[tile(D_MODEL), _weight((D_MODEL, d_ff), index, 0, 0), _weight((D_MODEL, d_ff), index, 0, 0),
                  _weight((d_ff, D_MODEL), index, 0, 0), tile(D_MODEL), _mod_spec(tm), vec, vec, _mod_spec(tm)],
        out_specs=[tile(D_MODEL), tile(D_MODEL)],
        out_shape=[jax.ShapeDtypeStruct((N_TOK, D_MODEL), F32), jax.ShapeDtypeStruct((N_TOK, D_MODEL), BF16)],
        compiler_params=_params(1),
        name="ffn",
    )(h2, wg, wu, wd, x1, mods, g2.reshape(1, D_MODEL), b2.reshape(1, D_MODEL), next_mods)


N_PAIRS = 2 * N_TOK
ROUTE_TILE = 512
N_ROUTE_TILES = N_TOK // ROUTE_TILE
GROUP_ALIGN = 16
LOCAL_ROWS = 2 * ROUTE_TILE + N_EXPERTS * GROUP_ALIGN
N_SLOTS = N_PAIRS + N_ROUTE_TILES * N_EXPERTS * GROUP_ALIGN
SLOT_TILE = 512
N_SLOT_TILES = N_SLOTS // SLOT_TILE
N_ITEMS = N_SLOT_TILES + N_EXPERTS
COPY_BITS = (ROUTE_TILE // GROUP_ALIGN).bit_length()
TAIL_EXPERT = N_EXPERTS

def _routing_tables(route, counts):
    n = counts.reshape(N_ROUTE_TILES, N_EXPERTS, LANES)[:, :, 0].astype(jnp.int32)
    g = (n + GROUP_ALIGN - 1) // GROUP_ALIGN * GROUP_ALIGN
    local_start = jnp.cumsum(g, axis=1) - g
    region = jnp.concatenate([jnp.zeros((1,), jnp.int32), jnp.cumsum(jnp.sum(g, axis=0))])
    global_start = region[None, :-1] + jnp.cumsum(g, axis=0) - g
    off = jnp.concatenate([region, jnp.full((1,), N_SLOTS, jnp.int32)])
    local = route[6:8].astype(jnp.int32)
    t0 = jnp.arange(N_SLOT_TILES, dtype=jnp.int32)[:, None] * SLOT_TILE
    live = jnp.maximum(off[None, :-1], t0) < jnp.minimum(off[None, 1:], t0 + SLOT_TILE)
    n_items = jnp.sum(live).astype(jnp.int32)
    order = jnp.nonzero(live.reshape(-1), size=N_ITEMS, fill_value=0)[0].astype(jnp.int32)
    order = jnp.where(jnp.arange(N_ITEMS) < n_items, order, order[n_items - 1])
    return dict(
        units=(g // GROUP_ALIGN).reshape(-1), local_start=local_start.reshape(-1), global_start=global_start.reshape(-1),
        used=region[-1:], off=off, n_items=n_items.reshape(1),
        item_tile=order // (N_EXPERTS + 1), item_expert=order % (N_EXPERTS + 1),
        local_by_lane=local, local_by_row=local.T, weight_by_lane=route[4:6])


def _for_each_chunk(units, fn):
    for b in range(COPY_BITS):
        @pl.when(((units >> b) & 1) == 1)
        def _():
            fn(pl.multiple_of((units & ((1 << b) - 1)) * GROUP_ALIGN, GROUP_ALIGN), GROUP_ALIGN << b)


def _group_copies(tile, units_ref, local_ref, global_ref, make, act):
    for e in range(N_EXPERTS):
        g = tile * N_EXPERTS + e

        def chunk(off, rows, g=g):
            act(make(pl.multiple_of(local_ref[g] + off, GROUP_ALIGN), pl.multiple_of(global_ref[g] + off, GROUP_ALIGN), rows))

        _for_each_chunk(units_ref[g], chunk)


def _dispatch_kernel(units_ref, local_ref, global_ref, used_ref, loc_ref, h_ref, xs_hbm, xl, zeros, sem):
    i = pl.program_id(0)
    cur = i % 2
    last = pl.num_programs(0) - 1
    slot = lax.broadcasted_iota(jnp.int32, (LOCAL_ROWS, ROUTE_TILE), 0)
    one_hot = jnp.where((slot == loc_ref[0:1, :]) | (slot == loc_ref[1:2, :]), 1.0, 0.0).astype(BF16)
    xl[cur] = jnp.dot(one_hot, h_ref[...], preferred_element_type=F32).astype(BF16)

    def copies(tile, b, act):
        make = lambda l, g, rows: pltpu.make_async_copy(xl.at[b, pl.ds(l, rows), :], xs_hbm.at[pl.ds(g, rows), :], sem.at[b])
        _group_copies(tile, units_ref, local_ref, global_ref, make, act)

    copies(i, cur, lambda c: c.start())

    @pl.when(i > 0)
    def _():
        copies(i - 1, 1 - cur, lambda c: c.wait())

    @pl.when(i == last)
    def _():
        copies(i, cur, lambda c: c.wait())

    @pl.when(i == last)
    def _():
        zeros[...] = jnp.zeros_like(zeros)
        used = used_ref[0]
        tail = (N_SLOTS - used) // GROUP_ALIGN
        small, n_big = tail % (SLOT_TILE // GROUP_ALIGN), tail // (SLOT_TILE // GROUP_ALIGN)
        big0 = used + small * GROUP_ALIGN

        fill_sem = sem.at[0]

        def fill(act):
            _for_each_chunk(small, lambda off, rows: act(pltpu.make_async_copy(
                zeros.at[pl.ds(0, rows), :], xs_hbm.at[pl.ds(pl.multiple_of(used + off, GROUP_ALIGN), rows), :], fill_sem)))
            for k in range((N_SLOTS - N_PAIRS) // SLOT_TILE):
                @pl.when(k < n_big)
                def _():
                    act(pltpu.make_async_copy(
                        zeros, xs_hbm.at[pl.ds(pl.multiple_of(big0 + k * SLOT_TILE, GROUP_ALIGN), SLOT_TILE), :], fill_sem))

        fill(lambda c: c.start())
        fill(lambda c: c.wait())


def _dispatch(h2, rt):
    return pl.pallas_call(
        _dispatch_kernel,
        grid_spec=pltpu.PrefetchScalarGridSpec(
            num_scalar_prefetch=4, grid=(N_ROUTE_TILES,),
            in_specs=[pl.BlockSpec((2, ROUTE_TILE), lambda i, *_: (0, i)),
                      pl.BlockSpec((ROUTE_TILE, D_MODEL), lambda i, *_: (i, 0))],
            out_specs=pl.BlockSpec(memory_space=pl.ANY),
            scratch_shapes=[pltpu.VMEM((2, LOCAL_ROWS, D_MODEL), BF16), pltpu.VMEM((SLOT_TILE, D_MODEL), BF16),
                            pltpu.SemaphoreType.DMA((2,))]),
        out_shape=jax.ShapeDtypeStruct((N_SLOTS, D_MODEL), BF16),
        compiler_params=_params(1),
        name="moe_dispatch",
    )(rt["units"], rt["local_start"], rt["global_start"], rt["used"], rt["local_by_lane"], h2)


def _moe_kernel(tile_ref, expert_ref, off_ref, n_ref, x_ref, wg_ref, wu_ref, wd_ref, y_ref):
    j = pl.program_id(0)

    @pl.when(j < n_ref[0])
    def _():
        t, e = tile_ref[j], expert_ref[j]
        lo, hi = off_ref[e] - t * SLOT_TILE, off_ref[e + 1] - t * SLOT_TILE
        opens_tile = (j == 0) | (tile_ref[jnp.maximum(j - 1, 0)] != t)
        is_tail = e == TAIL_EXPERT

        def run(keep_other_rows, compute):
            blocks = [slice(r, r + MERGE_ROWS) for r in range(0, SLOT_TILE, MERGE_ROWS)]
            zero_rows = jnp.zeros((MERGE_ROWS, D_MODEL), F32)
            expert = lambda rows: _swiglu(x_ref[rows, :], wg_ref, wu_ref, wd_ref) if compute else zero_rows
            f_next = expert(blocks[0])
            for b, rows in enumerate(blocks):
                f = f_next
                if b + 1 < len(blocks):
                    f_next = expert(blocks[b + 1])
                row = lax.broadcasted_iota(jnp.int32, (MERGE_ROWS, 1), 0) + rows.start
                mine = (row >= lo) & (row < hi)
                y_ref[rows, :] = jnp.where(mine, f, y_ref[rows, :] if keep_other_rows else 0.0)

        for keep in (False, True):
            for tail in (False, True):
                in_case = (jnp.logical_not(opens_tile) if keep else opens_tile) & (is_tail if tail else jnp.logical_not(is_tail))

                @pl.when(in_case)
                def _(keep=keep, tail=tail):
                    run(keep, not tail)


def _moe_ffn(x_sorted, rt, wg, wu, wd, index):
    d_ff = wg.shape[-1]
    rows = pl.BlockSpec((SLOT_TILE, D_MODEL), lambda j, it, ie, off, n: (it[j], 0))
    expert = lambda j, it, ie, off, n: (index, jnp.minimum(ie[j], N_EXPERTS - 1), 0, 0)
    w_in = pl.BlockSpec((None, None, D_MODEL, d_ff), expert)
    w_out = pl.BlockSpec((None, None, d_ff, D_MODEL), expert)
    return pl.pallas_call(
        _moe_kernel,
        grid_spec=pltpu.PrefetchScalarGridSpec(
            num_scalar_prefetch=4, grid=(N_ITEMS,),
            in_specs=[rows, w_in, w_in, w_out], out_specs=rows),
        out_shape=jax.ShapeDtypeStruct((N_SLOTS, D_MODEL), F32),
        compiler_params=_params(1),
        name="moe_ffn",
    )(rt["item_tile"], rt["item_expert"], rt["off"], rt["n_items"], x_sorted, wg, wu, wd)


def _combine_kernel(units_ref, local_ref, global_ref, loc_lane_ref, w_lane_ref, loc_row_ref, x_ref, mod_ref, g2_ref, b2_ref,
                    y_hbm, yp_ref, ys_ref, yl, sem):
    i = pl.program_id(0)
    cur = i % 2

    def gather(tile, b, act):
        make = lambda l, g, rows: pltpu.make_async_copy(y_hbm.at[pl.ds(g, rows), :], yl.at[b, pl.ds(l, rows), :], sem.at[b])
        _group_copies(tile, units_ref, local_ref, global_ref, make, act)

    @pl.when(i == 0)
    def _():
        yl[...] = jnp.zeros_like(yl)
        gather(0, 0, lambda c: c.start())

    @pl.when(i + 1 < pl.num_programs(0))
    def _():
        gather(i + 1, 1 - cur, lambda c: c.start())

    gather(i, cur, lambda c: c.wait())
    slot = lax.broadcasted_iota(jnp.int32, (LOCAL_ROWS, ROUTE_TILE), 0)
    gate = jnp.sum(jnp.where(slot == loc_lane_ref[0:1, :], w_lane_ref[0:1, :], 0.0)
                   + jnp.where(slot == loc_lane_ref[1:2, :], w_lane_ref[1:2, :], 0.0), axis=1, keepdims=True)
    y = yl[cur] * gate
    hi = y.astype(BF16)
    lo = (y - hi.astype(F32)).astype(BF16)
    slot_t = lax.broadcasted_iota(jnp.int32, (ROUTE_TILE, LOCAL_ROWS), 1)
    picks = jnp.where((slot_t == loc_row_ref[:, 0:1]) | (slot_t == loc_row_ref[:, 1:2]), 1.0, 0.0).astype(BF16)
    f = jnp.dot(picks, hi, preferred_element_type=F32) + jnp.dot(picks, lo, preferred_element_type=F32)
    x2 = _layer_norm(ALPHA * x_ref[...] + mod_ref[0, 5:6, :] * f, g2_ref[...], b2_ref[...])

    @pl.when(i < NP_TOK // ROUTE_TILE)
    def _():
        yp_ref[...] = x2

    @pl.when(i >= NP_TOK // ROUTE_TILE)
    def _():
        ys_ref[...] = x2


def _combine(y_sorted, rt, x1, mods, g2, b2):
    tm = ROUTE_TILE
    n_p = NP_TOK // tm
    vec = pl.BlockSpec((1, D_MODEL), lambda i, *_: (0, 0))
    lanes = pl.BlockSpec((2, tm), lambda i, *_: (0, i))
    return pl.pallas_call(
        _combine_kernel,
        grid_spec=pltpu.PrefetchScalarGridSpec(
            num_scalar_prefetch=3, grid=(N_ROUTE_TILES,),
            in_specs=[lanes, lanes, pl.BlockSpec((tm, 2), lambda i, *_: (i, 0)),
                      pl.BlockSpec((tm, D_MODEL), lambda i, *_: (i, 0)), _mod_spec(tm), vec, vec,
                      pl.BlockSpec(memory_space=pl.ANY)],
            out_specs=[pl.BlockSpec((tm, D_MODEL), lambda i, *_: (jnp.minimum(i, n_p - 1), 0)),
                       pl.BlockSpec((tm, D_MODEL), lambda i, *_: (jnp.maximum(i - n_p, 0), 0))],
            scratch_shapes=[pltpu.VMEM((2, LOCAL_ROWS, D_MODEL), F32), pltpu.SemaphoreType.DMA((2,))]),
        out_shape=[jax.ShapeDtypeStruct((NP_TOK, D_MODEL), F32), jax.ShapeDtypeStruct((NS_TOK, D_MODEL), F32)],
        compiler_params=_params(1),
        name="moe_combine",
    )(rt["units"], rt["local_start"], rt["global_start"], rt["local_by_lane"], rt["weight_by_lane"], rt["local_by_row"],
      x1, mods, g2.reshape(1, D_MODEL), b2.reshape(1, D_MODEL), y_sorted)


def kernel(x_prompt, x_sample, cache_k, cache_v, c, c_ctx, ln_in_g, ln_in_b, ada_w, ada_b, w_in, conv_w, conv_b, w_conv_out, lam_q1, lam_k1, lam_q2, lam_k2, subln_g, w_attn_out, w_out, ln1_g, ln1_b, ln2_g, ln2_b, ffn_w_gate, ffn_w_up, ffn_w_down, moe_router, moe_w_gate, moe_w_up, moe_w_down):
    assert DEPTH == 2
    cvec = jnp.concatenate([c, c_ctx[None, :], jnp.zeros((MOD_ROWS - DEC_BATCH - 1, D_MODEL), F32)], axis=0)
    mods = _ada(cvec, ada_w, ada_b).reshape(DEPTH, MOD_ROWS, 6, D_MODEL)
    tables = _rope_tables()

    x, h = _ln_in(x_prompt.reshape(NP_TOK, D_MODEL), x_sample.reshape(NS_TOK, D_MODEL), ln_in_g, ln_in_b, mods[0])
    caches = None
    for l in range(DEPTH):
        lam_init = 0.8 - 0.6 * math.exp(-0.3 * l)
        lam_vecs = jnp.stack([lam_q1[l], lam_k1[l], lam_q2[l], lam_k2[l]]).astype(F32)
        g_sub = subln_g[l].reshape(1, V_DIM)

        conv_y = _conv_branch(h, w_in, conv_w, conv_b, l)
        q, k, v, *caches = _qkv_prompt(h, w_in, l, caches)
        o_p = _attn_prompt(lam_vecs, g_sub, q, k, v, lam_init)
        q, k, v = _qkv_sample(h, w_in, tables, l)
        o_s = _attn_sample(lam_vecs, g_sub, q, k, v, cache_k, cache_v, l, lam_init)

        mix_w = (w_in, w_conv_out, w_attn_out, w_out, ln1_g, ln1_b, l)
        i = l // 2
        if l % 2 == 0:
            x1, h2 = _merge(h, conv_y, o_p, o_s, x, mods[l], *mix_w)
            x, h = _ffn(h2, ffn_w_gate.astype(BF16), ffn_w_up.astype(BF16), ffn_w_down.astype(BF16),
                        x1, mods[l], ln2_g[l], ln2_b[l], mods[l + 1], i)
        else:
            r_t = moe_router[i].T
            r_hi = r_t.astype(BF16)
            router = jnp.concatenate([r_hi, (r_t - r_hi.astype(F32)).astype(BF16)], axis=0)
            x1, h2, route, counts = _merge(h, conv_y, o_p, o_s, x, mods[l], *mix_w, router)
            rt = _routing_tables(route, counts)
            x_sorted = _dispatch(h2, rt)
            y_sorted = _moe_ffn(x_sorted, rt, moe_w_gate, moe_w_up, moe_w_down, i)
            y_p, y_s = _combine(y_sorted, rt, x1, mods[l], ln2_g[l], ln2_b[l])

    return (y_p.reshape(BATCH, SEQ, D_MODEL), y_s.reshape(DEC_BATCH, DEC_SEQ, D_MODEL), caches[0], caches[1])
```

```python
import functools
import math

import jax
import jax.numpy as jnp
from jax import lax
from jax.experimental import pallas as pl
from jax.experimental.pallas import tpu as pltpu

D_MODEL = 1024
BATCH = 32
SEQ = 256
DEPTH = 2
DEC_BATCH = 8
DEC_SEQ = 1024
PAST_LEN = 512
GRID_W = 64
D_CONV = 512
N_HEADS = 8
HEAD_DIM = 64
V_DIM = 2 * HEAD_DIM
ATTN_W = N_HEADS * V_DIM
AXIS_DIM = HEAD_DIM // 2
ROPE_BASE = 10000.0
D_FF = 2816
N_EXPERTS = 8
D_FF_EXPERT = 1408
ALPHA = (2 * DEPTH) ** 0.25
LN_EPS = 1e-5
QK_SCALE = HEAD_DIM ** -0.5 * math.log2(math.e)

NP_TOK = BATCH * SEQ
NS_TOK = DEC_BATCH * DEC_SEQ
N_TOK = NP_TOK + NS_TOK
MOD_ROWS = 16
CTX_ROW = DEC_BATCH
LANES = 128
VMEM_LIMIT = 56 * 1024 * 1024

F32 = jnp.float32
BF16 = jnp.bfloat16
_NT = (((1,), (1,)), ((), ()))


def _params(n_axes, vmem=VMEM_LIMIT):
    return pltpu.CompilerParams(dimension_semantics=("arbitrary",) * n_axes, vmem_limit_bytes=vmem)


def _resident(shape):
    return pl.BlockSpec(shape, lambda *_: (0,) * len(shape), pipeline_mode=pl.Buffered(1))


def _weight(shape, *index):
    lead = len(index) - len(shape)
    return pl.BlockSpec((None,) * lead + tuple(shape), lambda *_: tuple(index), pipeline_mode=pl.Buffered(1))


def _dot(a, w):
    return jnp.dot(a.astype(w.dtype), w, preferred_element_type=F32)


def _mod_row(i, tm):
    n_p = NP_TOK // tm
    return jnp.where(i < n_p, CTX_ROW, (i - n_p) // (DEC_SEQ // tm))


def _mod_spec(tm):
    return pl.BlockSpec((1, 6, D_MODEL), lambda i, *_: (_mod_row(i, tm), 0, 0))


def _layer_norm(x, g, b):
    mu = jnp.mean(x, axis=-1, keepdims=True)
    xc = x - mu
    var = jnp.mean(xc * xc, axis=-1, keepdims=True)
    return xc * lax.rsqrt(var + LN_EPS) * g + b


def _ada_kernel(c_ref, w_ref, b_ref, o_ref):
    c = c_ref[...]
    a = (c * jax.nn.sigmoid(c)).astype(BF16)
    o_ref[0] = jnp.dot(a, w_ref[0].astype(BF16), preferred_element_type=F32) + b_ref[0]


def _ada(cvec, ada_w, ada_b):
    tn = 1024
    return pl.pallas_call(
        _ada_kernel,
        grid=(DEPTH, 6 * D_MODEL // tn),
        in_specs=[
            pl.BlockSpec((MOD_ROWS, D_MODEL), lambda l, j: (0, 0)),
            pl.BlockSpec((1, D_MODEL, tn), lambda l, j: (l, 0, j)),
            pl.BlockSpec((1, 1, tn), lambda l, j: (l, 0, j)),
        ],
        out_specs=pl.BlockSpec((1, MOD_ROWS, tn), lambda l, j: (l, 0, j)),
        out_shape=jax.ShapeDtypeStruct((DEPTH, MOD_ROWS, 6 * D_MODEL), F32),
        compiler_params=_params(2),
        name="ada",
    )(cvec, ada_w, ada_b.reshape(DEPTH, 1, 6 * D_MODEL))


def _ln_in_kernel(xp_ref, xs_ref, g_ref, b_ref, mod_ref, x_ref, h_ref, *, n_p):
    i = pl.program_id(0)

    def emit(src_ref):
        y = _layer_norm(src_ref[...], g_ref[...], b_ref[...])
        x_ref[...] = y
        h_ref[...] = (y * (1 + mod_ref[0, 1:2, :]) + mod_ref[0, 0:1, :]).astype(BF16)

    @pl.when(i < n_p)
    def _():
        emit(xp_ref)

    @pl.when(i >= n_p)
    def _():
        emit(xs_ref)


def _ln_in(xp, xs, g, b, mods):
    tm = 1024
    n_p = NP_TOK // tm
    tile = lambda i: (i, 0)
    return pl.pallas_call(
        functools.partial(_ln_in_kernel, n_p=n_p),
        grid=(N_TOK // tm,),
        in_specs=[
            pl.BlockSpec((tm, D_MODEL), lambda i: (jnp.minimum(i, n_p - 1), 0)),
            pl.BlockSpec((tm, D_MODEL), lambda i: (jnp.maximum(i - n_p, 0), 0)),
            pl.BlockSpec((1, D_MODEL), lambda i: (0, 0)),
            pl.BlockSpec((1, D_MODEL), lambda i: (0, 0)),
            _mod_spec(tm),
        ],
        out_specs=[pl.BlockSpec((tm, D_MODEL), tile), pl.BlockSpec((tm, D_MODEL), tile)],
        out_shape=[jax.ShapeDtypeStruct((N_TOK, D_MODEL), F32), jax.ShapeDtypeStruct((N_TOK, D_MODEL), BF16)],
        compiler_params=_params(1),
        name="ln_in",
    )(xp, xs, g.reshape(1, D_MODEL), b.reshape(1, D_MODEL), mods)


CONV_CHUNK = 256


def _conv_kernel(h_ref, w_ref, cw_ref, cb_ref, y_ref, *, tm):
    i = pl.program_id(0)
    h = h_ref[...].astype(F32)
    seq = jnp.where(i < NP_TOK // tm, SEQ, DEC_SEQ)
    pos = lax.broadcasted_iota(jnp.int32, (tm, 1), 0) & (seq - 1)
    proj = lambda c: tuple(_dot(h, w_ref[:, part * D_CONV + c:part * D_CONV + c + CONV_CHUNK]) for part in range(3))
    pending = proj(0)
    for c in range(0, D_CONV, CONV_CHUNK):
        gate_b, gate_c, u = pending
        if c + CONV_CHUNK < D_CONV:
            pending = proj(c + CONV_CHUNK)
        cols = slice(c, c + CONV_CHUNK)
        pc = gate_c * u
        prev = jnp.where(pos == 0, 0.0, pltpu.roll(pc, 1, axis=0))
        nxt = jnp.where(pos == seq - 1, 0.0, pltpu.roll(pc, tm - 1, axis=0))
        conv = prev * cw_ref[0:1, cols] + pc * cw_ref[1:2, cols] + nxt * cw_ref[2:3, cols] + cb_ref[:, cols]
        y_ref[:, cols] = (gate_b * conv).astype(BF16)


def _conv_branch(h, w_in, conv_w, conv_b, layer):
    tm = DEC_SEQ
    return pl.pallas_call(
        functools.partial(_conv_kernel, tm=tm),
        grid=(N_TOK // tm,),
        in_specs=[
            pl.BlockSpec((tm, D_MODEL), lambda i: (i, 0)),
            _weight((D_MODEL, 3 * D_CONV), layer, 0, 0),
            pl.BlockSpec((None, 3, D_CONV), lambda i: (layer, 0, 0)),
            pl.BlockSpec((None, 1, D_CONV), lambda i: (layer, 0, 0)),
        ],
        out_specs=pl.BlockSpec((tm, D_CONV), lambda i: (i, 0)),
        out_shape=jax.ShapeDtypeStruct((N_TOK, D_CONV), BF16),
        compiler_params=_params(1),
        name="conv_branch",
    )(h, w_in, conv_w, conv_b.reshape(DEPTH, 1, D_CONV))


def _rope_tables():
    pos = jnp.arange(DEC_SEQ)
    row = (pos // GRID_W).astype(F32)
    col = (pos % GRID_W).astype(F32)
    inv_freq = ROPE_BASE ** (-jnp.arange(0, AXIS_DIM, 2, dtype=F32) / AXIS_DIM)
    ang_r = row[:, None] * inv_freq
    ang_c = col[:, None] * inv_freq
    lane = jnp.arange(V_DIM)
    sub = lane % HEAD_DIM
    ang = jnp.where((sub < AXIS_DIM)[None, :], ang_r[:, lane % (AXIS_DIM // 2)], ang_c[:, lane % (AXIS_DIM // 2)])
    first = ((lane % AXIS_DIM) < AXIS_DIM // 2)[None, :]
    cos, sin = jnp.cos(ang), jnp.sin(ang)
    return cos, jnp.where(first, -sin, 0.0), jnp.where(first, 0.0, sin)


QKV_CHUNK = 2 * V_DIM


def _project_chunks(h, w_refs, emit):
    per_ref = w_refs[0].shape[-1] // QKV_CHUNK
    n = per_ref * len(w_refs)
    h = h.astype(w_refs[0].dtype)
    proj = lambda c: _dot(h, w_refs[c // per_ref][:, (c % per_ref) * QKV_CHUNK:(c % per_ref + 1) * QKV_CHUNK])
    y_next = proj(0)
    for c in range(n):
        y = y_next
        if c + 1 < n:
            y_next = proj(c + 1)
        emit(c, y)


def _qkv_prompt_kernel(h_ref, wa_ref, wb_ref, *rest, tm, layer, first):
    q_ref, k_ref, v_ref, kc_all, vc_all = rest[-5:]
    kc_ref, vc_ref = (kc_all.at[:, layer], vc_all.at[:, layer]) if first else (kc_all, vc_all)
    per_part = ATTN_W // QKV_CHUNK

    def emit(c, y):
        part, cols = c // per_part, slice((c % per_part) * QKV_CHUNK, (c % per_part + 1) * QKV_CHUNK)
        if part == 0:
            q_ref[:, cols] = (y * QK_SCALE).astype(BF16)
            return
        act_ref, cache_ref = (k_ref, kc_ref) if part == 1 else (v_ref, vc_ref)
        act_ref[:, cols] = y.astype(BF16)
        for s in range(tm // SEQ):
            for j in range(QKV_CHUNK // V_DIM):
                cache_ref[s, (c % per_part) * (QKV_CHUNK // V_DIM) + j] = y[s * SEQ:(s + 1) * SEQ, j * V_DIM:(j + 1) * V_DIM]

    _project_chunks(h_ref[...], (wa_ref, wb_ref), emit)
    if first:
        for other in range(DEPTH):
            if other != layer:
                kc_all[:, other] = jnp.zeros((tm // SEQ, N_HEADS, SEQ, V_DIM), F32)
                vc_all[:, other] = jnp.zeros((tm // SEQ, N_HEADS, SEQ, V_DIM), F32)


def _qkv_sample_kernel(h_ref, wa_ref, wb_ref, cos_ref, sup_ref, sdn_ref, q_ref, k_ref, v_ref):
    cos, s_up, s_dn = cos_ref[...], sup_ref[...], sdn_ref[...]
    per_part = ATTN_W // QKV_CHUNK

    def rope(x):
        return x * cos + pltpu.roll(x, V_DIM - AXIS_DIM // 2, axis=1) * s_up + pltpu.roll(x, AXIS_DIM // 2, axis=1) * s_dn

    def emit(c, y):
        part, c0 = c // per_part, (c % per_part) * QKV_CHUNK
        if part == 2:
            v_ref[:, c0:c0 + QKV_CHUNK] = y.astype(BF16)
            return
        for j in range(QKV_CHUNK // V_DIM):
            r = rope(y[:, j * V_DIM:(j + 1) * V_DIM])
            if part == 0:
                q_ref[:, c0 + j * V_DIM:c0 + (j + 1) * V_DIM] = (r * QK_SCALE).astype(BF16)
            else:
                k_ref[:, c0 + j * V_DIM:c0 + (j + 1) * V_DIM] = r.astype(BF16)

    _project_chunks(h_ref[...], (wa_ref, wb_ref), emit)


def _qkv_weights(layer):
    assert 2 * 3 * D_CONV == 3 * ATTN_W
    return [_weight((D_MODEL, 3 * D_CONV), layer, 0, 1), _weight((D_MODEL, 3 * D_CONV), layer, 0, 2)]


def _qkv_prompt(h, w_in, layer, caches):
    tm = 512
    tile = pl.BlockSpec((tm, ATTN_W), lambda i: (i, 0))
    first = caches is None
    if first:
        cache = pl.BlockSpec((tm // SEQ, DEPTH, N_HEADS, SEQ, V_DIM), lambda i: (i, 0, 0, 0, 0))
    else:
        cache = pl.BlockSpec((tm // SEQ, None, N_HEADS, SEQ, V_DIM), lambda i: (i, layer, 0, 0, 0))
    act = jax.ShapeDtypeStruct((NP_TOK, ATTN_W), BF16)
    ctx = jax.ShapeDtypeStruct((BATCH, DEPTH, N_HEADS, SEQ, V_DIM), F32)
    in_specs = [pl.BlockSpec((tm, D_MODEL), lambda i: (i, 0)), *_qkv_weights(layer)]
    args = [h, w_in, w_in]
    aliases = {}
    if not first:
        aliases = {len(args): 3, len(args) + 1: 4}
        in_specs += [pl.BlockSpec(memory_space=pl.ANY)] * 2
        args += list(caches)
    return pl.pallas_call(
        functools.partial(_qkv_prompt_kernel, tm=tm, layer=layer, first=first),
        grid=(NP_TOK // tm,),
        in_specs=in_specs,
        out_specs=[tile, tile, tile, cache, cache],
        out_shape=[act, act, act, ctx, ctx],
        input_output_aliases=aliases,
        compiler_params=_params(1),
        name="qkv_prompt",
    )(*args)


def _qkv_sample(h, w_in, tables, layer):
    tm = 512
    first_tile = NP_TOK // tm
    tile = pl.BlockSpec((tm, ATTN_W), lambda i: (i, 0))
    tab = pl.BlockSpec((tm, V_DIM), lambda i: (i % (DEC_SEQ // tm), 0))
    act = jax.ShapeDtypeStruct((NS_TOK, ATTN_W), BF16)
    return pl.pallas_call(
        _qkv_sample_kernel,
        grid=(NS_TOK // tm,),
        in_specs=[pl.BlockSpec((tm, D_MODEL), lambda i: (first_tile + i, 0)), *_qkv_weights(layer), tab, tab, tab],
        out_specs=[tile, tile, tile],
        out_shape=[act, act, act],
        compiler_params=_params(1),
        name="qkv_sample",
    )(h, w_in, w_in, *tables)


def _lam(lam_ref, lam_init):
    a = jnp.sum(lam_ref[0:1, :] * lam_ref[1:2, :], axis=1, keepdims=True)
    b = jnp.sum(lam_ref[2:3, :] * lam_ref[3:4, :], axis=1, keepdims=True)
    return jnp.exp(a) - jnp.exp(b) + lam_init


def _scores(q, k):
    lo = lax.broadcasted_iota(jnp.int32, (1, V_DIM), 1) < HEAD_DIM
    zero = jnp.zeros_like(q)
    qq = jnp.concatenate([jnp.where(lo, q, zero), jnp.where(lo, zero, q)], axis=0)
    return lax.dot_general(qq, k, _NT, preferred_element_type=F32)


def _diff_probs(s, lam):
    tq = s.shape[0] // 2
    e = jnp.exp2(s - jnp.max(s, axis=-1, keepdims=True))
    l = jnp.sum(e, axis=-1, keepdims=True)
    p = e[:tq] - e[tq:] * (lam * l[:tq] / l[tq:])
    return p.astype(BF16), 1.0 / l[:tq]


def _head_out(p, inv_l1, v, g, lam_init):
    o = jnp.dot(p, v, preferred_element_type=F32) * inv_l1
    ms = jnp.mean(o * o, axis=-1, keepdims=True)
    return (o * lax.rsqrt(ms + LN_EPS) * g * (1 - lam_init)).astype(BF16)


def _diff_attn_tiles(tiles, lam, g, lam_init):
    n = len(tiles)
    scores = lambda t: _scores(tiles[t][0](), tiles[t][1]())
    s = {0: scores(0)}
    probs = {}
    for t in range(-1, n):
        if t + 2 < n:
            s[t + 2] = scores(t + 2)
        if t + 1 < n:
            if t + 1 not in s:
                s[t + 1] = scores(t + 1)
            probs[t + 1] = _diff_probs(s.pop(t + 1), lam)
        if t >= 0:
            p, inv_l1 = probs.pop(t)
            tiles[t][3](_head_out(p, inv_l1, tiles[t][2](), g, lam_init))


ATTN_SEQS_PER_STEP = 4
ATTN_HEADS_PER_STEP = 4


def _attn_prompt_kernel(lam_ref, g_ref, q_ref, k_ref, v_ref, o_ref, *, lam_init):
    def tile(s, hd):
        rows, cols = slice(s * SEQ, (s + 1) * SEQ), slice(hd * V_DIM, (hd + 1) * V_DIM)

        def store(o):
            o_ref[rows, cols] = o

        return (lambda: q_ref[rows, cols], lambda: k_ref[rows, cols], lambda: v_ref[rows, cols], store)

    tiles = [tile(s, hd) for s in range(ATTN_SEQS_PER_STEP) for hd in range(N_HEADS)]
    _diff_attn_tiles(tiles, _lam(lam_ref, lam_init), g_ref[...], lam_init)


def _attn_prompt(lam_vecs, g, q, k, v, lam_init):
    blk = pl.BlockSpec((ATTN_SEQS_PER_STEP * SEQ, ATTN_W), lambda b: (b, 0))
    return pl.pallas_call(
        functools.partial(_attn_prompt_kernel, lam_init=lam_init),
        grid=(BATCH // ATTN_SEQS_PER_STEP,),
        in_specs=[pl.BlockSpec((4, HEAD_DIM), lambda b: (0, 0)), pl.BlockSpec((1, V_DIM), lambda b: (0, 0)), blk, blk, blk],
        out_specs=blk,
        out_shape=jax.ShapeDtypeStruct((NP_TOK, ATTN_W), BF16),
        compiler_params=_params(1),
        name="attn_prompt",
    )(lam_vecs, g, q, k, v)


def _attn_sample_kernel(lam_ref, g_ref, q_ref, kn_ref, vn_ref, kc_ref, vc_ref, o_ref, k_s, v_s, *, lam_init, tq):
    for hd in range(ATTN_HEADS_PER_STEP):
        cols = slice(hd * V_DIM, (hd + 1) * V_DIM)
        k_s[hd, 0:PAST_LEN, :] = kc_ref[hd].astype(BF16)
        k_s[hd, PAST_LEN:, :] = kn_ref[:, cols]
        v_s[hd, 0:PAST_LEN, :] = vc_ref[hd].astype(BF16)
        v_s[hd, PAST_LEN:, :] = vn_ref[:, cols]

    def tile(hd, t):
        rows, cols = slice(t * tq, (t + 1) * tq), slice(hd * V_DIM, (hd + 1) * V_DIM)

        def store(o):
            o_ref[rows, cols] = o

        return (lambda: q_ref[rows, cols], lambda: k_s[hd], lambda: v_s[hd], store)

    tiles = [tile(hd, t) for hd in range(ATTN_HEADS_PER_STEP) for t in range(DEC_SEQ // tq)]
    _diff_attn_tiles(tiles, _lam(lam_ref, lam_init), g_ref[...], lam_init)


def _attn_sample(lam_vecs, g, q, k, v, cache_k, cache_v, layer, lam_init):
    tq = 128
    hps = ATTN_HEADS_PER_STEP
    new = pl.BlockSpec((DEC_SEQ, hps * V_DIM), lambda b, h: (b, h))
    past = pl.BlockSpec((None, None, hps, PAST_LEN, V_DIM), lambda b, h: (b, layer, h, 0, 0))
    kv_all = pltpu.VMEM((hps, PAST_LEN + DEC_SEQ, V_DIM), BF16)
    return pl.pallas_call(
        functools.partial(_attn_sample_kernel, lam_init=lam_init, tq=tq),
        grid=(DEC_BATCH, N_HEADS // hps),
        in_specs=[pl.BlockSpec((4, HEAD_DIM), lambda b, h: (0, 0)), pl.BlockSpec((1, V_DIM), lambda b, h: (0, 0)),
                  new, new, new, past, past],
        out_specs=pl.BlockSpec((DEC_SEQ, hps * V_DIM), lambda b, h: (b, h)),
        out_shape=jax.ShapeDtypeStruct((NS_TOK, ATTN_W), BF16),
        scratch_shapes=[kv_all, kv_all],
        compiler_params=_params(2),
        name="attn_sample",
    )(lam_vecs, g, q, k, v, cache_k, cache_v)


MERGE_ROWS = 256


def _route(h2, router_ref, cnt_ref):
    tm = h2.shape[0]
    hi = h2.astype(BF16)
    lo = (h2 - hi.astype(F32)).astype(BF16)
    a = lax.dot_general(router_ref[...], hi, _NT, preferred_element_type=F32)
    b = lax.dot_general(router_ref[0:N_EXPERTS, :], lo, _NT, preferred_element_type=F32)
    logits = a[:N_EXPERTS] + a[N_EXPERTS:] + b
    e = jnp.exp(logits - jnp.max(logits, axis=0, keepdims=True))
    p = e / jnp.sum(e, axis=0, keepdims=True)
    row = lax.broadcasted_iota(jnp.int32, p.shape, 0)
    v1 = jnp.max(p, axis=0, keepdims=True)
    i1 = jnp.min(jnp.where(p == v1, row, N_EXPERTS), axis=0, keepdims=True)
    p2 = jnp.where(row == i1, -1.0, p)
    v2 = jnp.max(p2, axis=0, keepdims=True)
    i2 = jnp.min(jnp.where(p2 == v2, row, N_EXPERTS), axis=0, keepdims=True)
    den = v1 + v2
    pick1, pick2 = row == i1, row == i2
    picked = jnp.where(pick1 | pick2, 1.0, 0.0)
    before = lax.broadcasted_iota(jnp.int32, (tm, tm), 0) < lax.broadcasted_iota(jnp.int32, (tm, tm), 1)
    ahead = jnp.dot(picked.astype(BF16), jnp.where(before, 1.0, 0.0).astype(BF16), preferred_element_type=F32)
    ahead = ahead + cnt_ref[:, 0:1]
    rank1 = jnp.sum(jnp.where(pick1, ahead, 0.0), axis=0, keepdims=True)
    rank2 = jnp.sum(jnp.where(pick2, ahead, 0.0), axis=0, keepdims=True)
    cnt_ref[...] = cnt_ref[...] + jnp.sum(picked, axis=1, keepdims=True)
    zero = jnp.zeros_like(v1)
    return jnp.concatenate([i1.astype(F32), i2.astype(F32), rank1, rank2, v1 / den, v2 / den, zero, zero], axis=0)


def _merge_kernel(h_ref, cy_ref, op_ref, os_ref, x_ref, mod_ref, wga_ref, wgb_ref, wc_ref, wa_ref, wo_ref, g1_ref, b1_ref,
                  *rest, tm, routed):
    if routed:
        router_ref, x1_ref, h2_ref, route_ref, count_ref, cnt_ref = rest
    else:
        x1_ref, h2_ref = rest
    i = pl.program_id(0)
    if routed:
        cnt_ref[...] = jnp.zeros_like(cnt_ref)

    def mix(rows):
        h = h_ref[rows, :]
        g = jnp.concatenate([_dot(h, wga_ref[...]), _dot(h, wgb_ref[...])], axis=1)
        y_conv = _dot(cy_ref[rows, :], wc_ref[...])
        o = jnp.where(i < NP_TOK // tm, op_ref[rows, :], os_ref[rows, :])
        y_attn = _dot(o, wa_ref[...])
        merged = jax.nn.sigmoid(g[:, :D_MODEL]) * y_conv + jax.nn.sigmoid(g[:, D_MODEL:]) * y_attn
        return _dot(merged.astype(BF16), wo_ref[...])

    def finish(rows, m):
        x1 = _layer_norm(ALPHA * x_ref[rows, :] + mod_ref[0, 2:3, :] * m, g1_ref[...], b1_ref[...])
        x1_ref[rows, :] = x1
        h2 = x1 * (1 + mod_ref[0, 4:5, :]) + mod_ref[0, 3:4, :]
        h2_ref[rows, :] = h2.astype(h2_ref.dtype)
        if routed:
            route_ref[:, rows] = _route(h2, router_ref, cnt_ref)

    blocks = [slice(r, r + MERGE_ROWS) for r in range(0, tm, MERGE_ROWS)]
    m_next = mix(blocks[0])
    for b, rows in enumerate(blocks):
        m = m_next
        if b + 1 < len(blocks):
            m_next = mix(blocks[b + 1])
        finish(rows, m)
    if routed:
        count_ref[...] = cnt_ref[...]
        group = jnp.ceil(cnt_ref[:, 0:1] * (1.0 / GROUP_ALIGN)) * GROUP_ALIGN
        expert = lax.broadcasted_iota(jnp.int32, (N_EXPERTS, 1), 0)
        start = jnp.zeros_like(group)
        for e in range(N_EXPERTS - 1):
            start = start + jnp.where(expert > e, group[e:e + 1, :], 0.0)
        expert_f = expert.astype(F32)
        for k in range(2):
            mine = route_ref[k:k + 1, :] == expert_f
            route_ref[6 + k:7 + k, :] = jnp.sum(jnp.where(mine, start, 0.0), axis=0, keepdims=True) + route_ref[2 + k:3 + k, :]


def _merge(h, conv_y, o_p, o_s, x, mods, w_in, w_conv_out, w_attn_out, w_out, g1, b1, layer, router=None):
    tm = 512
    n_p = NP_TOK // tm
    routed = router is not None
    tile = lambda w: pl.BlockSpec((tm, w), lambda i: (i, 0))
    vec = pl.BlockSpec((None, 1, D_MODEL), lambda i: (layer, 0, 0))
    gate0 = 3 * D_CONV + 3 * ATTN_W
    rest = 2 * D_MODEL - 3 * D_CONV
    assert gate0 % (3 * D_CONV) == 0 and (gate0 + 3 * D_CONV) % rest == 0
    in_specs = [
        tile(D_MODEL), tile(D_CONV),
        pl.BlockSpec((tm, ATTN_W), lambda i: (jnp.minimum(i, n_p - 1), 0)),
        pl.BlockSpec((tm, ATTN_W), lambda i: (jnp.maximum(i - n_p, 0), 0)),
        tile(D_MODEL), _mod_spec(tm),
        _weight((D_MODEL, 3 * D_CONV), layer, 0, gate0 // (3 * D_CONV)),
        _weight((D_MODEL, rest), layer, 0, (gate0 + 3 * D_CONV) // rest),
        _weight((D_CONV, D_MODEL), layer, 0, 0), _weight((ATTN_W, D_MODEL), layer, 0, 0),
        _weight((D_MODEL, D_MODEL), layer, 0, 0), vec, vec,
    ]
    args = [h, conv_y, o_p, o_s, x, mods, w_in, w_in, w_conv_out, w_attn_out, w_out,
            g1.reshape(DEPTH, 1, D_MODEL), b1.reshape(DEPTH, 1, D_MODEL)]
    out_specs = [tile(D_MODEL), tile(D_MODEL)]
    out_shape = [jax.ShapeDtypeStruct((N_TOK, D_MODEL), F32), jax.ShapeDtypeStruct((N_TOK, D_MODEL), BF16)]
    scratch = []
    if routed:
        in_specs.append(_resident((2 * N_EXPERTS, D_MODEL)))
        args.append(router)
        out_specs += [pl.BlockSpec((N_EXPERTS, tm), lambda i: (0, i)), pl.BlockSpec((N_EXPERTS, LANES), lambda i: (i, 0))]
        out_shape += [jax.ShapeDtypeStruct((N_EXPERTS, N_TOK), F32),
                      jax.ShapeDtypeStruct((N_TOK // tm * N_EXPERTS, LANES), F32)]
        scratch = [pltpu.VMEM((N_EXPERTS, LANES), F32)]
    return pl.pallas_call(
        functools.partial(_merge_kernel, tm=tm, routed=routed),
        grid=(N_TOK // tm,),
        in_specs=in_specs, out_specs=out_specs, out_shape=out_shape,
        scratch_shapes=scratch,
        compiler_params=_params(1),
        name="merge_routed" if routed else "merge",
    )(*args)


FF_CHUNK = 256


def _swiglu(x, wg_ref, wu_ref, wd_ref):
    d_ff = wg_ref.shape[-1]
    bounds = [(c, min(c + FF_CHUNK, d_ff)) for c in range(0, d_ff, FF_CHUNK)]
    x = x.astype(wg_ref.dtype)

    def up(lo, hi):
        return (jnp.dot(x, wg_ref[:, lo:hi], preferred_element_type=F32),
                jnp.dot(x, wu_ref[:, lo:hi], preferred_element_type=F32))

    f = None
    pending = up(*bounds[0])
    for c, (lo, hi) in enumerate(bounds):
        a, u = pending
        if c + 1 < len(bounds):
            pending = up(*bounds[c + 1])
        hid = (a * jax.nn.sigmoid(a) * u).astype(BF16).astype(wd_ref.dtype)
        d = jnp.dot(hid, wd_ref[lo:hi, :], preferred_element_type=F32)
        f = d if f is None else f + d
    return f


def _ffn_kernel(h_ref, wg_ref, wu_ref, wd_ref, x_ref, mod_ref, g2_ref, b2_ref, nmod_ref, x2_ref, hn_ref):
    def finish(rows, f):
        x2 = _layer_norm(ALPHA * x_ref[rows, :] + mod_ref[0, 5:6, :] * f, g2_ref[...], b2_ref[...])
        x2_ref[rows, :] = x2
        hn_ref[rows, :] = (x2 * (1 + nmod_ref[0, 1:2, :]) + nmod_ref[0, 0:1, :]).astype(BF16)

    blocks = [slice(r, r + MERGE_ROWS) for r in range(0, h_ref.shape[0], MERGE_ROWS)]
    f_next = _swiglu(h_ref[blocks[0], :], wg_ref, wu_ref, wd_ref)
    for b, rows in enumerate(blocks):
        f = f_next
        if b + 1 < len(blocks):
            f_next = _swiglu(h_ref[blocks[b + 1], :], wg_ref, wu_ref, wd_ref)
        finish(rows, f)


def _ffn(h2, wg, wu, wd, x1, mods, g2, b2, next_mods, index):
    tm = 512
    d_ff = wg.shape[-1]
    tile = lambda w: pl.BlockSpec((tm, w), lambda i: (i, 0))
    vec = pl.BlockSpec((1, D_MODEL), lambda i: (0, 0))
    return pl.pallas_call(
        _ffn_kernel,
        grid=(N_TOK // tm,),
        in_specs=[tile(D_MODEL), _weight((D_MODEL, d_ff), index, 0, 0), _weight((D_MODEL, d_ff), index, 0, 0),
                  _weight((d_ff, D_MODEL), index, 0, 0), tile(D_MODEL), _mod_spec(tm), vec, vec, _mod_spec(tm)],
        out_specs=[tile(D_MODEL), tile(D_MODEL)],
        out_shape=[jax.ShapeDtypeStruct((N_TOK, D_MODEL), F32), jax.ShapeDtypeStruct((N_TOK, D_MODEL), BF16)],
        compiler_params=_params(1),
        name="ffn",
    )(h2, wg, wu, wd, x1, mods, g2.reshape(1, D_MODEL), b2.reshape(1, D_MODEL), next_mods)


N_PAIRS = 2 * N_TOK
ROUTE_TILE = 512
N_ROUTE_TILES = N_TOK // ROUTE_TILE
GROUP_ALIGN = 16
LOCAL_ROWS = 2 * ROUTE_TILE + N_EXPERTS * GROUP_ALIGN
N_SLOTS = N_PAIRS + N_ROUTE_TILES * N_EXPERTS * GROUP_ALIGN
SLOT_TILE = 512
N_SLOT_TILES = N_SLOTS // SLOT_TILE
N_ITEMS = N_SLOT_TILES + N_EXPERTS
COPY_BITS = (ROUTE_TILE // GROUP_ALIGN).bit_length()
TAIL_EXPERT = N_EXPERTS

def _routing_tables(route, counts):
    n = counts.reshape(N_ROUTE_TILES, N_EXPERTS, LANES)[:, :, 0].astype(jnp.int32)
    g = (n + GROUP_ALIGN - 1) // GROUP_ALIGN * GROUP_ALIGN
    local_start = jnp.cumsum(g, axis=1) - g
    region = jnp.concatenate([jnp.zeros((1,), jnp.int32), jnp.cumsum(jnp.sum(g, axis=0))])
    global_start = region[None, :-1] + jnp.cumsum(g, axis=0) - g
    off = jnp.concatenate([region, jnp.full((1,), N_SLOTS, jnp.int32)])
    local = route[6:8].astype(jnp.int32)
    t0 = jnp.arange(N_SLOT_TILES, dtype=jnp.int32)[:, None] * SLOT_TILE
    live = jnp.maximum(off[None, :-1], t0) < jnp.minimum(off[None, 1:], t0 + SLOT_TILE)
    n_items = jnp.sum(live).astype(jnp.int32)
    order = jnp.nonzero(live.reshape(-1), size=N_ITEMS, fill_value=0)[0].astype(jnp.int32)
    order = jnp.where(jnp.arange(N_ITEMS) < n_items, order, order[n_items - 1])
    return dict(
        units=(g // GROUP_ALIGN).reshape(-1), local_start=local_start.reshape(-1), global_start=global_start.reshape(-1),
        used=region[-1:], off=off, n_items=n_items.reshape(1),
        item_tile=order // (N_EXPERTS + 1), item_expert=order % (N_EXPERTS + 1),
        local_by_lane=local, local_by_row=local.T, weight_by_lane=route[4:6])


def _for_each_chunk(units, fn):
    for b in range(COPY_BITS):
        @pl.when(((units >> b) & 1) == 1)
        def _():
            fn(pl.multiple_of((units & ((1 << b) - 1)) * GROUP_ALIGN, GROUP_ALIGN), GROUP_ALIGN << b)


def _group_copies(tile, units_ref, local_ref, global_ref, make, act):
    for e in range(N_EXPERTS):
        g = tile * N_EXPERTS + e

        def chunk(off, rows, g=g):
            act(make(pl.multiple_of(local_ref[g] + off, GROUP_ALIGN), pl.multiple_of(global_ref[g] + off, GROUP_ALIGN), rows))

        _for_each_chunk(units_ref[g], chunk)


def _dispatch_kernel(units_ref, local_ref, global_ref, used_ref, loc_ref, h_ref, xs_hbm, xl, zeros, sem):
    i = pl.program_id(0)
    cur = i % 2
    last = pl.num_programs(0) - 1
    slot = lax.broadcasted_iota(jnp.int32, (LOCAL_ROWS, ROUTE_TILE), 0)
    one_hot = jnp.where((slot == loc_ref[0:1, :]) | (slot == loc_ref[1:2, :]), 1.0, 0.0).astype(BF16)
    xl[cur] = jnp.dot(one_hot, h_ref[...], preferred_element_type=F32).astype(BF16)

    def copies(tile, b, act):
        make = lambda l, g, rows: pltpu.make_async_copy(xl.at[b, pl.ds(l, rows), :], xs_hbm.at[pl.ds(g, rows), :], sem.at[b])
        _group_copies(tile, units_ref, local_ref, global_ref, make, act)

    copies(i, cur, lambda c: c.start())

    @pl.when(i > 0)
    def _():
        copies(i - 1, 1 - cur, lambda c: c.wait())

    @pl.when(i == last)
    def _():
        copies(i, cur, lambda c: c.wait())

    @pl.when(i == last)
    def _():
        zeros[...] = jnp.zeros_like(zeros)
        used = used_ref[0]
        tail = (N_SLOTS - used) // GROUP_ALIGN
        small, n_big = tail % (SLOT_TILE // GROUP_ALIGN), tail // (SLOT_TILE // GROUP_ALIGN)
        big0 = used + small * GROUP_ALIGN

        fill_sem = sem.at[0]

        def fill(act):
            _for_each_chunk(small, lambda off, rows: act(pltpu.make_async_copy(
                zeros.at[pl.ds(0, rows), :], xs_hbm.at[pl.ds(pl.multiple_of(used + off, GROUP_ALIGN), rows), :], fill_sem)))
            for k in range((N_SLOTS - N_PAIRS) // SLOT_TILE):
                @pl.when(k < n_big)
                def _():
                    act(pltpu.make_async_copy(
                        zeros, xs_hbm.at[pl.ds(pl.multiple_of(big0 + k * SLOT_TILE, GROUP_ALIGN), SLOT_TILE), :], fill_sem))

        fill(lambda c: c.start())
        fill(lambda c: c.wait())


def _dispatch(h2, rt):
    return pl.pallas_call(
        _dispatch_kernel,
        grid_spec=pltpu.PrefetchScalarGridSpec(
            num_scalar_prefetch=4, grid=(N_ROUTE_TILES,),
            in_specs=[pl.BlockSpec((2, ROUTE_TILE), lambda i, *_: (0, i)),
                      pl.BlockSpec((ROUTE_TILE, D_MODEL), lambda i, *_: (i, 0))],
            out_specs=pl.BlockSpec(memory_space=pl.ANY),
            scratch_shapes=[pltpu.VMEM((2, LOCAL_ROWS, D_MODEL), BF16), pltpu.VMEM((SLOT_TILE, D_MODEL), BF16),
                            pltpu.SemaphoreType.DMA((2,))]),
        out_shape=jax.ShapeDtypeStruct((N_SLOTS, D_MODEL), BF16),
        compiler_params=_params(1),
        name="moe_dispatch",
    )(rt["units"], rt["local_start"], rt["global_start"], rt["used"], rt["local_by_lane"], h2)


def _moe_kernel(tile_ref, expert_ref, off_ref, n_ref, x_ref, wg_ref, wu_ref, wd_ref, y_ref):
    j = pl.program_id(0)

    @pl.when(j < n_ref[0])
    def _():
        t, e = tile_ref[j], expert_ref[j]
        lo, hi = off_ref[e] - t * SLOT_TILE, off_ref[e + 1] - t * SLOT_TILE
        opens_tile = (j == 0) | (tile_ref[jnp.maximum(j - 1, 0)] != t)
        is_tail = e == TAIL_EXPERT

        def run(keep_other_rows, compute):
            blocks = [slice(r, r + MERGE_ROWS) for r in range(0, SLOT_TILE, MERGE_ROWS)]
            zero_rows = jnp.zeros((MERGE_ROWS, D_MODEL), F32)
            expert = lambda rows: _swiglu(x_ref[rows, :], wg_ref, wu_ref, wd_ref) if compute else zero_rows
            f_next = expert(blocks[0])
            for b, rows in enumerate(blocks):
                f = f_next
                if b + 1 < len(blocks):
                    f_next = expert(blocks[b + 1])
                row = lax.broadcasted_iota(jnp.int32, (MERGE_ROWS, 1), 0) + rows.start
                mine = (row >= lo) & (row < hi)
                y_ref[rows, :] = jnp.where(mine, f, y_ref[rows, :] if keep_other_rows else 0.0)

        for keep in (False, True):
            for tail in (False, True):
                in_case = (jnp.logical_not(opens_tile) if keep else opens_tile) & (is_tail if tail else jnp.logical_not(is_tail))

                @pl.when(in_case)
                def _(keep=keep, tail=tail):
                    run(keep, not tail)


def _moe_ffn(x_sorted, rt, wg, wu, wd, index):
    d_ff = wg.shape[-1]
    rows = pl.BlockSpec((SLOT_TILE, D_MODEL), lambda j, it, ie, off, n: (it[j], 0))
    expert = lambda j, it, ie, off, n: (index, jnp.minimum(ie[j], N_EXPERTS - 1), 0, 0)
    w_in = pl.BlockSpec((None, None, D_MODEL, d_ff), expert)
    w_out = pl.BlockSpec((None, None, d_ff, D_MODEL), expert)
    return pl.pallas_call(
        _moe_kernel,
        grid_spec=pltpu.PrefetchScalarGridSpec(
            num_scalar_prefetch=4, grid=(N_ITEMS,),
            in_specs=[rows, w_in, w_in, w_out], out_specs=rows),
        out_shape=jax.ShapeDtypeStruct((N_SLOTS, D_MODEL), F32),
        compiler_params=_params(1),
        name="moe_ffn",
    )(rt["item_tile"], rt["item_expert"], rt["off"], rt["n_items"], x_sorted, wg, wu, wd)


def _combine_kernel(units_ref, local_ref, global_ref, loc_lane_ref, w_lane_ref, loc_row_ref, x_ref, mod_ref, g2_ref, b2_ref,
                    y_hbm, yp_ref, ys_ref, yl, sem):
    i = pl.program_id(0)
    cur = i % 2

    def gather(tile, b, act):
        make = lambda l, g, rows: pltpu.make_async_copy(y_hbm.at[pl.ds(g, rows), :], yl.at[b, pl.ds(l, rows), :], sem.at[b])
        _group_copies(tile, units_ref, local_ref, global_ref, make, act)

    @pl.when(i == 0)
    def _():
        yl[...] = jnp.zeros_like(yl)
        gather(0, 0, lambda c: c.start())

    @pl.when(i + 1 < pl.num_programs(0))
    def _():
        gather(i + 1, 1 - cur, lambda c: c.start())

    gather(i, cur, lambda c: c.wait())
    slot = lax.broadcasted_iota(jnp.int32, (LOCAL_ROWS, ROUTE_TILE), 0)
    gate = jnp.sum(jnp.where(slot == loc_lane_ref[0:1, :], w_lane_ref[0:1, :], 0.0)
                   + jnp.where(slot == loc_lane_ref[1:2, :], w_lane_ref[1:2, :], 0.0), axis=1, keepdims=True)
    y = yl[cur] * gate
    hi = y.astype(BF16)
    lo = (y - hi.astype(F32)).astype(BF16)
    slot_t = lax.broadcasted_iota(jnp.int32, (ROUTE_TILE, LOCAL_ROWS), 1)
    picks = jnp.where((slot_t == loc_row_ref[:, 0:1]) | (slot_t == loc_row_ref[:, 1:2]), 1.0, 0.0).astype(BF16)
    f = jnp.dot(picks, hi, preferred_element_type=F32) + jnp.dot(picks, lo, preferred_element_type=F32)
    x2 = _layer_norm(ALPHA * x_ref[...] + mod_ref[0, 5:6, :] * f, g2_ref[...], b2_ref[...])

    @pl.when(i < NP_TOK // ROUTE_TILE)
    def _():
        yp_ref[...] = x2

    @pl.when(i >= NP_TOK // ROUTE_TILE)
    def _():
        ys_ref[...] = x2


def _combine(y_sorted, rt, x1, mods, g2, b2):
    tm = ROUTE_TILE
    n_p = NP_TOK // tm
    vec = pl.BlockSpec((1, D_MODEL), lambda i, *_: (0, 0))
    lanes = pl.BlockSpec((2, tm), lambda i, *_: (0, i))
    return pl.pallas_call(
        _combine_kernel,
        grid_spec=pltpu.PrefetchScalarGridSpec(
            num_scalar_prefetch=3, grid=(N_ROUTE_TILES,),
            in_specs=[lanes, lanes, pl.BlockSpec((tm, 2), lambda i, *_: (i, 0)),
                      pl.BlockSpec((tm, D_MODEL), lambda i, *_: (i, 0)), _mod_spec(tm), vec, vec,
                      pl.BlockSpec(memory_space=pl.ANY)],
            out_specs=[pl.BlockSpec((tm, D_MODEL), lambda i, *_: (jnp.minimum(i, n_p - 1), 0)),
                       pl.BlockSpec((tm, D_MODEL), lambda i, *_: (jnp.maximum(i - n_p, 0), 0))],
            scratch_shapes=[pltpu.VMEM((2, LOCAL_ROWS, D_MODEL), F32), pltpu.SemaphoreType.DMA((2,))]),
        out_shape=[jax.ShapeDtypeStruct((NP_TOK, D_MODEL), F32), jax.ShapeDtypeStruct((NS_TOK, D_MODEL), F32)],
        compiler_params=_params(1),
        name="moe_combine",
    )(rt["units"], rt["local_start"], rt["global_start"], rt["local_by_lane"], rt["weight_by_lane"], rt["local_by_row"],
      x1, mods, g2.reshape(1, D_MODEL), b2.reshape(1, D_MODEL), y_sorted)


def kernel(x_prompt, x_sample, cache_k, cache_v, c, c_ctx, ln_in_g, ln_in_b, ada_w, ada_b, w_in, conv_w, conv_b, w_conv_out, lam_q1, lam_k1, lam_q2, lam_k2, subln_g, w_attn_out, w_out, ln1_g, ln1_b, ln2_g, ln2_b, ffn_w_gate, ffn_w_up, ffn_w_down, moe_router, moe_w_gate, moe_w_up, moe_w_down):
    assert DEPTH == 2
    cvec = jnp.concatenate([c, c_ctx[None, :], jnp.zeros((MOD_ROWS - DEC_BATCH - 1, D_MODEL), F32)], axis=0)
    mods = _ada(cvec, ada_w, ada_b).reshape(DEPTH, MOD_ROWS, 6, D_MODEL)
    tables = _rope_tables()

    x, h = _ln_in(x_prompt.reshape(NP_TOK, D_MODEL), x_sample.reshape(NS_TOK, D_MODEL), ln_in_g, ln_in_b, mods[0])
    caches = None
    for l in range(DEPTH):
        lam_init = 0.8 - 0.6 * math.exp(-0.3 * l)
        lam_vecs = jnp.stack([lam_q1[l], lam_k1[l], lam_q2[l], lam_k2[l]]).astype(F32)
        g_sub = subln_g[l].reshape(1, V_DIM)

        conv_y = _conv_branch(h, w_in, conv_w, conv_b, l)
        q, k, v, *caches = _qkv_prompt(h, w_in, l, caches)
        o_p = _attn_prompt(lam_vecs, g_sub, q, k, v, lam_init)
        q, k, v = _qkv_sample(h, w_in, tables, l)
        o_s = _attn_sample(lam_vecs, g_sub, q, k, v, cache_k, cache_v, l, lam_init)

        mix_w = (w_in, w_conv_out, w_attn_out, w_out, ln1_g, ln1_b, l)
        i = l // 2
        if l % 2 == 0:
            x1, h2 = _merge(h, conv_y, o_p, o_s, x, mods[l], *mix_w)
            x, h = _ffn(h2, ffn_w_gate.astype(BF16), ffn_w_up.astype(BF16), ffn_w_down.astype(BF16),
                        x1, mods[l], ln2_g[l], ln2_b[l], mods[l + 1], i)
        else:
            r_t = moe_router[i].T
            r_hi = r_t.astype(BF16)
            router = jnp.concatenate([r_hi, (r_t - r_hi.astype(F32)).astype(BF16)], axis=0)
            x1, h2, route, counts = _merge(h, conv_y, o_p, o_s, x, mods[l], *mix_w, router)
            rt = _routing_tables(route, counts)
            x_sorted = _dispatch(h2, rt)
            y_sorted = _moe_ffn(x_sorted, rt, moe_w_gate, moe_w_up, moe_w_down, i)
            y_p, y_s = _combine(y_sorted, rt, x1, mods[l], ln2_g[l], ln2_b[l])

    return (y_p.reshape(BATCH, SEQ, D_MODEL), y_s.reshape(DEC_BATCH, DEC_SEQ, D_MODEL), caches[0], caches[1])
```

```python
import functools
import math

import jax
import jax.numpy as jnp
from jax import lax
from jax.experimental import pallas as pl
from jax.experimental.pallas import tpu as pltpu

D_MODEL = 1024
BATCH = 32
SEQ = 256
DEPTH = 2
DEC_BATCH = 8
DEC_SEQ = 1024
PAST_LEN = 512
GRID_W = 64
D_CONV = 512
N_HEADS = 8
HEAD_DIM = 64
V_DIM = 2 * HEAD_DIM
ATTN_W = N_HEADS * V_DIM
AXIS_DIM = HEAD_DIM // 2
ROPE_BASE = 10000.0
N_EXPERTS = 8
ALPHA = (2 * DEPTH) ** 0.25
LN_EPS = 1e-5
QK_SCALE = HEAD_DIM ** -0.5 * math.log2(math.e)

NP_TOK = BATCH * SEQ
NS_TOK = DEC_BATCH * DEC_SEQ
N_TOK = NP_TOK + NS_TOK
MOD_ROWS = 16
CTX_ROW = DEC_BATCH
LANES = 128
MXU_TILE = 256
TOKEN_TILE = 2 * MXU_TILE
VMEM_LIMIT = 56 * 1024 * 1024

F32 = jnp.float32
BF16 = jnp.bfloat16
_NT = (((1,), (1,)), ((), ()))


def _params(n_axes, vmem=VMEM_LIMIT):
    return pltpu.CompilerParams(dimension_semantics=("arbitrary",) * n_axes, vmem_limit_bytes=vmem)


def _resident(shape):
    return pl.BlockSpec(shape, lambda *_: (0,) * len(shape), pipeline_mode=pl.Buffered(1))


def _weight(shape, *index):
    lead = len(index) - len(shape)
    return pl.BlockSpec((None,) * lead + tuple(shape), lambda *_: tuple(index), pipeline_mode=pl.Buffered(1))


def _dot(a, w):
    return jnp.dot(a.astype(w.dtype), w, preferred_element_type=F32)


def _mod_row(i, tm):
    n_p = NP_TOK // tm
    return jnp.where(i < n_p, CTX_ROW, (i - n_p) // (DEC_SEQ // tm))


def _mod_spec(tm):
    return pl.BlockSpec((1, 6, D_MODEL), lambda i, *_: (_mod_row(i, tm), 0, 0))


def _layer_norm(x, g, b):
    mu = jnp.mean(x, axis=-1, keepdims=True)
    xc = x - mu
    var = jnp.mean(xc * xc, axis=-1, keepdims=True)
    return xc * lax.rsqrt(var + LN_EPS) * g + b


def _ada_kernel(c_ref, w_ref, b_ref, o_ref):
    c = c_ref[...]
    a = (c * jax.nn.sigmoid(c)).astype(BF16)
    o_ref[0] = jnp.dot(a, w_ref[0].astype(BF16), preferred_element_type=F32) + b_ref[0]


def _ada(cvec, ada_w, ada_b):
    tn = 1024
    return pl.pallas_call(
        _ada_kernel,
        grid=(DEPTH, 6 * D_MODEL // tn),
        in_specs=[
            pl.BlockSpec((MOD_ROWS, D_MODEL), lambda l, j: (0, 0)),
            pl.BlockSpec((1, D_MODEL, tn), lambda l, j: (l, 0, j)),
            pl.BlockSpec((1, 1, tn), lambda l, j: (l, 0, j)),
        ],
        out_specs=pl.BlockSpec((1, MOD_ROWS, tn), lambda l, j: (l, 0, j)),
        out_shape=jax.ShapeDtypeStruct((DEPTH, MOD_ROWS, 6 * D_MODEL), F32),
        compiler_params=_params(2),
        name="ada",
    )(cvec, ada_w, ada_b.reshape(DEPTH, 1, 6 * D_MODEL))


def _ln_in_kernel(xp_ref, xs_ref, g_ref, b_ref, mod_ref, x_ref, h_ref, *, n_p):
    i = pl.program_id(0)

    def emit(src_ref):
        y = _layer_norm(src_ref[...], g_ref[...], b_ref[...])
        x_ref[...] = y
        h_ref[...] = (y * (1 + mod_ref[0, 1:2, :]) + mod_ref[0, 0:1, :]).astype(BF16)

    @pl.when(i < n_p)
    def _():
        emit(xp_ref)

    @pl.when(i >= n_p)
    def _():
        emit(xs_ref)


def _ln_in(xp, xs, g, b, mods):
    tm = 2 * TOKEN_TILE
    n_p = NP_TOK // tm
    tile = lambda i: (i, 0)
    return pl.pallas_call(
        functools.partial(_ln_in_kernel, n_p=n_p),
        grid=(N_TOK // tm,),
        in_specs=[
            pl.BlockSpec((tm, D_MODEL), lambda i: (jnp.minimum(i, n_p - 1), 0)),
            pl.BlockSpec((tm, D_MODEL), lambda i: (jnp.maximum(i - n_p, 0), 0)),
            pl.BlockSpec((1, D_MODEL), lambda i: (0, 0)),
            pl.BlockSpec((1, D_MODEL), lambda i: (0, 0)),
            _mod_spec(tm),
        ],
        out_specs=[pl.BlockSpec((tm, D_MODEL), tile), pl.BlockSpec((tm, D_MODEL), tile)],
        out_shape=[jax.ShapeDtypeStruct((N_TOK, D_MODEL), F32), jax.ShapeDtypeStruct((N_TOK, D_MODEL), BF16)],
        compiler_params=_params(1),
        name="ln_in",
    )(xp, xs, g.reshape(1, D_MODEL), b.reshape(1, D_MODEL), mods)


CONV_CHUNK = MXU_TILE


def _conv_kernel(h_ref, w_ref, cw_ref, cb_ref, y_ref, *, tm):
    i = pl.program_id(0)
    h = h_ref[...].astype(F32)
    seq = jnp.where(i < NP_TOK // tm, SEQ, DEC_SEQ)
    pos = lax.broadcasted_iota(jnp.int32, (tm, 1), 0) & (seq - 1)
    proj = lambda c: tuple(_dot(h, w_ref[:, part * D_CONV + c:part * D_CONV + c + CONV_CHUNK]) for part in range(3))
    pending = proj(0)
    for c in range(0, D_CONV, CONV_CHUNK):
        gate_b, gate_c, u = pending
        if c + CONV_CHUNK < D_CONV:
            pending = proj(c + CONV_CHUNK)
        cols = slice(c, c + CONV_CHUNK)
        pc = gate_c * u
        prev = jnp.where(pos == 0, 0.0, pltpu.roll(pc, 1, axis=0))
        nxt = jnp.where(pos == seq - 1, 0.0, pltpu.roll(pc, tm - 1, axis=0))
        conv = prev * cw_ref[0:1, cols] + pc * cw_ref[1:2, cols] + nxt * cw_ref[2:3, cols] + cb_ref[:, cols]
        y_ref[:, cols] = (gate_b * conv).astype(BF16)


def _conv_branch(h, w_in, conv_w, conv_b, layer):
    tm = DEC_SEQ
    return pl.pallas_call(
        functools.partial(_conv_kernel, tm=tm),
        grid=(N_TOK // tm,),
        in_specs=[
            pl.BlockSpec((tm, D_MODEL), lambda i: (i, 0)),
            _weight((D_MODEL, 3 * D_CONV), layer, 0, 0),
            pl.BlockSpec((None, 3, D_CONV), lambda i: (layer, 0, 0)),
            pl.BlockSpec((None, 1, D_CONV), lambda i: (layer, 0, 0)),
        ],
        out_specs=pl.BlockSpec((tm, D_CONV), lambda i: (i, 0)),
        out_shape=jax.ShapeDtypeStruct((N_TOK, D_CONV), BF16),
        compiler_params=_params(1),
        name="conv_branch",
    )(h, w_in, conv_w, conv_b.reshape(DEPTH, 1, D_CONV))


def _rope_tables():
    pos = jnp.arange(DEC_SEQ)
    row = (pos // GRID_W).astype(F32)
    col = (pos % GRID_W).astype(F32)
    inv_freq = ROPE_BASE ** (-jnp.arange(0, AXIS_DIM, 2, dtype=F32) / AXIS_DIM)
    ang_r = row[:, None] * inv_freq
    ang_c = col[:, None] * inv_freq
    lane = jnp.arange(V_DIM)
    sub = lane % HEAD_DIM
    ang = jnp.where((sub < AXIS_DIM)[None, :], ang_r[:, lane % (AXIS_DIM // 2)], ang_c[:, lane % (AXIS_DIM // 2)])
    first = ((lane % AXIS_DIM) < AXIS_DIM // 2)[None, :]
    cos, sin = jnp.cos(ang), jnp.sin(ang)
    return cos, jnp.where(first, -sin, 0.0), jnp.where(first, 0.0, sin)


QKV_CHUNK = MXU_TILE


def _project_chunks(h, w_refs, emit):
    per_ref = w_refs[0].shape[-1] // QKV_CHUNK
    n = per_ref * len(w_refs)
    h = h.astype(w_refs[0].dtype)
    proj = lambda c: _dot(h, w_refs[c // per_ref][:, (c % per_ref) * QKV_CHUNK:(c % per_ref + 1) * QKV_CHUNK])
    y_next = proj(0)
    for c in range(n):
        y = y_next
        if c + 1 < n:
            y_next = proj(c + 1)
        emit(c, y)


def _qkv_prompt_kernel(h_ref, wa_ref, wb_ref, *rest, tm, layer, first):
    q_ref, k_ref, v_ref, kc_all, vc_all = rest[-5:]
    kc_ref, vc_ref = (kc_all.at[:, layer], vc_all.at[:, layer]) if first else (kc_all, vc_all)
    per_part = ATTN_W // QKV_CHUNK

    def emit(c, y):
        part, cols = c // per_part, slice((c % per_part) * QKV_CHUNK, (c % per_part + 1) * QKV_CHUNK)
        if part == 0:
            q_ref[:, cols] = (y * QK_SCALE).astype(BF16)
            return
        act_ref, cache_ref = (k_ref, kc_ref) if part == 1 else (v_ref, vc_ref)
        act_ref[:, cols] = y.astype(BF16)
        for s in range(tm // SEQ):
            for j in range(QKV_CHUNK // V_DIM):
                cache_ref[s, (c % per_part) * (QKV_CHUNK // V_DIM) + j] = y[s * SEQ:(s + 1) * SEQ, j * V_DIM:(j + 1) * V_DIM]

    _project_chunks(h_ref[...], (wa_ref, wb_ref), emit)
    if first:
        for other in range(DEPTH):
            if other != layer:
                kc_all[:, other] = jnp.zeros((tm // SEQ, N_HEADS, SEQ, V_DIM), F32)
                vc_all[:, other] = jnp.zeros((tm // SEQ, N_HEADS, SEQ, V_DIM), F32)


def _qkv_sample_kernel(h_ref, wa_ref, wb_ref, cos_ref, sup_ref, sdn_ref, q_ref, k_ref, v_ref):
    cos, s_up, s_dn = cos_ref[...], sup_ref[...], sdn_ref[...]
    per_part = ATTN_W // QKV_CHUNK

    def rope(x):
        return x * cos + pltpu.roll(x, V_DIM - AXIS_DIM // 2, axis=1) * s_up + pltpu.roll(x, AXIS_DIM // 2, axis=1) * s_dn

    def emit(c, y):
        part, c0 = c // per_part, (c % per_part) * QKV_CHUNK
        if part == 2:
            v_ref[:, c0:c0 + QKV_CHUNK] = y.astype(BF16)
            return
        for j in range(QKV_CHUNK // V_DIM):
            r = rope(y[:, j * V_DIM:(j + 1) * V_DIM])
            if part == 0:
                q_ref[:, c0 + j * V_DIM:c0 + (j + 1) * V_DIM] = (r * QK_SCALE).astype(BF16)
            else:
                k_ref[:, c0 + j * V_DIM:c0 + (j + 1) * V_DIM] = r.astype(BF16)

    _project_chunks(h_ref[...], (wa_ref, wb_ref), emit)


def _qkv_weights(layer):
    assert 2 * 3 * D_CONV == 3 * ATTN_W
    return [_weight((D_MODEL, 3 * D_CONV), layer, 0, 1), _weight((D_MODEL, 3 * D_CONV), layer, 0, 2)]


def _qkv_prompt(h, w_in, layer, caches):
    tm = TOKEN_TILE
    tile = pl.BlockSpec((tm, ATTN_W), lambda i: (i, 0))
    first = caches is None
    if first:
        cache = pl.BlockSpec((tm // SEQ, DEPTH, N_HEADS, SEQ, V_DIM), lambda i: (i, 0, 0, 0, 0))
    else:
        cache = pl.BlockSpec((tm // SEQ, None, N_HEADS, SEQ, V_DIM), lambda i: (i, layer, 0, 0, 0))
    act = jax.ShapeDtypeStruct((NP_TOK, ATTN_W), BF16)
    ctx = jax.ShapeDtypeStruct((BATCH, DEPTH, N_HEADS, SEQ, V_DIM), F32)
    in_specs = [pl.BlockSpec((tm, D_MODEL), lambda i: (i, 0)), *_qkv_weights(layer)]
    args = [h, w_in, w_in]
    aliases = {}
    if not first:
        aliases = {len(args): 3, len(args) + 1: 4}
        in_specs += [pl.BlockSpec(memory_space=pl.ANY)] * 2
        args += list(caches)
    return pl.pallas_call(
        functools.partial(_qkv_prompt_kernel, tm=tm, layer=layer, first=first),
        grid=(NP_TOK // tm,),
        in_specs=in_specs,
        out_specs=[tile, tile, tile, cache, cache],
        out_shape=[act, act, act, ctx, ctx],
        input_output_aliases=aliases,
        compiler_params=_params(1),
        name="qkv_prompt",
    )(*args)


def _qkv_sample(h, w_in, tables, layer):
    tm = TOKEN_TILE
    first_tile = NP_TOK // tm
    tile = pl.BlockSpec((tm, ATTN_W), lambda i: (i, 0))
    tab = pl.BlockSpec((tm, V_DIM), lambda i: (i % (DEC_SEQ // tm), 0))
    act = jax.ShapeDtypeStruct((NS_TOK, ATTN_W), BF16)
    return pl.pallas_call(
        _qkv_sample_kernel,
        grid=(NS_TOK // tm,),
        in_specs=[pl.BlockSpec((tm, D_MODEL), lambda i: (first_tile + i, 0)), *_qkv_weights(layer), tab, tab, tab],
        out_specs=[tile, tile, tile],
        out_shape=[act, act, act],
        compiler_params=_params(1),
        name="qkv_sample",
    )(h, w_in, w_in, *tables)


def _lam(lam_ref, lam_init):
    a = jnp.sum(lam_ref[0:1, :] * lam_ref[1:2, :], axis=1, keepdims=True)
    b = jnp.sum(lam_ref[2:3, :] * lam_ref[3:4, :], axis=1, keepdims=True)
    return jnp.exp(a) - jnp.exp(b) + lam_init


def _scores(q, k):
    lo = lax.broadcasted_iota(jnp.int32, (1, V_DIM), 1) < HEAD_DIM
    zero = jnp.zeros_like(q)
    qq = jnp.concatenate([jnp.where(lo, q, zero), jnp.where(lo, zero, q)], axis=0)
    return lax.dot_general(qq, k, _NT, preferred_element_type=F32)


def _diff_probs(s, lam):
    tq = s.shape[0] // 2
    e = jnp.exp2(s - jnp.max(s, axis=-1, keepdims=True))
    l = jnp.sum(e, axis=-1, keepdims=True)
    p = e[:tq] - e[tq:] * (lam * l[:tq] / l[tq:])
    return p.astype(BF16), 1.0 / l[:tq]


def _head_out(p, inv_l1, v, g, lam_init):
    o = jnp.dot(p, v, preferred_element_type=F32) * inv_l1
    ms = jnp.mean(o * o, axis=-1, keepdims=True)
    return (o * lax.rsqrt(ms + LN_EPS) * g * (1 - lam_init)).astype(BF16)


def _diff_attn_tiles(tiles, lam, g, lam_init):
    n = len(tiles)
    scores = lambda t: _scores(tiles[t][0](), tiles[t][1]())
    s = {0: scores(0)}
    probs = {}
    for t in range(-1, n):
        if t + 2 < n:
            s[t + 2] = scores(t + 2)
        if t + 1 < n:
            if t + 1 not in s:
                s[t + 1] = scores(t + 1)
            probs[t + 1] = _diff_probs(s.pop(t + 1), lam)
        if t >= 0:
            p, inv_l1 = probs.pop(t)
            tiles[t][3](_head_out(p, inv_l1, tiles[t][2](), g, lam_init))


ATTN_SEQS_PER_STEP = 4
ATTN_HEADS_PER_STEP = 4


def _attn_prompt_kernel(lam_ref, g_ref, q_ref, k_ref, v_ref, o_ref, *, lam_init):
    def tile(s, hd):
        rows, cols = slice(s * SEQ, (s + 1) * SEQ), slice(hd * V_DIM, (hd + 1) * V_DIM)

        def store(o):
            o_ref[rows, cols] = o

        return (lambda: q_ref[rows, cols], lambda: k_ref[rows, cols], lambda: v_ref[rows, cols], store)

    tiles = [tile(s, hd) for s in range(ATTN_SEQS_PER_STEP) for hd in range(N_HEADS)]
    _diff_attn_tiles(tiles, _lam(lam_ref, lam_init), g_ref[...], lam_init)


def _attn_prompt(lam_vecs, g, q, k, v, lam_init):
    blk = pl.BlockSpec((ATTN_SEQS_PER_STEP * SEQ, ATTN_W), lambda b: (b, 0))
    return pl.pallas_call(
        functools.partial(_attn_prompt_kernel, lam_init=lam_init),
        grid=(BATCH // ATTN_SEQS_PER_STEP,),
        in_specs=[pl.BlockSpec((4, HEAD_DIM), lambda b: (0, 0)), pl.BlockSpec((1, V_DIM), lambda b: (0, 0)), blk, blk, blk],
        out_specs=blk,
        out_shape=jax.ShapeDtypeStruct((NP_TOK, ATTN_W), BF16),
        compiler_params=_params(1),
        name="attn_prompt",
    )(lam_vecs, g, q, k, v)


def _attn_sample_kernel(lam_ref, g_ref, q_ref, kn_ref, vn_ref, kc_ref, vc_ref, o_ref, k_s, v_s, *, lam_init, tq):
    for hd in range(ATTN_HEADS_PER_STEP):
        cols = slice(hd * V_DIM, (hd + 1) * V_DIM)
        k_s[hd, 0:PAST_LEN, :] = kc_ref[hd].astype(BF16)
        k_s[hd, PAST_LEN:, :] = kn_ref[:, cols]
        v_s[hd, 0:PAST_LEN, :] = vc_ref[hd].astype(BF16)
        v_s[hd, PAST_LEN:, :] = vn_ref[:, cols]

    def tile(hd, t):
        rows, cols = slice(t * tq, (t + 1) * tq), slice(hd * V_DIM, (hd + 1) * V_DIM)

        def store(o):
            o_ref[rows, cols] = o

        return (lambda: q_ref[rows, cols], lambda: k_s[hd], lambda: v_s[hd], store)

    tiles = [tile(hd, t) for hd in range(ATTN_HEADS_PER_STEP) for t in range(DEC_SEQ // tq)]
    _diff_attn_tiles(tiles, _lam(lam_ref, lam_init), g_ref[...], lam_init)


def _attn_sample(lam_vecs, g, q, k, v, cache_k, cache_v, layer, lam_init):
    tq = 128
    hps = ATTN_HEADS_PER_STEP
    new = pl.BlockSpec((DEC_SEQ, hps * V_DIM), lambda b, h: (b, h))
    past = pl.BlockSpec((None, None, hps, PAST_LEN, V_DIM), lambda b, h: (b, layer, h, 0, 0))
    kv_all = pltpu.VMEM((hps, PAST_LEN + DEC_SEQ, V_DIM), BF16)
    return pl.pallas_call(
        functools.partial(_attn_sample_kernel, lam_init=lam_init, tq=tq),
        grid=(DEC_BATCH, N_HEADS // hps),
        in_specs=[pl.BlockSpec((4, HEAD_DIM), lambda b, h: (0, 0)), pl.BlockSpec((1, V_DIM), lambda b, h: (0, 0)),
                  new, new, new, past, past],
        out_specs=pl.BlockSpec((DEC_SEQ, hps * V_DIM), lambda b, h: (b, h)),
        out_shape=jax.ShapeDtypeStruct((NS_TOK, ATTN_W), BF16),
        scratch_shapes=[kv_all, kv_all],
        compiler_params=_params(2),
        name="attn_sample",
    )(lam_vecs, g, q, k, v, cache_k, cache_v)


MERGE_ROWS = MXU_TILE


def _route(h2, router_ref, cnt_ref):
    tm = h2.shape[0]
    hi = h2.astype(BF16)
    lo = (h2 - hi.astype(F32)).astype(BF16)
    a = lax.dot_general(router_ref[...], hi, _NT, preferred_element_type=F32)
    b = lax.dot_general(router_ref[0:N_EXPERTS, :], lo, _NT, preferred_element_type=F32)
    logits = a[:N_EXPERTS] + a[N_EXPERTS:] + b
    e = jnp.exp(logits - jnp.max(logits, axis=0, keepdims=True))
    p = e / jnp.sum(e, axis=0, keepdims=True)
    row = lax.broadcasted_iota(jnp.int32, p.shape, 0)
    v1 = jnp.max(p, axis=0, keepdims=True)
    i1 = jnp.min(jnp.where(p == v1, row, N_EXPERTS), axis=0, keepdims=True)
    p2 = jnp.where(row == i1, -1.0, p)
    v2 = jnp.max(p2, axis=0, keepdims=True)
    i2 = jnp.min(jnp.where(p2 == v2, row, N_EXPERTS), axis=0, keepdims=True)
    den = v1 + v2
    pick1, pick2 = row == i1, row == i2
    picked = jnp.where(pick1 | pick2, 1.0, 0.0)
    before = lax.broadcasted_iota(jnp.int32, (tm, tm), 0) < lax.broadcasted_iota(jnp.int32, (tm, tm), 1)
    ahead = jnp.dot(picked.astype(BF16), jnp.where(before, 1.0, 0.0).astype(BF16), preferred_element_type=F32)
    ahead = ahead + cnt_ref[:, 0:1]
    rank1 = jnp.sum(jnp.where(pick1, ahead, 0.0), axis=0, keepdims=True)
    rank2 = jnp.sum(jnp.where(pick2, ahead, 0.0), axis=0, keepdims=True)
    cnt_ref[...] = cnt_ref[...] + jnp.sum(picked, axis=1, keepdims=True)
    zero = jnp.zeros_like(v1)
    return jnp.concatenate([i1.astype(F32), i2.astype(F32), rank1, rank2, v1 / den, v2 / den, zero, zero], axis=0)


def _merge_kernel(h_ref, cy_ref, op_ref, os_ref, x_ref, mod_ref, wga_ref, wgb_ref, wc_ref, wa_ref, wo_ref, g1_ref, b1_ref,
                  *rest, tm, routed):
    if routed:
        router_ref, x1_ref, h2_ref, route_ref, count_ref, cnt_ref = rest
    else:
        x1_ref, h2_ref = rest
    i = pl.program_id(0)
    if routed:
        cnt_ref[...] = jnp.zeros_like(cnt_ref)

    def mix(rows):
        h = h_ref[rows, :]
        g = jnp.concatenate([_dot(h, wga_ref[...]), _dot(h, wgb_ref[...])], axis=1)
        y_conv = _dot(cy_ref[rows, :], wc_ref[...])
        o = jnp.where(i < NP_TOK // tm, op_ref[rows, :], os_ref[rows, :])
        y_attn = _dot(o, wa_ref[...])
        merged = jax.nn.sigmoid(g[:, :D_MODEL]) * y_conv + jax.nn.sigmoid(g[:, D_MODEL:]) * y_attn
        return _dot(merged.astype(BF16), wo_ref[...])

    def finish(rows, m):
        x1 = _layer_norm(ALPHA * x_ref[rows, :] + mod_ref[0, 2:3, :] * m, g1_ref[...], b1_ref[...])
        x1_ref[rows, :] = x1
        h2 = x1 * (1 + mod_ref[0, 4:5, :]) + mod_ref[0, 3:4, :]
        h2_ref[rows, :] = h2.astype(h2_ref.dtype)
        if routed:
            route_ref[:, rows] = _route(h2, router_ref, cnt_ref)

    blocks = [slice(r, r + MERGE_ROWS) for r in range(0, tm, MERGE_ROWS)]
    m_next = mix(blocks[0])
    for b, rows in enumerate(blocks):
        m = m_next
        if b + 1 < len(blocks):
            m_next = mix(blocks[b + 1])
        finish(rows, m)
    if routed:
        count_ref[...] = cnt_ref[...]
        group = jnp.ceil(cnt_ref[:, 0:1] * (1.0 / GROUP_ALIGN)) * GROUP_ALIGN
        expert = lax.broadcasted_iota(jnp.int32, (N_EXPERTS, 1), 0)
        start = jnp.zeros_like(group)
        for e in range(N_EXPERTS - 1):
            start = start + jnp.where(expert > e, group[e:e + 1, :], 0.0)
        expert_f = expert.astype(F32)
        for k in range(2):
            mine = route_ref[k:k + 1, :] == expert_f
            route_ref[6 + k:7 + k, :] = jnp.sum(jnp.where(mine, start, 0.0), axis=0, keepdims=True) + route_ref[2 + k:3 + k, :]


def _merge(h, conv_y, o_p, o_s, x, mods, w_in, w_conv_out, w_attn_out, w_out, g1, b1, layer, router=None):
    tm = TOKEN_TILE
    n_p = NP_TOK // tm
    routed = router is not None
    tile = lambda w: pl.BlockSpec((tm, w), lambda i: (i, 0))
    vec = pl.BlockSpec((None, 1, D_MODEL), lambda i: (layer, 0, 0))
    gate0 = 3 * D_CONV + 3 * ATTN_W
    rest = 2 * D_MODEL - 3 * D_CONV
    assert gate0 % (3 * D_CONV) == 0 and (gate0 + 3 * D_CONV) % rest == 0
    in_specs = [
        tile(D_MODEL), tile(D_CONV),
        pl.BlockSpec((tm, ATTN_W), lambda i: (jnp.minimum(i, n_p - 1), 0)),
        pl.BlockSpec((tm, ATTN_W), lambda i: (jnp.maximum(i - n_p, 0), 0)),
        tile(D_MODEL), _mod_spec(tm),
        _weight((D_MODEL, 3 * D_CONV), layer, 0, gate0 // (3 * D_CONV)),
        _weight((D_MODEL, rest), layer, 0, (gate0 + 3 * D_CONV) // rest),
        _weight((D_CONV, D_MODEL), layer, 0, 0), _weight((ATTN_W, D_MODEL), layer, 0, 0),
        _weight((D_MODEL, D_MODEL), layer, 0, 0), vec, vec,
    ]
    args = [h, conv_y, o_p, o_s, x, mods, w_in, w_in, w_conv_out, w_attn_out, w_out,
            g1.reshape(DEPTH, 1, D_MODEL), b1.reshape(DEPTH, 1, D_MODEL)]
    out_specs = [tile(D_MODEL), tile(D_MODEL)]
    out_shape = [jax.ShapeDtypeStruct((N_TOK, D_MODEL), F32), jax.ShapeDtypeStruct((N_TOK, D_MODEL), BF16)]
    scratch = []
    if routed:
        in_specs.append(_resident((2 * N_EXPERTS, D_MODEL)))
        args.append(router)
        out_specs += [pl.BlockSpec((N_EXPERTS, tm), lambda i: (0, i)), pl.BlockSpec((N_EXPERTS, LANES), lambda i: (i, 0))]
        out_shape += [jax.ShapeDtypeStruct((N_EXPERTS, N_TOK), F32),
                      jax.ShapeDtypeStruct((N_TOK // tm * N_EXPERTS, LANES), F32)]
        scratch = [pltpu.VMEM((N_EXPERTS, LANES), F32)]
    return pl.pallas_call(
        functools.partial(_merge_kernel, tm=tm, routed=routed),
        grid=(N_TOK // tm,),
        in_specs=in_specs, out_specs=out_specs, out_shape=out_shape,
        scratch_shapes=scratch,
        compiler_params=_params(1),
        name="merge_routed" if routed else "merge",
    )(*args)


FF_CHUNK = MXU_TILE


def _swiglu(x, wg_ref, wu_ref, wd_ref):
    d_ff = wg_ref.shape[-1]
    bounds = [(c, min(c + FF_CHUNK, d_ff)) for c in range(0, d_ff, FF_CHUNK)]
    x = x.astype(wg_ref.dtype)

    def up(lo, hi):
        return (jnp.dot(x, wg_ref[:, lo:hi], preferred_element_type=F32),
                jnp.dot(x, wu_ref[:, lo:hi], preferred_element_type=F32))

    f = None
    pending = up(*bounds[0])
    for c, (lo, hi) in enumerate(bounds):
        a, u = pending
        if c + 1 < len(bounds):
            pending = up(*bounds[c + 1])
        hid = (a * jax.nn.sigmoid(a) * u).astype(BF16).astype(wd_ref.dtype)
        d = jnp.dot(hid, wd_ref[lo:hi, :], preferred_element_type=F32)
        f = d if f is None else f + d
    return f


def _ffn_kernel(h_ref, wg_ref, wu_ref, wd_ref, x_ref, mod_ref, g2_ref, b2_ref, nmod_ref, x2_ref, hn_ref):
    def finish(rows, f):
        x2 = _layer_norm(ALPHA * x_ref[rows, :] + mod_ref[0, 5:6, :] * f, g2_ref[...], b2_ref[...])
        x2_ref[rows, :] = x2
        hn_ref[rows, :] = (x2 * (1 + nmod_ref[0, 1:2, :]) + nmod_ref[0, 0:1, :]).astype(BF16)

    blocks = [slice(r, r + MERGE_ROWS) for r in range(0, h_ref.shape[0], MERGE_ROWS)]
    f_next = _swiglu(h_ref[blocks[0], :], wg_ref, wu_ref, wd_ref)
    for b, rows in enumerate(blocks):
        f = f_next
        if b + 1 < len(blocks):
            f_next = _swiglu(h_ref[blocks[b + 1], :], wg_ref, wu_ref, wd_ref)
        finish(rows, f)


def _ffn(h2, wg, wu, wd, x1, mods, g2, b2, next_mods, index):
    tm = TOKEN_TILE
    d_ff = wg.shape[-1]
    tile = lambda w: pl.BlockSpec((tm, w), lambda i: (i, 0))
    vec = pl.BlockSpec((1, D_MODEL), lambda i: (0, 0))
    return pl.pallas_call(
        _ffn_kernel,
        grid=(N_TOK // tm,),
        in_specs=[tile(D_MODEL), _weight((D_MODEL, d_ff), index, 0, 0), _weight((D_MODEL, d_ff), index, 0, 0),
                  _weight((d_ff, D_MODEL), index, 0, 0), tile(D_MODEL), _mod_spec(tm), vec, vec, _mod_spec(tm)],
        out_specs=[tile(D_MODEL), tile(D_MODEL)],
        out_shape=[jax.ShapeDtypeStruct((N_TOK, D_MODEL), F32), jax.ShapeDtypeStruct((N_TOK, D_MODEL), BF16)],
        compiler_params=_params(1),
        name="ffn",
    )(h2, wg, wu, wd, x1, mods, g2.reshape(1, D_MODEL), b2.reshape(1, D_MODEL), next_mods)


N_PAIRS = 2 * N_TOK
ROUTE_TILE = TOKEN_TILE
N_ROUTE_TILES = N_TOK // ROUTE_TILE
GROUP_ALIGN = 16
LOCAL_ROWS = 2 * ROUTE_TILE + N_EXPERTS * GROUP_ALIGN
N_SLOTS = N_PAIRS + N_ROUTE_TILES * N_EXPERTS * GROUP_ALIGN
SLOT_TILE = TOKEN_TILE
N_SLOT_TILES = N_SLOTS // SLOT_TILE
N_ITEMS = N_SLOT_TILES + N_EXPERTS
COPY_BITS = (ROUTE_TILE // GROUP_ALIGN).bit_length()
TAIL_EXPERT = N_EXPERTS

def _routing_tables(route, counts):
    n = counts.reshape(N_ROUTE_TILES, N_EXPERTS, LANES)[:, :, 0].astype(jnp.int32)
    g = (n + GROUP_ALIGN - 1) // GROUP_ALIGN * GROUP_ALIGN
    local_start = jnp.cumsum(g, axis=1) - g
    region = jnp.concatenate([jnp.zeros((1,), jnp.int32), jnp.cumsum(jnp.sum(g, axis=0))])
    global_start = region[None, :-1] + jnp.cumsum(g, axis=0) - g
    off = jnp.concatenate([region, jnp.full((1,), N_SLOTS, jnp.int32)])
    local = route[6:8].astype(jnp.int32)
    t0 = jnp.arange(N_SLOT_TILES, dtype=jnp.int32)[:, None] * SLOT_TILE
    live = jnp.maximum(off[None, :-1], t0) < jnp.minimum(off[None, 1:], t0 + SLOT_TILE)
    n_items = jnp.sum(live).astype(jnp.int32)
    order = jnp.nonzero(live.reshape(-1), size=N_ITEMS, fill_value=0)[0].astype(jnp.int32)
    order = jnp.where(jnp.arange(N_ITEMS) < n_items, order, order[n_items - 1])
    return dict(
        units=(g // GROUP_ALIGN).reshape(-1), local_start=local_start.reshape(-1), global_start=global_start.reshape(-1),
        used=region[-1:], off=off, n_items=n_items.reshape(1),
        item_tile=order // (N_EXPERTS + 1), item_expert=order % (N_EXPERTS + 1),
        local_by_lane=local, local_by_row=local.T, weight_by_lane=route[4:6])


def _for_each_chunk(units, fn):
    for b in range(COPY_BITS):
        @pl.when(((units >> b) & 1) == 1)
        def _():
            fn(pl.multiple_of((units & ((1 << b) - 1)) * GROUP_ALIGN, GROUP_ALIGN), GROUP_ALIGN << b)


def _group_copies(tile, units_ref, local_ref, global_ref, make, act):
    for e in range(N_EXPERTS):
        g = tile * N_EXPERTS + e

        def chunk(off, rows, g=g):
            act(make(pl.multiple_of(local_ref[g] + off, GROUP_ALIGN), pl.multiple_of(global_ref[g] + off, GROUP_ALIGN), rows))

        _for_each_chunk(units_ref[g], chunk)


def _dispatch_kernel(units_ref, local_ref, global_ref, used_ref, loc_ref, h_ref, xs_hbm, xl, zeros, sem):
    i = pl.program_id(0)
    cur = i % 2
    last = pl.num_programs(0) - 1
    slot = lax.broadcasted_iota(jnp.int32, (LOCAL_ROWS, ROUTE_TILE), 0)
    one_hot = jnp.where((slot == loc_ref[0:1, :]) | (slot == loc_ref[1:2, :]), 1.0, 0.0).astype(BF16)
    xl[cur] = jnp.dot(one_hot, h_ref[...], preferred_element_type=F32).astype(BF16)

    def copies(tile, b, act):
        make = lambda l, g, rows: pltpu.make_async_copy(xl.at[b, pl.ds(l, rows), :], xs_hbm.at[pl.ds(g, rows), :], sem.at[b])
        _group_copies(tile, units_ref, local_ref, global_ref, make, act)

    copies(i, cur, lambda c: c.start())

    @pl.when(i > 0)
    def _():
        copies(i - 1, 1 - cur, lambda c: c.wait())

    @pl.when(i == last)
    def _():
        copies(i, cur, lambda c: c.wait())

    @pl.when(i == last)
    def _():
        zeros[...] = jnp.zeros_like(zeros)
        used = used_ref[0]
        tail = (N_SLOTS - used) // GROUP_ALIGN
        small, n_big = tail % (SLOT_TILE // GROUP_ALIGN), tail // (SLOT_TILE // GROUP_ALIGN)
        big0 = used + small * GROUP_ALIGN

        fill_sem = sem.at[0]

        def fill(act):
            _for_each_chunk(small, lambda off, rows: act(pltpu.make_async_copy(
                zeros.at[pl.ds(0, rows), :], xs_hbm.at[pl.ds(pl.multiple_of(used + off, GROUP_ALIGN), rows), :], fill_sem)))
            for k in range((N_SLOTS - N_PAIRS) // SLOT_TILE):
                @pl.when(k < n_big)
                def _():
                    act(pltpu.make_async_copy(
                        zeros, xs_hbm.at[pl.ds(pl.multiple_of(big0 + k * SLOT_TILE, GROUP_ALIGN), SLOT_TILE), :], fill_sem))

        fill(lambda c: c.start())
        fill(lambda c: c.wait())


def _dispatch(h2, rt):
    return pl.pallas_call(
        _dispatch_kernel,
        grid_spec=pltpu.PrefetchScalarGridSpec(
            num_scalar_prefetch=4, grid=(N_ROUTE_TILES,),
            in_specs=[pl.BlockSpec((2, ROUTE_TILE), lambda i, *_: (0, i)),
                      pl.BlockSpec((ROUTE_TILE, D_MODEL), lambda i, *_: (i, 0))],
            out_specs=pl.BlockSpec(memory_space=pl.ANY),
            scratch_shapes=[pltpu.VMEM((2, LOCAL_ROWS, D_MODEL), BF16), pltpu.VMEM((SLOT_TILE, D_MODEL), BF16),
                            pltpu.SemaphoreType.DMA((2,))]),
        out_shape=jax.ShapeDtypeStruct((N_SLOTS, D_MODEL), BF16),
        compiler_params=_params(1),
        name="moe_dispatch",
    )(rt["units"], rt["local_start"], rt["global_start"], rt["used"], rt["local_by_lane"], h2)


def _moe_kernel(tile_ref, expert_ref, off_ref, n_ref, x_ref, wg_ref, wu_ref, wd_ref, y_ref):
    j = pl.program_id(0)

    @pl.when(j < n_ref[0])
    def _():
        t, e = tile_ref[j], expert_ref[j]
        lo, hi = off_ref[e] - t * SLOT_TILE, off_ref[e + 1] - t * SLOT_TILE
        opens_tile = (j == 0) | (tile_ref[jnp.maximum(j - 1, 0)] != t)
        is_tail = e == TAIL_EXPERT

        def run(keep_other_rows, compute):
            blocks = [slice(r, r + MERGE_ROWS) for r in range(0, SLOT_TILE, MERGE_ROWS)]
            zero_rows = jnp.zeros((MERGE_ROWS, D_MODEL), F32)
            expert = lambda rows: _swiglu(x_ref[rows, :], wg_ref, wu_ref, wd_ref) if compute else zero_rows
            f_next = expert(blocks[0])
            for b, rows in enumerate(blocks):
                f = f_next
                if b + 1 < len(blocks):
                    f_next = expert(blocks[b + 1])
                row = lax.broadcasted_iota(jnp.int32, (MERGE_ROWS, 1), 0) + rows.start
                mine = (row >= lo) & (row < hi)
                y_ref[rows, :] = jnp.where(mine, f, y_ref[rows, :] if keep_other_rows else 0.0)

        for keep in (False, True):
            for tail in (False, True):
                in_case = (jnp.logical_not(opens_tile) if keep else opens_tile) & (is_tail if tail else jnp.logical_not(is_tail))

                @pl.when(in_case)
                def _(keep=keep, tail=tail):
                    run(keep, not tail)


def _moe_ffn(x_sorted, rt, wg, wu, wd, index):
    d_ff = wg.shape[-1]
    rows = pl.BlockSpec((SLOT_TILE, D_MODEL), lambda j, it, ie, off, n: (it[j], 0))
    expert = lambda j, it, ie, off, n: (index, jnp.minimum(ie[j], N_EXPERTS - 1), 0, 0)
    w_in = pl.BlockSpec((None, None, D_MODEL, d_ff), expert)
    w_out = pl.BlockSpec((None, None, d_ff, D_MODEL), expert)
    return pl.pallas_call(
        _moe_kernel,
        grid_spec=pltpu.PrefetchScalarGridSpec(
            num_scalar_prefetch=4, grid=(N_ITEMS,),
            in_specs=[rows, w_in, w_in, w_out], out_specs=rows),
        out_shape=jax.ShapeDtypeStruct((N_SLOTS, D_MODEL), F32),
        compiler_params=_params(1),
        name="moe_ffn",
    )(rt["item_tile"], rt["item_expert"], rt["off"], rt["n_items"], x_sorted, wg, wu, wd)


def _combine_kernel(units_ref, local_ref, global_ref, loc_lane_ref, w_lane_ref, loc_row_ref, x_ref, mod_ref, g2_ref, b2_ref,
                    y_hbm, yp_ref, ys_ref, yl, sem):
    i = pl.program_id(0)
    cur = i % 2

    def gather(tile, b, act):
        make = lambda l, g, rows: pltpu.make_async_copy(y_hbm.at[pl.ds(g, rows), :], yl.at[b, pl.ds(l, rows), :], sem.at[b])
        _group_copies(tile, units_ref, local_ref, global_ref, make, act)

    @pl.when(i == 0)
    def _():
        yl[...] = jnp.zeros_like(yl)
        gather(0, 0, lambda c: c.start())

    @pl.when(i + 1 < pl.num_programs(0))
    def _():
        gather(i + 1, 1 - cur, lambda c: c.start())

    gather(i, cur, lambda c: c.wait())
    slot = lax.broadcasted_iota(jnp.int32, (LOCAL_ROWS, ROUTE_TILE), 0)
    gate = jnp.sum(jnp.where(slot == loc_lane_ref[0:1, :], w_lane_ref[0:1, :], 0.0)
                   + jnp.where(slot == loc_lane_ref[1:2, :], w_lane_ref[1:2, :], 0.0), axis=1, keepdims=True)
    y = yl[cur] * gate
    hi = y.astype(BF16)
    lo = (y - hi.astype(F32)).astype(BF16)
    slot_t = lax.broadcasted_iota(jnp.int32, (ROUTE_TILE, LOCAL_ROWS), 1)
    picks = jnp.where((slot_t == loc_row_ref[:, 0:1]) | (slot_t == loc_row_ref[:, 1:2]), 1.0, 0.0).astype(BF16)
    f = jnp.dot(picks, hi, preferred_element_type=F32) + jnp.dot(picks, lo, preferred_element_type=F32)
    x2 = _layer_norm(ALPHA * x_ref[...] + mod_ref[0, 5:6, :] * f, g2_ref[...], b2_ref[...])

    @pl.when(i < NP_TOK // ROUTE_TILE)
    def _():
        yp_ref[...] = x2

    @pl.when(i >= NP_TOK // ROUTE_TILE)
    def _():
        ys_ref[...] = x2


def _combine(y_sorted, rt, x1, mods, g2, b2):
    tm = ROUTE_TILE
    n_p = NP_TOK // tm
    vec = pl.BlockSpec((1, D_MODEL), lambda i, *_: (0, 0))
    lanes = pl.BlockSpec((2, tm), lambda i, *_: (0, i))
    return pl.pallas_call(
        _combine_kernel,
        grid_spec=pltpu.PrefetchScalarGridSpec(
            num_scalar_prefetch=3, grid=(N_ROUTE_TILES,),
            in_specs=[lanes, lanes, pl.BlockSpec((tm, 2), lambda i, *_: (i, 0)),
                      pl.BlockSpec((tm, D_MODEL), lambda i, *_: (i, 0)), _mod_spec(tm), vec, vec,
                      pl.BlockSpec(memory_space=pl.ANY)],
            out_specs=[pl.BlockSpec((tm, D_MODEL), lambda i, *_: (jnp.minimum(i, n_p - 1), 0)),
                       pl.BlockSpec((tm, D_MODEL), lambda i, *_: (jnp.maximum(i - n_p, 0), 0))],
            scratch_shapes=[pltpu.VMEM((2, LOCAL_ROWS, D_MODEL), F32), pltpu.SemaphoreType.DMA((2,))]),
        out_shape=[jax.ShapeDtypeStruct((NP_TOK, D_MODEL), F32), jax.ShapeDtypeStruct((NS_TOK, D_MODEL), F32)],
        compiler_params=_params(1),
        name="moe_combine",
    )(rt["units"], rt["local_start"], rt["global_start"], rt["local_by_lane"], rt["weight_by_lane"], rt["local_by_row"],
      x1, mods, g2.reshape(1, D_MODEL), b2.reshape(1, D_MODEL), y_sorted)


def kernel(x_prompt, x_sample, cache_k, cache_v, c, c_ctx, ln_in_g, ln_in_b, ada_w, ada_b, w_in, conv_w, conv_b, w_conv_out, lam_q1, lam_k1, lam_q2, lam_k2, subln_g, w_attn_out, w_out, ln1_g, ln1_b, ln2_g, ln2_b, ffn_w_gate, ffn_w_up, ffn_w_down, moe_router, moe_w_gate, moe_w_up, moe_w_down):
    assert DEPTH == 2
    cvec = jnp.concatenate([c, c_ctx[None, :], jnp.zeros((MOD_ROWS - DEC_BATCH - 1, D_MODEL), F32)], axis=0)
    mods = _ada(cvec, ada_w, ada_b).reshape(DEPTH, MOD_ROWS, 6, D_MODEL)
    tables = _rope_tables()

    x, h = _ln_in(x_prompt.reshape(NP_TOK, D_MODEL), x_sample.reshape(NS_TOK, D_MODEL), ln_in_g, ln_in_b, mods[0])
    caches = None
    for l in range(DEPTH):
        lam_init = 0.8 - 0.6 * math.exp(-0.3 * l)
        lam_vecs = jnp.stack([lam_q1[l], lam_k1[l], lam_q2[l], lam_k2[l]]).astype(F32)
        g_sub = subln_g[l].reshape(1, V_DIM)

        conv_y = _conv_branch(h, w_in, conv_w, conv_b, l)
        q, k, v, *caches = _qkv_prompt(h, w_in, l, caches)
        o_p = _attn_prompt(lam_vecs, g_sub, q, k, v, lam_init)
        q, k, v = _qkv_sample(h, w_in, tables, l)
        o_s = _attn_sample(lam_vecs, g_sub, q, k, v, cache_k, cache_v, l, lam_init)

        mix_w = (w_in, w_conv_out, w_attn_out, w_out, ln1_g, ln1_b, l)
        i = l // 2
        if l % 2 == 0:
            x1, h2 = _merge(h, conv_y, o_p, o_s, x, mods[l], *mix_w)
            x, h = _ffn(h2, ffn_w_gate.astype(BF16), ffn_w_up.astype(BF16), ffn_w_down.astype(BF16),
                        x1, mods[l], ln2_g[l], ln2_b[l], mods[l + 1], i)
        else:
            r_t = moe_router[i].T
            r_hi = r_t.astype(BF16)
            router = jnp.concatenate([r_hi, (r_t - r_hi.astype(F32)).astype(BF16)], axis=0)
            x1, h2, route, counts = _merge(h, conv_y, o_p, o_s, x, mods[l], *mix_w, router)
            rt = _routing_tables(route, counts)
            x_sorted = _dispatch(h2, rt)
            y_sorted = _moe_ffn(x_sorted, rt, moe_w_gate, moe_w_up, moe_w_down, i)
            y_p, y_s = _combine(y_sorted, rt, x1, mods[l], ln2_g[l], ln2_b[l])

    return (y_p.reshape(BATCH, SEQ, D_MODEL), y_s.reshape(DEC_BATCH, DEC_SEQ, D_MODEL), caches[0], caches[1])
```

```python
import functools
import math

import jax
import jax.numpy as jnp
from jax import lax
from jax.experimental import pallas as pl
from jax.experimental.pallas import tpu as pltpu

D_MODEL = 1024
BATCH = 32
SEQ = 256
DEPTH = 2
DEC_BATCH = 8
DEC_SEQ = 1024
PAST_LEN = 512
GRID_W = 64
D_CONV = 512
N_HEADS = 8
HEAD_DIM = 64
V_DIM = 2 * HEAD_DIM
ATTN_W = N_HEADS * V_DIM
AXIS_DIM = HEAD_DIM // 2
ROPE_BASE = 10000.0
N_EXPERTS = 8
ALPHA = (2 * DEPTH) ** 0.25
LN_EPS = 1e-5
QK_SCALE = HEAD_DIM ** -0.5 * math.log2(math.e)

NP_TOK = BATCH * SEQ
NS_TOK = DEC_BATCH * DEC_SEQ
N_TOK = NP_TOK + NS_TOK
MOD_ROWS = 16
CTX_ROW = DEC_BATCH
LANES = 128
MXU_TILE = 256
TOKEN_TILE = 2 * MXU_TILE
MIB = 1024 * 1024
VMEM_LIMIT = 56 * MIB

F32 = jnp.float32
BF16 = jnp.bfloat16
_NT = (((1,), (1,)), ((), ()))


def _params(n_axes, vmem=VMEM_LIMIT):
    return pltpu.CompilerParams(dimension_semantics=("arbitrary",) * n_axes, vmem_limit_bytes=vmem)


def _resident(shape):
    return pl.BlockSpec(shape, lambda *_: (0,) * len(shape), pipeline_mode=pl.Buffered(1))


def _weight(shape, *index):
    lead = len(index) - len(shape)
    return pl.BlockSpec((None,) * lead + tuple(shape), lambda *_: tuple(index), pipeline_mode=pl.Buffered(1))


def _dot(a, w):
    return jnp.dot(a.astype(w.dtype), w, preferred_element_type=F32)


def _mod_row(i, tm):
    n_p = NP_TOK // tm
    return jnp.where(i < n_p, CTX_ROW, (i - n_p) // (DEC_SEQ // tm))


def _mod_spec(tm):
    return pl.BlockSpec((1, 6, D_MODEL), lambda i, *_: (_mod_row(i, tm), 0, 0))


def _layer_norm(x, g, b):
    mu = jnp.mean(x, axis=-1, keepdims=True)
    xc = x - mu
    var = jnp.mean(xc * xc, axis=-1, keepdims=True)
    return xc * lax.rsqrt(var + LN_EPS) * g + b


def _ada_kernel(c_ref, w_ref, b_ref, o_ref):
    c = c_ref[...]
    a = (c * jax.nn.sigmoid(c)).astype(BF16)
    o_ref[0] = jnp.dot(a, w_ref[0].astype(BF16), preferred_element_type=F32) + b_ref[0]


def _ada(cvec, ada_w, ada_b):
    tn = 1024
    return pl.pallas_call(
        _ada_kernel,
        grid=(DEPTH, 6 * D_MODEL // tn),
        in_specs=[
            pl.BlockSpec((MOD_ROWS, D_MODEL), lambda l, j: (0, 0)),
            pl.BlockSpec((1, D_MODEL, tn), lambda l, j: (l, 0, j)),
            pl.BlockSpec((1, 1, tn), lambda l, j: (l, 0, j)),
        ],
        out_specs=pl.BlockSpec((1, MOD_ROWS, tn), lambda l, j: (l, 0, j)),
        out_shape=jax.ShapeDtypeStruct((DEPTH, MOD_ROWS, 6 * D_MODEL), F32),
        compiler_params=_params(2),
        name="ada",
    )(cvec, ada_w, ada_b.reshape(DEPTH, 1, 6 * D_MODEL))


def _ln_in_kernel(xp_ref, xs_ref, g_ref, b_ref, mod_ref, x_ref, h_ref, *, n_p):
    i = pl.program_id(0)

    def emit(src_ref):
        y = _layer_norm(src_ref[...], g_ref[...], b_ref[...])
        x_ref[...] = y
        h_ref[...] = (y * (1 + mod_ref[0, 1:2, :]) + mod_ref[0, 0:1, :]).astype(BF16)

    @pl.when(i < n_p)
    def _():
        emit(xp_ref)

    @pl.when(i >= n_p)
    def _():
        emit(xs_ref)


def _ln_in(xp, xs, g, b, mods):
    tm = 2 * TOKEN_TILE
    n_p = NP_TOK // tm
    tile = lambda i: (i, 0)
    return pl.pallas_call(
        functools.partial(_ln_in_kernel, n_p=n_p),
        grid=(N_TOK // tm,),
        in_specs=[
            pl.BlockSpec((tm, D_MODEL), lambda i: (jnp.minimum(i, n_p - 1), 0)),
            pl.BlockSpec((tm, D_MODEL), lambda i: (jnp.maximum(i - n_p, 0), 0)),
            pl.BlockSpec((1, D_MODEL), lambda i: (0, 0)),
            pl.BlockSpec((1, D_MODEL), lambda i: (0, 0)),
            _mod_spec(tm),
        ],
        out_specs=[pl.BlockSpec((tm, D_MODEL), tile), pl.BlockSpec((tm, D_MODEL), tile)],
        out_shape=[jax.ShapeDtypeStruct((N_TOK, D_MODEL), F32), jax.ShapeDtypeStruct((N_TOK, D_MODEL), BF16)],
        compiler_params=_params(1, 36 * MIB),
        name="ln_in",
    )(xp, xs, g.reshape(1, D_MODEL), b.reshape(1, D_MODEL), mods)


CONV_CHUNK = MXU_TILE


def _conv_kernel(h_ref, w_ref, cw_ref, cb_ref, y_ref, *, tm):
    i = pl.program_id(0)
    h = h_ref[...].astype(F32)
    seq = jnp.where(i < NP_TOK // tm, SEQ, DEC_SEQ)
    pos = lax.broadcasted_iota(jnp.int32, (tm, 1), 0) & (seq - 1)
    proj = lambda c: tuple(_dot(h, w_ref[:, part * D_CONV + c:part * D_CONV + c + CONV_CHUNK]) for part in range(3))
    pending = proj(0)
    for c in range(0, D_CONV, CONV_CHUNK):
        gate_b, gate_c, u = pending
        if c + CONV_CHUNK < D_CONV:
            pending = proj(c + CONV_CHUNK)
        cols = slice(c, c + CONV_CHUNK)
        pc = gate_c * u
        prev = jnp.where(pos == 0, 0.0, pltpu.roll(pc, 1, axis=0))
        nxt = jnp.where(pos == seq - 1, 0.0, pltpu.roll(pc, tm - 1, axis=0))
        conv = prev * cw_ref[0:1, cols] + pc * cw_ref[1:2, cols] + nxt * cw_ref[2:3, cols] + cb_ref[:, cols]
        y_ref[:, cols] = (gate_b * conv).astype(BF16)


def _conv_branch(h, w_in, conv_w, conv_b, layer):
    tm = DEC_SEQ
    return pl.pallas_call(
        functools.partial(_conv_kernel, tm=tm),
        grid=(N_TOK // tm,),
        in_specs=[
            pl.BlockSpec((tm, D_MODEL), lambda i: (i, 0)),
            _weight((D_MODEL, 3 * D_CONV), layer, 0, 0),
            pl.BlockSpec((None, 3, D_CONV), lambda i: (layer, 0, 0)),
            pl.BlockSpec((None, 1, D_CONV), lambda i: (layer, 0, 0)),
        ],
        out_specs=pl.BlockSpec((tm, D_CONV), lambda i: (i, 0)),
        out_shape=jax.ShapeDtypeStruct((N_TOK, D_CONV), BF16),
        compiler_params=_params(1, 24 * MIB),
        name="conv_branch",
    )(h, w_in, conv_w, conv_b.reshape(DEPTH, 1, D_CONV))


def _rope_tables():
    pos = jnp.arange(DEC_SEQ)
    row = (pos // GRID_W).astype(F32)
    col = (pos % GRID_W).astype(F32)
    inv_freq = ROPE_BASE ** (-jnp.arange(0, AXIS_DIM, 2, dtype=F32) / AXIS_DIM)
    ang_r = row[:, None] * inv_freq
    ang_c = col[:, None] * inv_freq
    lane = jnp.arange(V_DIM)
    sub = lane % HEAD_DIM
    ang = jnp.where((sub < AXIS_DIM)[None, :], ang_r[:, lane % (AXIS_DIM // 2)], ang_c[:, lane % (AXIS_DIM // 2)])
    first = ((lane % AXIS_DIM) < AXIS_DIM // 2)[None, :]
    cos, sin = jnp.cos(ang), jnp.sin(ang)
    return cos, jnp.where(first, -sin, 0.0), jnp.where(first, 0.0, sin)


QKV_CHUNK = MXU_TILE


def _project_chunks(h, w_refs, emit):
    per_ref = w_refs[0].shape[-1] // QKV_CHUNK
    n = per_ref * len(w_refs)
    h = h.astype(w_refs[0].dtype)
    proj = lambda c: _dot(h, w_refs[c // per_ref][:, (c % per_ref) * QKV_CHUNK:(c % per_ref + 1) * QKV_CHUNK])
    y_next = proj(0)
    for c in range(n):
        y = y_next
        if c + 1 < n:
            y_next = proj(c + 1)
        emit(c, y)


def _qkv_prompt_kernel(h_ref, wa_ref, wb_ref, *rest, tm, layer, first):
    q_ref, k_ref, v_ref, kc_all, vc_all = rest[-5:]
    kc_ref, vc_ref = (kc_all.at[:, layer], vc_all.at[:, layer]) if first else (kc_all, vc_all)
    per_part = ATTN_W // QKV_CHUNK

    def emit(c, y):
        part, cols = c // per_part, slice((c % per_part) * QKV_CHUNK, (c % per_part + 1) * QKV_CHUNK)
        if part == 0:
            q_ref[:, cols] = (y * QK_SCALE).astype(BF16)
            return
        act_ref, cache_ref = (k_ref, kc_ref) if part == 1 else (v_ref, vc_ref)
        act_ref[:, cols] = y.astype(BF16)
        for s in range(tm // SEQ):
            for j in range(QKV_CHUNK // V_DIM):
                cache_ref[s, (c % per_part) * (QKV_CHUNK // V_DIM) + j] = y[s * SEQ:(s + 1) * SEQ, j * V_DIM:(j + 1) * V_DIM]

    _project_chunks(h_ref[...], (wa_ref, wb_ref), emit)
    if first:
        for other in range(DEPTH):
            if other != layer:
                kc_all[:, other] = jnp.zeros((tm // SEQ, N_HEADS, SEQ, V_DIM), F32)
                vc_all[:, other] = jnp.zeros((tm // SEQ, N_HEADS, SEQ, V_DIM), F32)


def _qkv_sample_kernel(h_ref, wa_ref, wb_ref, cos_ref, sup_ref, sdn_ref, q_ref, k_ref, v_ref):
    cos, s_up, s_dn = cos_ref[...], sup_ref[...], sdn_ref[...]
    per_part = ATTN_W // QKV_CHUNK

    def rope(x):
        return x * cos + pltpu.roll(x, V_DIM - AXIS_DIM // 2, axis=1) * s_up + pltpu.roll(x, AXIS_DIM // 2, axis=1) * s_dn

    def emit(c, y):
        part, c0 = c // per_part, (c % per_part) * QKV_CHUNK
        if part == 2:
            v_ref[:, c0:c0 + QKV_CHUNK] = y.astype(BF16)
            return
        for j in range(QKV_CHUNK // V_DIM):
            r = rope(y[:, j * V_DIM:(j + 1) * V_DIM])
            if part == 0:
                q_ref[:, c0 + j * V_DIM:c0 + (j + 1) * V_DIM] = (r * QK_SCALE).astype(BF16)
            else:
                k_ref[:, c0 + j * V_DIM:c0 + (j + 1) * V_DIM] = r.astype(BF16)

    _project_chunks(h_ref[...], (wa_ref, wb_ref), emit)


def _qkv_weights(layer):
    assert 2 * 3 * D_CONV == 3 * ATTN_W
    return [_weight((D_MODEL, 3 * D_CONV), layer, 0, 1), _weight((D_MODEL, 3 * D_CONV), layer, 0, 2)]


def _qkv_prompt(h, w_in, layer, caches):
    tm = TOKEN_TILE
    tile = pl.BlockSpec((tm, ATTN_W), lambda i: (i, 0))
    first = caches is None
    if first:
        cache = pl.BlockSpec((tm // SEQ, DEPTH, N_HEADS, SEQ, V_DIM), lambda i: (i, 0, 0, 0, 0))
    else:
        cache = pl.BlockSpec((tm // SEQ, None, N_HEADS, SEQ, V_DIM), lambda i: (i, layer, 0, 0, 0))
    act = jax.ShapeDtypeStruct((NP_TOK, ATTN_W), BF16)
    ctx = jax.ShapeDtypeStruct((BATCH, DEPTH, N_HEADS, SEQ, V_DIM), F32)
    in_specs = [pl.BlockSpec((tm, D_MODEL), lambda i: (i, 0)), *_qkv_weights(layer)]
    args = [h, w_in, w_in]
    aliases = {}
    if not first:
        aliases = {len(args): 3, len(args) + 1: 4}
        in_specs += [pl.BlockSpec(memory_space=pl.ANY)] * 2
        args += list(caches)
    return pl.pallas_call(
        functools.partial(_qkv_prompt_kernel, tm=tm, layer=layer, first=first),
        grid=(NP_TOK // tm,),
        in_specs=in_specs,
        out_specs=[tile, tile, tile, cache, cache],
        out_shape=[act, act, act, ctx, ctx],
        input_output_aliases=aliases,
        compiler_params=_params(1, 44 * MIB),
        name="qkv_prompt",
    )(*args)


def _qkv_sample(h, w_in, tables, layer):
    tm = TOKEN_TILE
    first_tile = NP_TOK // tm
    tile = pl.BlockSpec((tm, ATTN_W), lambda i: (i, 0))
    tab = pl.BlockSpec((tm, V_DIM), lambda i: (i % (DEC_SEQ // tm), 0))
    act = jax.ShapeDtypeStruct((NS_TOK, ATTN_W), BF16)
    return pl.pallas_call(
        _qkv_sample_kernel,
        grid=(NS_TOK // tm,),
        in_specs=[pl.BlockSpec((tm, D_MODEL), lambda i: (first_tile + i, 0)), *_qkv_weights(layer), tab, tab, tab],
        out_specs=[tile, tile, tile],
        out_shape=[act, act, act],
        compiler_params=_params(1, 32 * MIB),
        name="qkv_sample",
    )(h, w_in, w_in, *tables)


def _lam(lam_ref, lam_init):
    a = jnp.sum(lam_ref[0:1, :] * lam_ref[1:2, :], axis=1, keepdims=True)
    b = jnp.sum(lam_ref[2:3, :] * lam_ref[3:4, :], axis=1, keepdims=True)
    return jnp.exp(a) - jnp.exp(b) + lam_init


def _scores(q, k):
    lo = lax.broadcasted_iota(jnp.int32, (1, V_DIM), 1) < HEAD_DIM
    zero = jnp.zeros_like(q)
    qq = jnp.concatenate([jnp.where(lo, q, zero), jnp.where(lo, zero, q)], axis=0)
    return lax.dot_general(qq, k, _NT, preferred_element_type=F32)


def _diff_probs(s, lam):
    tq = s.shape[0] // 2
    e = jnp.exp2(s - jnp.max(s, axis=-1, keepdims=True))
    l = jnp.sum(e, axis=-1, keepdims=True)
    p = e[:tq] - e[tq:] * (lam * l[:tq] / l[tq:])
    return p.astype(BF16), 1.0 / l[:tq]


def _head_out(p, inv_l1, v, g, lam_init):
    o = jnp.dot(p, v, preferred_element_type=F32) * inv_l1
    ms = jnp.mean(o * o, axis=-1, keepdims=True)
    return (o * lax.rsqrt(ms + LN_EPS) * g * (1 - lam_init)).astype(BF16)


def _diff_attn_tiles(tiles, lam, g, lam_init):
    n = len(tiles)
    scores = lambda t: _scores(tiles[t][0](), tiles[t][1]())
    s = {0: scores(0)}
    probs = {}
    for t in range(-1, n):
        if t + 2 < n:
            s[t + 2] = scores(t + 2)
        if t + 1 < n:
            if t + 1 not in s:
                s[t + 1] = scores(t + 1)
            probs[t + 1] = _diff_probs(s.pop(t + 1), lam)
        if t >= 0:
            p, inv_l1 = probs.pop(t)
            tiles[t][3](_head_out(p, inv_l1, tiles[t][2](), g, lam_init))


ATTN_SEQS_PER_STEP = 4
ATTN_HEADS_PER_STEP = 4


def _attn_prompt_kernel(lam_ref, g_ref, q_ref, k_ref, v_ref, o_ref, *, lam_init):
    def tile(s, hd):
        rows, cols = slice(s * SEQ, (s + 1) * SEQ), slice(hd * V_DIM, (hd + 1) * V_DIM)

        def store(o):
            o_ref[rows, cols] = o

        return (lambda: q_ref[rows, cols], lambda: k_ref[rows, cols], lambda: v_ref[rows, cols], store)

    tiles = [tile(s, hd) for s in range(ATTN_SEQS_PER_STEP) for hd in range(N_HEADS)]
    _diff_attn_tiles(tiles, _lam(lam_ref, lam_init), g_ref[...], lam_init)


def _attn_prompt(lam_vecs, g, q, k, v, lam_init):
    blk = pl.BlockSpec((ATTN_SEQS_PER_STEP * SEQ, ATTN_W), lambda b: (b, 0))
    return pl.pallas_call(
        functools.partial(_attn_prompt_kernel, lam_init=lam_init),
        grid=(BATCH // ATTN_SEQS_PER_STEP,),
        in_specs=[pl.BlockSpec((4, HEAD_DIM), lambda b: (0, 0)), pl.BlockSpec((1, V_DIM), lambda b: (0, 0)), blk, blk, blk],
        out_specs=blk,
        out_shape=jax.ShapeDtypeStruct((NP_TOK, ATTN_W), BF16),
        compiler_params=_params(1, 24 * MIB),
        name="attn_prompt",
    )(lam_vecs, g, q, k, v)


def _attn_sample_kernel(lam_ref, g_ref, q_ref, kn_ref, vn_ref, kc_ref, vc_ref, o_ref, k_s, v_s, *, lam_init, tq):
    for hd in range(ATTN_HEADS_PER_STEP):
        cols = slice(hd * V_DIM, (hd + 1) * V_DIM)
        k_s[hd, 0:PAST_LEN, :] = kc_ref[hd].astype(BF16)
        k_s[hd, PAST_LEN:, :] = kn_ref[:, cols]
        v_s[hd, 0:PAST_LEN, :] = vc_ref[hd].astype(BF16)
        v_s[hd, PAST_LEN:, :] = vn_ref[:, cols]

    def tile(hd, t):
        rows, cols = slice(t * tq, (t + 1) * tq), slice(hd * V_DIM, (hd + 1) * V_DIM)

        def store(o):
            o_ref[rows, cols] = o

        return (lambda: q_ref[rows, cols], lambda: k_s[hd], lambda: v_s[hd], store)

    tiles = [tile(hd, t) for hd in range(ATTN_HEADS_PER_STEP) for t in range(DEC_SEQ // tq)]
    _diff_attn_tiles(tiles, _lam(lam_ref, lam_init), g_ref[...], lam_init)


def _attn_sample(lam_vecs, g, q, k, v, cache_k, cache_v, layer, lam_init):
    tq = 128
    hps = ATTN_HEADS_PER_STEP
    new = pl.BlockSpec((DEC_SEQ, hps * V_DIM), lambda b, h: (b, h))
    past = pl.BlockSpec((None, None, hps, PAST_LEN, V_DIM), lambda b, h: (b, layer, h, 0, 0))
    kv_all = pltpu.VMEM((hps, PAST_LEN + DEC_SEQ, V_DIM), BF16)
    return pl.pallas_call(
        functools.partial(_attn_sample_kernel, lam_init=lam_init, tq=tq),
        grid=(DEC_BATCH, N_HEADS // hps),
        in_specs=[pl.BlockSpec((4, HEAD_DIM), lambda b, h: (0, 0)), pl.BlockSpec((1, V_DIM), lambda b, h: (0, 0)),
                  new, new, new, past, past],
        out_specs=pl.BlockSpec((DEC_SEQ, hps * V_DIM), lambda b, h: (b, h)),
        out_shape=jax.ShapeDtypeStruct((NS_TOK, ATTN_W), BF16),
        scratch_shapes=[kv_all, kv_all],
        compiler_params=_params(2, 24 * MIB),
        name="attn_sample",
    )(lam_vecs, g, q, k, v, cache_k, cache_v)


MERGE_ROWS = MXU_TILE


def _route(h2, router_ref, cnt_ref):
    tm = h2.shape[0]
    hi = h2.astype(BF16)
    lo = (h2 - hi.astype(F32)).astype(BF16)
    a = lax.dot_general(router_ref[...], hi, _NT, preferred_element_type=F32)
    b = lax.dot_general(router_ref[0:N_EXPERTS, :], lo, _NT, preferred_element_type=F32)
    logits = a[:N_EXPERTS] + a[N_EXPERTS:] + b
    e = jnp.exp(logits - jnp.max(logits, axis=0, keepdims=True))
    p = e / jnp.sum(e, axis=0, keepdims=True)
    row = lax.broadcasted_iota(jnp.int32, p.shape, 0)
    v1 = jnp.max(p, axis=0, keepdims=True)
    i1 = jnp.min(jnp.where(p == v1, row, N_EXPERTS), axis=0, keepdims=True)
    p2 = jnp.where(row == i1, -1.0, p)
    v2 = jnp.max(p2, axis=0, keepdims=True)
    i2 = jnp.min(jnp.where(p2 == v2, row, N_EXPERTS), axis=0, keepdims=True)
    den = v1 + v2
    pick1, pick2 = row == i1, row == i2
    picked = jnp.where(pick1 | pick2, 1.0, 0.0)
    before = lax.broadcasted_iota(jnp.int32, (tm, tm), 0) < lax.broadcasted_iota(jnp.int32, (tm, tm), 1)
    ahead = jnp.dot(picked.astype(BF16), jnp.where(before, 1.0, 0.0).astype(BF16), preferred_element_type=F32)
    ahead = ahead + cnt_ref[:, 0:1]
    rank1 = jnp.sum(jnp.where(pick1, ahead, 0.0), axis=0, keepdims=True)
    rank2 = jnp.sum(jnp.where(pick2, ahead, 0.0), axis=0, keepdims=True)
    cnt_ref[...] = cnt_ref[...] + jnp.sum(picked, axis=1, keepdims=True)
    zero = jnp.zeros_like(v1)
    return jnp.concatenate([i1.astype(F32), i2.astype(F32), rank1, rank2, v1 / den, v2 / den, zero, zero], axis=0)


def _merge_kernel(h_ref, cy_ref, op_ref, os_ref, x_ref, mod_ref, wga_ref, wgb_ref, wc_ref, wa_ref, wo_ref, g1_ref, b1_ref,
                  *rest, tm, routed):
    if routed:
        router_ref, x1_ref, h2_ref, route_ref, count_ref, cnt_ref = rest
    else:
        x1_ref, h2_ref = rest
    i = pl.program_id(0)
    if routed:
        cnt_ref[...] = jnp.zeros_like(cnt_ref)

    def mix(rows):
        h = h_ref[rows, :]
        g = jnp.concatenate([_dot(h, wga_ref[...]), _dot(h, wgb_ref[...])], axis=1)
        y_conv = _dot(cy_ref[rows, :], wc_ref[...])
        o = jnp.where(i < NP_TOK // tm, op_ref[rows, :], os_ref[rows, :])
        y_attn = _dot(o, wa_ref[...])
        merged = jax.nn.sigmoid(g[:, :D_MODEL]) * y_conv + jax.nn.sigmoid(g[:, D_MODEL:]) * y_attn
        return _dot(merged.astype(BF16), wo_ref[...])

    def finish(rows, m):
        x1 = _layer_norm(ALPHA * x_ref[rows, :] + mod_ref[0, 2:3, :] * m, g1_ref[...], b1_ref[...])
        x1_ref[rows, :] = x1
        h2 = x1 * (1 + mod_ref[0, 4:5, :]) + mod_ref[0, 3:4, :]
        h2_ref[rows, :] = h2.astype(h2_ref.dtype)
        if routed:
            route_ref[:, rows] = _route(h2, router_ref, cnt_ref)

    blocks = [slice(r, r + MERGE_ROWS) for r in range(0, tm, MERGE_ROWS)]
    m_next = mix(blocks[0])
    for b, rows in enumerate(blocks):
        m = m_next
        if b + 1 < len(blocks):
            m_next = mix(blocks[b + 1])
        finish(rows, m)
    if routed:
        count_ref[...] = cnt_ref[...]
        group = jnp.ceil(cnt_ref[:, 0:1] * (1.0 / GROUP_ALIGN)) * GROUP_ALIGN
        expert = lax.broadcasted_iota(jnp.int32, (N_EXPERTS, 1), 0)
        start = jnp.zeros_like(group)
        for e in range(N_EXPERTS - 1):
            start = start + jnp.where(expert > e, group[e:e + 1, :], 0.0)
        expert_f = expert.astype(F32)
        for k in range(2):
            mine = route_ref[k:k + 1, :] == expert_f
            route_ref[6 + k:7 + k, :] = jnp.sum(jnp.where(mine, start, 0.0), axis=0, keepdims=True) + route_ref[2 + k:3 + k, :]


def _merge(h, conv_y, o_p, o_s, x, mods, w_in, w_conv_out, w_attn_out, w_out, g1, b1, layer, router=None):
    tm = TOKEN_TILE
    n_p = NP_TOK // tm
    routed = router is not None
    tile = lambda w: pl.BlockSpec((tm, w), lambda i: (i, 0))
    vec = pl.BlockSpec((None, 1, D_MODEL), lambda i: (layer, 0, 0))
    gate0 = 3 * D_CONV + 3 * ATTN_W
    rest = 2 * D_MODEL - 3 * D_CONV
    assert gate0 % (3 * D_CONV) == 0 and (gate0 + 3 * D_CONV) % rest == 0
    in_specs = [
        tile(D_MODEL), tile(D_CONV),
        pl.BlockSpec((tm, ATTN_W), lambda i: (jnp.minimum(i, n_p - 1), 0)),
        pl.BlockSpec((tm, ATTN_W), lambda i: (jnp.maximum(i - n_p, 0), 0)),
        tile(D_MODEL), _mod_spec(tm),
        _weight((D_MODEL, 3 * D_CONV), layer, 0, gate0 // (3 * D_CONV)),
        _weight((D_MODEL, rest), layer, 0, (gate0 + 3 * D_CONV) // rest),
        _weight((D_CONV, D_MODEL), layer, 0, 0), _weight((ATTN_W, D_MODEL), layer, 0, 0),
        _weight((D_MODEL, D_MODEL), layer, 0, 0), vec, vec,
    ]
    args = [h, conv_y, o_p, o_s, x, mods, w_in, w_in, w_conv_out, w_attn_out, w_out,
            g1.reshape(DEPTH, 1, D_MODEL), b1.reshape(DEPTH, 1, D_MODEL)]
    out_specs = [tile(D_MODEL), tile(D_MODEL)]
    out_shape = [jax.ShapeDtypeStruct((N_TOK, D_MODEL), F32), jax.ShapeDtypeStruct((N_TOK, D_MODEL), BF16)]
    scratch = []
    if routed:
        in_specs.append(_resident((2 * N_EXPERTS, D_MODEL)))
        args.append(router)
        out_specs += [pl.BlockSpec((N_EXPERTS, tm), lambda i: (0, i)), pl.BlockSpec((N_EXPERTS, LANES), lambda i: (i, 0))]
        out_shape += [jax.ShapeDtypeStruct((N_EXPERTS, N_TOK), F32),
                      jax.ShapeDtypeStruct((N_TOK // tm * N_EXPERTS, LANES), F32)]
        scratch = [pltpu.VMEM((N_EXPERTS, LANES), F32)]
    return pl.pallas_call(
        functools.partial(_merge_kernel, tm=tm, routed=routed),
        grid=(N_TOK // tm,),
        in_specs=in_specs, out_specs=out_specs, out_shape=out_shape,
        scratch_shapes=scratch,
        compiler_params=_params(1),
        name="merge_routed" if routed else "merge",
    )(*args)


FF_CHUNK = MXU_TILE


def _swiglu(x, wg_ref, wu_ref, wd_ref):
    d_ff = wg_ref.shape[-1]
    bounds = [(c, min(c + FF_CHUNK, d_ff)) for c in range(0, d_ff, FF_CHUNK)]
    x = x.astype(wg_ref.dtype)

    def up(lo, hi):
        return (jnp.dot(x, wg_ref[:, lo:hi], preferred_element_type=F32),
                jnp.dot(x, wu_ref[:, lo:hi], preferred_element_type=F32))

    f = None
    pending = up(*bounds[0])
    for c, (lo, hi) in enumerate(bounds):
        a, u = pending
        if c + 1 < len(bounds):
            pending = up(*bounds[c + 1])
        hid = (a * jax.nn.sigmoid(a) * u).astype(BF16).astype(wd_ref.dtype)
        d = jnp.dot(hid, wd_ref[lo:hi, :], preferred_element_type=F32)
        f = d if f is None else f + d
    return f


def _ffn_kernel(h_ref, wg_ref, wu_ref, wd_ref, x_ref, mod_ref, g2_ref, b2_ref, nmod_ref, x2_ref, hn_ref):
    def finish(rows, f):
        x2 = _layer_norm(ALPHA * x_ref[rows, :] + mod_ref[0, 5:6, :] * f, g2_ref[...], b2_ref[...])
        x2_ref[rows, :] = x2
        hn_ref[rows, :] = (x2 * (1 + nmod_ref[0, 1:2, :]) + nmod_ref[0, 0:1, :]).astype(BF16)

    blocks = [slice(r, r + MERGE_ROWS) for r in range(0, h_ref.shape[0], MERGE_ROWS)]
    f_next = _swiglu(h_ref[blocks[0], :], wg_ref, wu_ref, wd_ref)
    for b, rows in enumerate(blocks):
        f = f_next
        if b + 1 < len(blocks):
            f_next = _swiglu(h_ref[blocks[b + 1], :], wg_ref, wu_ref, wd_ref)
        finish(rows, f)


def _ffn(h2, wg, wu, wd, x1, mods, g2, b2, next_mods, index):
    tm = TOKEN_TILE
    d_ff = wg.shape[-1]
    tile = lambda w: pl.BlockSpec((tm, w), lambda i: (i, 0))
    vec = pl.BlockSpec((1, D_MODEL), lambda i: (0, 0))
    return pl.pallas_call(
        _ffn_kernel,
        grid=(N_TOK // tm,),
        in_specs=[tile(D_MODEL), _weight((D_MODEL, d_ff), index, 0, 0), _weight((D_MODEL, d_ff), index, 0, 0),
                  _weight((d_ff, D_MODEL), index, 0, 0), tile(D_MODEL), _mod_spec(tm), vec, vec, _mod_spec(tm)],
        out_specs=[tile(D_MODEL), tile(D_MODEL)],
        out_shape=[jax.ShapeDtypeStruct((N_TOK, D_MODEL), F32), jax.ShapeDtypeStruct((N_TOK, D_MODEL), BF16)],
        compiler_params=_params(1, 42 * MIB),
        name="ffn",
    )(h2, wg, wu, wd, x1, mods, g2.reshape(1, D_MODEL), b2.reshape(1, D_MODEL), next_mods)


N_PAIRS = 2 * N_TOK
ROUTE_TILE = TOKEN_TILE
N_ROUTE_TILES = N_TOK // ROUTE_TILE
GROUP_ALIGN = 16
LOCAL_ROWS = 2 * ROUTE_TILE + N_EXPERTS * GROUP_ALIGN
N_SLOTS = N_PAIRS + N_ROUTE_TILES * N_EXPERTS * GROUP_ALIGN
SLOT_TILE = TOKEN_TILE
N_SLOT_TILES = N_SLOTS // SLOT_TILE
N_ITEMS = N_SLOT_TILES + N_EXPERTS
COPY_BITS = (ROUTE_TILE // GROUP_ALIGN).bit_length()
TAIL_EXPERT = N_EXPERTS

def _routing_tables(route, counts):
    n = counts.reshape(N_ROUTE_TILES, N_EXPERTS, LANES)[:, :, 0].astype(jnp.int32)
    g = (n + GROUP_ALIGN - 1) // GROUP_ALIGN * GROUP_ALIGN
    local_start = jnp.cumsum(g, axis=1) - g
    region = jnp.concatenate([jnp.zeros((1,), jnp.int32), jnp.cumsum(jnp.sum(g, axis=0))])
    global_start = region[None, :-1] + jnp.cumsum(g, axis=0) - g
    off = jnp.concatenate([region, jnp.full((1,), N_SLOTS, jnp.int32)])
    local = route[6:8].astype(jnp.int32)
    t0 = jnp.arange(N_SLOT_TILES, dtype=jnp.int32)[:, None] * SLOT_TILE
    live = jnp.maximum(off[None, :-1], t0) < jnp.minimum(off[None, 1:], t0 + SLOT_TILE)
    n_items = jnp.sum(live).astype(jnp.int32)
    order = jnp.nonzero(live.reshape(-1), size=N_ITEMS, fill_value=0)[0].astype(jnp.int32)
    order = jnp.where(jnp.arange(N_ITEMS) < n_items, order, order[n_items - 1])
    return dict(
        units=(g // GROUP_ALIGN).reshape(-1), local_start=local_start.reshape(-1), global_start=global_start.reshape(-1),
        used=region[-1:], off=off, n_items=n_items.reshape(1),
        item_tile=order // (N_EXPERTS + 1), item_expert=order % (N_EXPERTS + 1),
        local_by_lane=local, local_by_row=local.T, weight_by_lane=route[4:6])


def _for_each_chunk(units, fn):
    for b in range(COPY_BITS):
        @pl.when(((units >> b) & 1) == 1)
        def _():
            fn(pl.multiple_of((units & ((1 << b) - 1)) * GROUP_ALIGN, GROUP_ALIGN), GROUP_ALIGN << b)


def _group_copies(tile, units_ref, local_ref, global_ref, make, act):
    for e in range(N_EXPERTS):
        g = tile * N_EXPERTS + e

        def chunk(off, rows, g=g):
            act(make(pl.multiple_of(local_ref[g] + off, GROUP_ALIGN), pl.multiple_of(global_ref[g] + off, GROUP_ALIGN), rows))

        _for_each_chunk(units_ref[g], chunk)


def _dispatch_kernel(units_ref, local_ref, global_ref, used_ref, loc_ref, h_ref, xs_hbm, xl, zeros, sem):
    i = pl.program_id(0)
    cur = i % 2
    last = pl.num_programs(0) - 1
    slot = lax.broadcasted_iota(jnp.int32, (LOCAL_ROWS, ROUTE_TILE), 0)
    one_hot = jnp.where((slot == loc_ref[0:1, :]) | (slot == loc_ref[1:2, :]), 1.0, 0.0).astype(BF16)
    xl[cur] = jnp.dot(one_hot, h_ref[...], preferred_element_type=F32).astype(BF16)

    def copies(tile, b, act):
        make = lambda l, g, rows: pltpu.make_async_copy(xl.at[b, pl.ds(l, rows), :], xs_hbm.at[pl.ds(g, rows), :], sem.at[b])
        _group_copies(tile, units_ref, local_ref, global_ref, make, act)

    copies(i, cur, lambda c: c.start())

    @pl.when(i > 0)
    def _():
        copies(i - 1, 1 - cur, lambda c: c.wait())

    @pl.when(i == last)
    def _():
        copies(i, cur, lambda c: c.wait())

    @pl.when(i == last)
    def _():
        zeros[...] = jnp.zeros_like(zeros)
        used = used_ref[0]
        tail = (N_SLOTS - used) // GROUP_ALIGN
        small, n_big = tail % (SLOT_TILE // GROUP_ALIGN), tail // (SLOT_TILE // GROUP_ALIGN)
        big0 = used + small * GROUP_ALIGN

        fill_sem = sem.at[0]

        def fill(act):
            _for_each_chunk(small, lambda off, rows: act(pltpu.make_async_copy(
                zeros.at[pl.ds(0, rows), :], xs_hbm.at[pl.ds(pl.multiple_of(used + off, GROUP_ALIGN), rows), :], fill_sem)))
            for k in range((N_SLOTS - N_PAIRS) // SLOT_TILE):
                @pl.when(k < n_big)
                def _():
                    act(pltpu.make_async_copy(
                        zeros, xs_hbm.at[pl.ds(pl.multiple_of(big0 + k * SLOT_TILE, GROUP_ALIGN), SLOT_TILE), :], fill_sem))

        fill(lambda c: c.start())
        fill(lambda c: c.wait())


def _dispatch(h2, rt):
    return pl.pallas_call(
        _dispatch_kernel,
        grid_spec=pltpu.PrefetchScalarGridSpec(
            num_scalar_prefetch=4, grid=(N_ROUTE_TILES,),
            in_specs=[pl.BlockSpec((2, ROUTE_TILE), lambda i, *_: (0, i)),
                      pl.BlockSpec((ROUTE_TILE, D_MODEL), lambda i, *_: (i, 0))],
            out_specs=pl.BlockSpec(memory_space=pl.ANY),
            scratch_shapes=[pltpu.VMEM((2, LOCAL_ROWS, D_MODEL), BF16), pltpu.VMEM((SLOT_TILE, D_MODEL), BF16),
                            pltpu.SemaphoreType.DMA((2,))]),
        out_shape=jax.ShapeDtypeStruct((N_SLOTS, D_MODEL), BF16),
        compiler_params=_params(1),
        name="moe_dispatch",
    )(rt["units"], rt["local_start"], rt["global_start"], rt["used"], rt["local_by_lane"], h2)


def _moe_kernel(tile_ref, expert_ref, off_ref, n_ref, x_ref, wg_ref, wu_ref, wd_ref, y_ref):
    j = pl.program_id(0)

    @pl.when(j < n_ref[0])
    def _():
        t, e = tile_ref[j], expert_ref[j]
        lo, hi = off_ref[e] - t * SLOT_TILE, off_ref[e + 1] - t * SLOT_TILE
        opens_tile = (j == 0) | (tile_ref[jnp.maximum(j - 1, 0)] != t)
        is_tail = e == TAIL_EXPERT

        def run(keep_other_rows, compute):
            blocks = [slice(r, r + MERGE_ROWS) for r in range(0, SLOT_TILE, MERGE_ROWS)]
            zero_rows = jnp.zeros((MERGE_ROWS, D_MODEL), F32)
            expert = lambda rows: _swiglu(x_ref[rows, :], wg_ref, wu_ref, wd_ref) if compute else zero_rows
            f_next = expert(blocks[0])
            for b, rows in enumerate(blocks):
                f = f_next
                if b + 1 < len(blocks):
                    f_next = expert(blocks[b + 1])
                row = lax.broadcasted_iota(jnp.int32, (MERGE_ROWS, 1), 0) + rows.start
                mine = (row >= lo) & (row < hi)
                y_ref[rows, :] = jnp.where(mine, f, y_ref[rows, :] if keep_other_rows else 0.0)

        for keep in (False, True):
            for tail in (False, True):
                in_case = (jnp.logical_not(opens_tile) if keep else opens_tile) & (is_tail if tail else jnp.logical_not(is_tail))

                @pl.when(in_case)
                def _(keep=keep, tail=tail):
                    run(keep, not tail)


def _moe_ffn(x_sorted, rt, wg, wu, wd, index):
    d_ff = wg.shape[-1]
    rows = pl.BlockSpec((SLOT_TILE, D_MODEL), lambda j, it, ie, off, n: (it[j], 0))
    expert = lambda j, it, ie, off, n: (index, jnp.minimum(ie[j], N_EXPERTS - 1), 0, 0)
    w_in = pl.BlockSpec((None, None, D_MODEL, d_ff), expert)
    w_out = pl.BlockSpec((None, None, d_ff, D_MODEL), expert)
    return pl.pallas_call(
        _moe_kernel,
        grid_spec=pltpu.PrefetchScalarGridSpec(
            num_scalar_prefetch=4, grid=(N_ITEMS,),
            in_specs=[rows, w_in, w_in, w_out], out_specs=rows),
        out_shape=jax.ShapeDtypeStruct((N_SLOTS, D_MODEL), F32),
        compiler_params=_params(1),
        name="moe_ffn",
    )(rt["item_tile"], rt["item_expert"], rt["off"], rt["n_items"], x_sorted, wg, wu, wd)


def _combine_kernel(units_ref, local_ref, global_ref, loc_lane_ref, w_lane_ref, loc_row_ref, x_ref, mod_ref, g2_ref, b2_ref,
                    y_hbm, yp_ref, ys_ref, yl, sem):
    i = pl.program_id(0)
    cur = i % 2

    def gather(tile, b, act):
        make = lambda l, g, rows: pltpu.make_async_copy(y_hbm.at[pl.ds(g, rows), :], yl.at[b, pl.ds(l, rows), :], sem.at[b])
        _group_copies(tile, units_ref, local_ref, global_ref, make, act)

    @pl.when(i == 0)
    def _():
        yl[...] = jnp.zeros_like(yl)
        gather(0, 0, lambda c: c.start())

    @pl.when(i + 1 < pl.num_programs(0))
    def _():
        gather(i + 1, 1 - cur, lambda c: c.start())

    gather(i, cur, lambda c: c.wait())
    slot = lax.broadcasted_iota(jnp.int32, (LOCAL_ROWS, ROUTE_TILE), 0)
    gate = jnp.sum(jnp.where(slot == loc_lane_ref[0:1, :], w_lane_ref[0:1, :], 0.0)
                   + jnp.where(slot == loc_lane_ref[1:2, :], w_lane_ref[1:2, :], 0.0), axis=1, keepdims=True)
    y = yl[cur] * gate
    hi = y.astype(BF16)
    lo = (y - hi.astype(F32)).astype(BF16)
    slot_t = lax.broadcasted_iota(jnp.int32, (ROUTE_TILE, LOCAL_ROWS), 1)
    picks = jnp.where((slot_t == loc_row_ref[:, 0:1]) | (slot_t == loc_row_ref[:, 1:2]), 1.0, 0.0).astype(BF16)
    f = jnp.dot(picks, hi, preferred_element_type=F32) + jnp.dot(picks, lo, preferred_element_type=F32)
    x2 = _layer_norm(ALPHA * x_ref[...] + mod_ref[0, 5:6, :] * f, g2_ref[...], b2_ref[...])

    @pl.when(i < NP_TOK // ROUTE_TILE)
    def _():
        yp_ref[...] = x2

    @pl.when(i >= NP_TOK // ROUTE_TILE)
    def _():
        ys_ref[...] = x2


def _combine(y_sorted, rt, x1, mods, g2, b2):
    tm = ROUTE_TILE
    n_p = NP_TOK // tm
    vec = pl.BlockSpec((1, D_MODEL), lambda i, *_: (0, 0))
    lanes = pl.BlockSpec((2, tm), lambda i, *_: (0, i))
    return pl.pallas_call(
        _combine_kernel,
        grid_spec=pltpu.PrefetchScalarGridSpec(
            num_scalar_prefetch=3, grid=(N_ROUTE_TILES,),
            in_specs=[lanes, lanes, pl.BlockSpec((tm, 2), lambda i, *_: (i, 0)),
                      pl.BlockSpec((tm, D_MODEL), lambda i, *_: (i, 0)), _mod_spec(tm), vec, vec,
                      pl.BlockSpec(memory_space=pl.ANY)],
            out_specs=[pl.BlockSpec((tm, D_MODEL), lambda i, *_: (jnp.minimum(i, n_p - 1), 0)),
                       pl.BlockSpec((tm, D_MODEL), lambda i, *_: (jnp.maximum(i - n_p, 0), 0))],
            scratch_shapes=[pltpu.VMEM((2, LOCAL_ROWS, D_MODEL), F32), pltpu.SemaphoreType.DMA((2,))]),
        out_shape=[jax.ShapeDtypeStruct((NP_TOK, D_MODEL), F32), jax.ShapeDtypeStruct((NS_TOK, D_MODEL), F32)],
        compiler_params=_params(1),
        name="moe_combine",
    )(rt["units"], rt["local_start"], rt["global_start"], rt["local_by_lane"], rt["weight_by_lane"], rt["local_by_row"],
      x1, mods, g2.reshape(1, D_MODEL), b2.reshape(1, D_MODEL), y_sorted)


def kernel(x_prompt, x_sample, cache_k, cache_v, c, c_ctx, ln_in_g, ln_in_b, ada_w, ada_b, w_in, conv_w, conv_b, w_conv_out, lam_q1, lam_k1, lam_q2, lam_k2, subln_g, w_attn_out, w_out, ln1_g, ln1_b, ln2_g, ln2_b, ffn_w_gate, ffn_w_up, ffn_w_down, moe_router, moe_w_gate, moe_w_up, moe_w_down):
    assert DEPTH == 2
    cvec = jnp.concatenate([c, c_ctx[None, :], jnp.zeros((MOD_ROWS - DEC_BATCH - 1, D_MODEL), F32)], axis=0)
    mods = _ada(cvec, ada_w, ada_b).reshape(DEPTH, MOD_ROWS, 6, D_MODEL)
    tables = _rope_tables()

    x, h = _ln_in(x_prompt.reshape(NP_TOK, D_MODEL), x_sample.reshape(NS_TOK, D_MODEL), ln_in_g, ln_in_b, mods[0])
    caches = None
    for l in range(DEPTH):
        lam_init = 0.8 - 0.6 * math.exp(-0.3 * l)
        lam_vecs = jnp.stack([lam_q1[l], lam_k1[l], lam_q2[l], lam_k2[l]]).astype(F32)
        g_sub = subln_g[l].reshape(1, V_DIM)

        conv_y = _conv_branch(h, w_in, conv_w, conv_b, l)
        q, k, v, *caches = _qkv_prompt(h, w_in, l, caches)
        o_p = _attn_prompt(lam_vecs, g_sub, q, k, v, lam_init)
        q, k, v = _qkv_sample(h, w_in, tables, l)
        o_s = _attn_sample(lam_vecs, g_sub, q, k, v, cache_k, cache_v, l, lam_init)

        mix_w = (w_in, w_conv_out, w_attn_out, w_out, ln1_g, ln1_b, l)
        i = l // 2
        if l % 2 == 0:
            x1, h2 = _merge(h, conv_y, o_p, o_s, x, mods[l], *mix_w)
            x, h = _ffn(h2, ffn_w_gate.astype(BF16), ffn_w_up.astype(BF16), ffn_w_down.astype(BF16),
                        x1, mods[l], ln2_g[l], ln2_b[l], mods[l + 1], i)
        else:
            r_t = moe_router[i].T
            r_hi = r_t.astype(BF16)
            router = jnp.concatenate([r_hi, (r_t - r_hi.astype(F32)).astype(BF16)], axis=0)
            x1, h2, route, counts = _merge(h, conv_y, o_p, o_s, x, mods[l], *mix_w, router)
            rt = _routing_tables(route, counts)
            x_sorted = _dispatch(h2, rt)
            y_sorted = _moe_ffn(x_sorted, rt, moe_w_gate, moe_w_up, moe_w_down, i)
            y_p, y_s = _combine(y_sorted, rt, x1, mods[l], ln2_g[l], ln2_b[l])

    return (y_p.reshape(BATCH, SEQ, D_MODEL), y_s.reshape(DEC_BATCH, DEC_SEQ, D_MODEL), caches[0], caches[1])
```
